```python
import math
import jax, jax.numpy as jnp
from jax import lax
import numpy as np

D_MODEL = 1024
BATCH = 8
SEQ = 2048
DEPTH = 2

HEAD_DIM = 64
FOX_HEADS = 4
DIL_HEADS = 4
FOX_W = FOX_HEADS * HEAD_DIM
DIL_W = DIL_HEADS * HEAD_DIM
SSD_HEADS = 8
SSD_HEAD_DIM = 64
SSD_INNER = SSD_HEADS * SSD_HEAD_DIM
SSD_GROUPS = 2
SSD_HPG = SSD_HEADS // SSD_GROUPS
SSD_STATE = 128
SSD_CONV = 4
SSD_CHUNK = 128
SSD_CONV_CH = SSD_INNER + 2 * SSD_GROUPS * SSD_STATE
MIX_WIDTH = FOX_W + DIL_W + SSD_INNER
PROJ_SIZES = (FOX_W, FOX_W, FOX_W, FOX_HEADS, DIL_W, DIL_W, DIL_W, SSD_INNER, SSD_CONV_CH, SSD_HEADS)
IN_PROJ = sum(PROJ_SIZES)
Q_BLOCK = 128
DIL_CONFIGS = ((128, 1), (512, 4), (2048, 16))
ROPE_THETA = 500000.0
ROPE_DIM = HEAD_DIM // 4
MEM_LEN = 256
XA_HEADS = 4
XA_HEAD_DIM = D_MODEL // XA_HEADS
N_GROUPS = 4
EXPERTS_PER_GROUP = 8
N_EXPERTS = N_GROUPS * EXPERTS_PER_GROUP
TOP_K = 2
EXPERT_FF = 512
MOE_BLOCK = 128
EPS = 1e-6

kernel_name = "hybrid_fox_dilated_ssd_hmoe"


def rmsnorm(x, g):
    xf = x.astype(jnp.float32)
    y = xf * lax.rsqrt(jnp.mean(xf * xf, axis=-1, keepdims=True) + EPS)
    return (y * g.astype(jnp.float32)).astype(x.dtype)


def rotary_partial(t, pos):
    half = ROPE_DIM // 2
    inv = jnp.power(ROPE_THETA, -2.0 * jnp.arange(half, dtype=jnp.float32) / ROPE_DIM)
    ang = pos.astype(jnp.float32)[:, None] * inv[None, :]
    cos = jnp.cos(ang).astype(t.dtype)
    sin = jnp.sin(ang).astype(t.dtype)
    t1, t2, rest = t[..., :half], t[..., half:ROPE_DIM], t[..., ROPE_DIM:]
    return jnp.concatenate([t1 * cos - t2 * sin, t2 * cos + t1 * sin, rest], axis=-1)


def fox_attention(q, k, v, log_f):
    B, H, S, hd = q.shape
    nb = S // Q_BLOCK
    c = jnp.cumsum(log_f, axis=-1)
    scale = hd ** -0.5
    qb = q.reshape(B, H, nb, Q_BLOCK, hd).transpose(2, 0, 1, 3, 4)
    cb = c.reshape(B, H, nb, Q_BLOCK).transpose(2, 0, 1, 3)
    kpos = jnp.arange(S)

    def block(args):
        qi, ci, bi = args
        s = jnp.einsum("bhqd,bhkd->bhqk", qi, k).astype(jnp.float32) * scale
        s = s + ci[..., :, None] - c[..., None, :]
        qpos = bi * Q_BLOCK + jnp.arange(Q_BLOCK)
        s = jnp.where(kpos[None, :] <= qpos[:, None], s, -jnp.inf)
        p = jax.nn.softmax(s, axis=-1)
        return jnp.einsum("bhqk,bhkd->bhqd", p.astype(v.dtype), v)

    o = lax.map(block, (qb, cb, jnp.arange(nb)))
    return o.transpose(1, 2, 0, 3, 4).reshape(B, H, S, hd)


def dilated_attention(q, k, v):
    B, H, S, hd = q.shape
    scale = hd ** -0.5
    outs, maxes, dens = [], [], []
    for window, dil in DIL_CONFIGS:
        steps = window // dil
        L = S // dil
        nb = -(-L // steps)
        Lp = nb * steps

        def to_res(t):
            t = t.reshape(B, H, L, dil, hd).transpose(0, 1, 3, 2, 4)
            t = jnp.pad(t, ((0, 0), (0, 0), (0, 0), (0, Lp - L), (0, 0)))
            return t.reshape(B, H, dil, nb, steps, hd)

        def band(t):
            prev = jnp.pad(t, ((0, 0), (0, 0), (0, 0), (1, 0), (0, 0), (0, 0)))[:, :, :, :-1]
            return jnp.concatenate([prev, t], axis=4)

        def from_res(t):
            X = t.shape[-1]
            t = t.reshape(B, H, dil, Lp, X)[:, :, :, :L]
            return t.transpose(0, 1, 3, 2, 4).reshape(B, H, S, X)

        qr = to_res(q)
        kb = band(to_res(k))
        vb = band(to_res(v))
        s = jnp.einsum("bhrnqd,bhrnkd->bhrnqk", qr, kb).astype(jnp.float32) * scale
        qi = jnp.arange(steps)[:, None] + steps
        kj = jnp.arange(2 * steps)[None, :]
        dist = qi - kj
        blk = jnp.arange(nb)[:, None, None]
        valid = (dist >= 0) & (dist <= steps) & (blk * steps + kj[None] - steps >= 0)
        s = jnp.where(valid, s, -jnp.inf)
        m = jnp.max(s, axis=-1, keepdims=True)
        e = jnp.exp(s - m)
        den = jnp.sum(e, axis=-1, keepdims=True)
        o = jnp.einsum("bhrnqk,bhrnkd->bhrnqd", (e / den).astype(v.dtype), vb)
        outs.append(from_res(o).astype(jnp.float32))
        maxes.append(from_res(m))
        dens.append(from_res(den))
    ms = jnp.stack(maxes)
    ds = jnp.stack(dens)
    os_ = jnp.stack(outs)
    w = jnp.exp(ms - jnp.max(ms, axis=0, keepdims=True)) * ds
    o = jnp.sum(w * os_, axis=0) / jnp.sum(w, axis=0)
    return o.astype(q.dtype)


def ssd_mixer(z, xbc, dt_raw, conv_w, conv_b, dt_bias, A_log, D_skip, norm_g):
    B, S, _ = z.shape
    G, R, P, N, Q = SSD_GROUPS, SSD_HPG, SSD_HEAD_DIM, SSD_STATE, SSD_CHUNK
    nc = S // Q
    conv = lax.conv_general_dilated(
        xbc, conv_w.reshape(SSD_CONV, 1, SSD_CONV_CH).astype(xbc.dtype),
        window_strides=(1,), padding=((SSD_CONV - 1, 0),),
        dimension_numbers=("NWC", "WIO", "NWC"), feature_group_count=SSD_CONV_CH)
    xbc = jax.nn.silu(conv + conv_b)
    xs = xbc[..., :SSD_INNER].reshape(B, S, G, R, P)
    Bm = xbc[..., SSD_INNER:SSD_INNER + G * N].reshape(B, nc, Q, G, N)
    Cm = xbc[..., SSD_INNER + G * N:].reshape(B, nc, Q, G, N)
    dt = jax.nn.softplus(dt_raw.astype(jnp.float32) + dt_bias.astype(jnp.float32))
    A = -jnp.exp(A_log.astype(jnp.float32))
    dA = (dt * A).reshape(B, nc, Q, G, R).transpose(0, 3, 4, 1, 2)
    Xc = (xs * dt.reshape(B, S, G, R)[..., None]).reshape(B, nc, Q, G, R, P)
    A_cs = jnp.cumsum(dA, axis=-1)
    tri = jnp.tril(jnp.ones((Q, Q), dtype=bool))
    Lmat = jnp.exp(jnp.where(tri, A_cs[..., :, None] - A_cs[..., None, :], -jnp.inf))
    CB = jnp.einsum("bclgn,bcsgn->bcgls", Cm, Bm)
    y_diag = jnp.einsum("bcgls,bgrcls,bcsgrp->bclgrp", CB, Lmat, Xc)
    decay_states = jnp.exp(A_cs[..., -1:] - A_cs)
    states = jnp.einsum("bclgn,bgrcl,bclgrp->bcgrpn", Bm, decay_states, Xc)
    chunk_decay = jnp.exp(A_cs[..., -1])

    def step(h, inp):
        st, dec = inp
        return dec[..., None, None] * h + st, h

    h0 = jnp.zeros((B, G, R, P, N), states.dtype)
    _, h_in = lax.scan(step, h0, (jnp.moveaxis(states, 1, 0), jnp.moveaxis(chunk_decay, -1, 0)))
    h_in = jnp.moveaxis(h_in, 0, 1)
    y_off = jnp.einsum("bclgn,bcgrpn,bgrcl->bclgrp", Cm, h_in, jnp.exp(A_cs))
    y = (y_diag + y_off).reshape(B, S, G, R, P) + xs * D_skip.reshape(G, R, 1)
    y = y.reshape(B, S, SSD_INNER).astype(z.dtype)
    return rmsnorm(y * jax.nn.silu(z), norm_g)


def memory_cross_attention(h, mem_n, wq, wkv, qn_g, kn_g, wo):
    B, S, _ = h.shape
    q = rmsnorm((h @ wq).reshape(B, S, XA_HEADS, XA_HEAD_DIM), qn_g)
    k, v = jnp.split(mem_n @ wkv, 2, axis=-1)
    k = rmsnorm(k.reshape(B, MEM_LEN, XA_HEADS, XA_HEAD_DIM), kn_g)
    v = v.reshape(B, MEM_LEN, XA_HEADS, XA_HEAD_DIM)
    s = jnp.einsum("bshd,bmhd->bhsm", q, k).astype(jnp.float32) * (XA_HEAD_DIM ** -0.5)
    p = jax.nn.softmax(s, axis=-1)
    o = jnp.einsum("bhsm,bmhd->bshd", p.astype(v.dtype), v).reshape(B, S, D_MODEL)
    return o @ wo


def hierarchical_moe(h, wg, bg, we, be, w1, w3, w2):
    B, S, D = h.shape
    T = B * S
    xt = h.reshape(T, D)
    g_logits = (xt @ wg).astype(jnp.float32) + bg
    g_prob = jax.nn.softmax(g_logits, axis=-1)
    g_sel = jnp.argmax(g_logits, axis=-1)
    g_gate = jnp.take_along_axis(g_prob, g_sel[:, None], axis=1)
    e_logits = ((xt @ we).astype(jnp.float32) + be).reshape(T, N_GROUPS, EXPERTS_PER_GROUP)
    e_in = jnp.take_along_axis(e_logits, g_sel[:, None, None], axis=1)[:, 0]
    top_v, top_i = lax.top_k(e_in, TOP_K)
    gates = (jax.nn.softmax(top_v, axis=-1) * g_gate).reshape(T * TOP_K)
    eid = (g_sel[:, None] * EXPERTS_PER_GROUP + top_i).reshape(T * TOP_K)
    tok = jnp.repeat(jnp.arange(T), TOP_K)
    n_assign = T * TOP_K
    order = jnp.argsort(eid)
    s_eid, s_tok, s_gate = eid[order], tok[order], gates[order]
    counts = jnp.bincount(eid, length=N_EXPERTS)
    padded = (counts + MOE_BLOCK - 1) // MOE_BLOCK * MOE_BLOCK
    starts = jnp.cumsum(counts) - counts
    ends_p = jnp.cumsum(padded)
    pstarts = ends_p - padded
    dest = pstarts[s_eid] + jnp.arange(n_assign) - starts[s_eid]
    nblk = (n_assign + MOE_BLOCK - 1) // MOE_BLOCK + N_EXPERTS
    buf = jnp.zeros((nblk * MOE_BLOCK, D), xt.dtype).at[dest].set(xt[s_tok])
    blk_expert = jnp.minimum(jnp.searchsorted(ends_p, jnp.arange(nblk) * MOE_BLOCK, side="right"), N_EXPERTS - 1)

    def run(args):
        xb, e = args
        return (jax.nn.silu(xb @ w1[e]) * (xb @ w3[e])) @ w2[e]

    ybuf = lax.map(run, (buf.reshape(nblk, MOE_BLOCK, D), blk_expert)).reshape(nblk * MOE_BLOCK, D)
    y = jnp.zeros((T, D), xt.dtype).at[s_tok].add(ybuf[dest] * s_gate[:, None].astype(xt.dtype))
    return y.reshape(B, S, D)


def setup_inputs(seed: int = 0) -> dict:
    key = jax.random.key(seed)
    ks = iter(jax.random.split(key, 48))
    L = DEPTH

    def nrm(shape, s):
        return jax.random.normal(next(ks), shape, jnp.float32) * s

    def gain(n):
        return 1.0 + nrm((L, n), 0.02)

    x = nrm((BATCH, SEQ, D_MODEL), 1.0)
    mem = nrm((BATCH, MEM_LEN, D_MODEL), 1.0)
    norm1_g = gain(D_MODEL)
    w_in = nrm((L, D_MODEL, IN_PROJ), D_MODEL ** -0.5)
    fox_fgate_b = jax.random.uniform(next(ks), (L, FOX_HEADS), jnp.float32, 1.0, 4.0)
    fox_qn_g = gain(HEAD_DIM)
    fox_kn_g = gain(HEAD_DIM)
    dil_qn_g = gain(HEAD_DIM)
    dil_kn_g = gain(HEAD_DIM)
    ssd_conv_w = nrm((L, SSD_CONV, SSD_CONV_CH), SSD_CONV ** -0.5)
    ssd_conv_b = nrm((L, SSD_CONV_CH), 0.01)
    dt0 = jnp.exp(jax.random.uniform(next(ks), (L, SSD_HEADS), jnp.float32)
                  * (math.log(0.1) - math.log(0.001)) + math.log(0.001))
    ssd_dt_bias = dt0 + jnp.log(-jnp.expm1(-dt0))
    ssd_A_log = jnp.log(jax.random.uniform(next(ks), (L, SSD_HEADS), jnp.float32, 1.0, 16.0))
    ssd_D = 1.0 + nrm((L, SSD_HEADS), 0.1)
    ssd_norm_g = gain(SSD_INNER)
    w_out = nrm((L, MIX_WIDTH, D_MODEL), MIX_WIDTH ** -0.5)
    norm2_g = gain(D_MODEL)
    mem_norm_g = gain(D_MODEL)
    xa_wq = nrm((L, D_MODEL, D_MODEL), D_MODEL ** -0.5)
    xa_wkv = nrm((L, D_MODEL, 2 * D_MODEL), D_MODEL ** -0.5)
    xa_qn_g = gain(XA_HEAD_DIM)
    xa_kn_g = gain(XA_HEAD_DIM)
    xa_wo = nrm((L, D_MODEL, D_MODEL), D_MODEL ** -0.5)
    norm3_g = gain(D_MODEL)
    router_wg = nrm((L, D_MODEL, N_GROUPS), D_MODEL ** -0.5)
    router_bg = nrm((L, N_GROUPS), 0.01)
    router_we = nrm((L, D_MODEL, N_EXPERTS), D_MODEL ** -0.5)
    router_be = nrm((L, N_EXPERTS), 0.01)
    exp_w1 = nrm((L, N_EXPERTS, D_MODEL, EXPERT_FF), D_MODEL ** -0.5)
    exp_w3 = nrm((L, N_EXPERTS, D_MODEL, EXPERT_FF), D_MODEL ** -0.5)
    exp_w2 = nrm((L, N_EXPERTS, EXPERT_FF, D_MODEL), EXPERT_FF ** -0.5)
    return {"x": x, "mem": mem, "norm1_g": norm1_g, "w_in": w_in, "fox_fgate_b": fox_fgate_b,
            "fox_qn_g": fox_qn_g, "fox_kn_g": fox_kn_g, "dil_qn_g": dil_qn_g, "dil_kn_g": dil_kn_g,
            "ssd_conv_w": ssd_conv_w, "ssd_conv_b": ssd_conv_b, "ssd_dt_bias": ssd_dt_bias,
            "ssd_A_log": ssd_A_log, "ssd_D": ssd_D, "ssd_norm_g": ssd_norm_g, "w_out": w_out,
            "norm2_g": norm2_g, "mem_norm_g": mem_norm_g, "xa_wq": xa_wq, "xa_wkv": xa_wkv,
            "xa_qn_g": xa_qn_g, "xa_kn_g": xa_kn_g, "xa_wo": xa_wo, "norm3_g": norm3_g,
            "router_wg": router_wg, "router_bg": router_bg, "router_we": router_we,
            "router_be": router_be, "exp_w1": exp_w1, "exp_w3": exp_w3, "exp_w2": exp_w2}


def reference(x, mem, norm1_g, w_in, fox_fgate_b, fox_qn_g, fox_kn_g, dil_qn_g, dil_kn_g,
              ssd_conv_w, ssd_conv_b, ssd_dt_bias, ssd_A_log, ssd_D, ssd_norm_g, w_out,
              norm2_g, mem_norm_g, xa_wq, xa_wkv, xa_qn_g, xa_kn_g, xa_wo, norm3_g,
              router_wg, router_bg, router_we, router_be, exp_w1, exp_w3, exp_w2):
    B, S, _ = x.shape
    pos = jnp.arange(S)
    split_idx = np.cumsum(PROJ_SIZES)[:-1].tolist()

    def heads(t, n):
        return t.reshape(B, S, n, -1).transpose(0, 2, 1, 3)

    for l in range(DEPTH):
        h = rmsnorm(x, norm1_g[l])
        (fq, fk, fv, ff, dq, dk, dv, z, xbc, dt_raw) = jnp.split(h @ w_in[l], split_idx, axis=-1)
        fq = rmsnorm(heads(fq, FOX_HEADS), fox_qn_g[l])
        fk = rmsnorm(heads(fk, FOX_HEADS), fox_kn_g[l])
        log_f = jax.nn.log_sigmoid(ff.astype(jnp.float32) + fox_fgate_b[l].astype(jnp.float32)).transpose(0, 2, 1)
        o_fox = fox_attention(fq, fk, heads(fv, FOX_HEADS), log_f)
        dq = rotary_partial(rmsnorm(heads(dq, DIL_HEADS), dil_qn_g[l]), pos)
        dk = rotary_partial(rmsnorm(heads(dk, DIL_HEADS), dil_kn_g[l]), pos)
        o_dil = dilated_attention(dq, dk, heads(dv, DIL_HEADS))
        o_ssd = ssd_mixer(z, xbc, dt_raw, ssd_conv_w[l], ssd_conv_b[l], ssd_dt_bias[l],
                          ssd_A_log[l], ssd_D[l], ssd_norm_g[l])
        mix = jnp.concatenate([o_fox.transpose(0, 2, 1, 3).reshape(B, S, FOX_W),
                               o_dil.transpose(0, 2, 1, 3).reshape(B, S, DIL_W),
                               o_ssd.astype(x.dtype)], axis=-1)
        x = x + mix @ w_out[l]
        h = rmsnorm(x, norm2_g[l])
        m = rmsnorm(mem, mem_norm_g[l])
        x = x + memory_cross_attention(h, m, xa_wq[l], xa_wkv[l], xa_qn_g[l], xa_kn_g[l], xa_wo[l])
        h = rmsnorm(x, norm3_g[l])
        x = x + hierarchical_moe(h, router_wg[l], router_bg[l], router_we[l], router_be[l],
                                 exp_w1[l], exp_w3[l], exp_w2[l])
    return x
```

```python
import functools

import jax
import jax.numpy as jnp
import numpy as np
from jax import lax
from jax.experimental import pallas as pl
from jax.experimental.pallas import tpu as pltpu

F32 = jnp.float32
BF16 = jnp.bfloat16

D_MODEL = 1024
HEAD_DIM = 64
ATT_HEADS = 4
ATT_W = ATT_HEADS * HEAD_DIM
SSD_HEADS = 8
SSD_INNER = 512
SSD_STATE = 128
SSD_CONV = 4
SSD_CHUNK = 128
SSD_CONV_CH = 1024
XA_HEADS = 4
XA_HEAD_DIM = 256
MEM_LEN = 256
N_GROUPS = 4
EXPERTS_PER_GROUP = 8
N_EXPERTS = 32
EXPERT_FF = 512
DIL_CONFIGS = ((128, 1), (512, 4), (2048, 16))
ROPE_THETA = 500000.0
ROPE_DIM = 16
EPS = 1e-6
NEG = -1e30

LANES = 128
N_FGATE = 4
DT_LANE0 = 4
MAIN_W = 3 * ATT_W + 3 * ATT_W + SSD_INNER + SSD_CONV_CH
IN_W = MAIN_W + LANES

TM_IN = 512
TQ = 256
TK = 256
TM_MID = 256
TM_MOE = 256
TD = 512
TC = 512
VMEM_LIMIT = 48 * 1024 * 1024


def _cparams(n_axes):
    return pltpu.CompilerParams(dimension_semantics=("arbitrary",) * n_axes,
                                vmem_limit_bytes=VMEM_LIMIT)


def _rms(x, g):
    return x * lax.rsqrt(jnp.mean(x * x, axis=-1, keepdims=True) + EPS) * g


def _split3(x):
    hi = x.astype(BF16)
    r = x - hi.astype(F32)
    mid = r.astype(BF16)
    lo = (r - mid.astype(F32)).astype(BF16)
    return jnp.concatenate([hi, mid, lo], axis=1)


def _dot(a, b):
    return jnp.dot(a, b, preferred_element_type=F32)


def _dot_nt(a, b):
    return lax.dot_general(a, b, (((1,), (1,)), ((), ())), preferred_element_type=F32)


def _sum3(c):
    w = c.shape[1] // 3
    return c[:, 0:w] + c[:, w:2 * w] + c[:, 2 * w:3 * w]


def _inproj_kernel(x_ref, g_ref, w_ref, gm_ref, qkg_ref, rope_ref, sb_ref,
                   fq_ref, fkt_ref, fv_ref, dq_ref, dkt_ref, dv_ref, z_ref, xbc_ref, sm_ref):
    h = _rms(x_ref[...], g_ref[...]).astype(BF16)

    def proj(a, b):
        return _dot(h, w_ref[:, a:b])

    def head_norm(a, idx):
        ssq = _dot((a * a).astype(BF16), gm_ref[...])
        return a * lax.rsqrt(ssq * (1.0 / HEAD_DIM) + EPS) * qkg_ref[idx]

    def rope(a):
        return (a * rope_ref[0] + pltpu.roll(a, ATT_W - ROPE_DIM // 2, 1) * rope_ref[1]
                + pltpu.roll(a, ROPE_DIM // 2, 1) * rope_ref[2])

    scale = HEAD_DIM ** -0.5
    fq_ref[...] = (head_norm(proj(0, 256), 0) * scale).astype(BF16)
    fkt_ref[0] = head_norm(proj(256, 512), 1).T.astype(BF16)
    fv_ref[...] = proj(512, 768).astype(BF16)
    dq_ref[...] = (rope(head_norm(proj(768, 1024), 2)) * scale).astype(BF16)
    dkt_ref[0] = rope(head_norm(proj(1024, 1280), 3)).T.astype(BF16)
    dv_ref[...] = proj(1280, 1536).astype(BF16)
    z_ref[...] = proj(1536, 2048)
    xbc_ref[...] = proj(2048, MAIN_W)
    v = proj(MAIN_W, IN_W) + sb_ref[...]
    e = jnp.log1p(jnp.exp(-jnp.abs(v)))
    lane = lax.broadcasted_iota(jnp.int32, v.shape, 1)
    sm_ref[...] = jnp.where(lane < N_FGATE, jnp.minimum(v, 0.0) - e, jnp.maximum(v, 0.0) + e)


def _inproj(x2d, g, w, gm, qkg, rope, sb, B, S):
    T = x2d.shape[0]
    nst = S // TM_IN
    row = lambda i: (i, 0)
    const2 = lambda i: (0, 0)
    tr = lambda i: (i // nst, 0, i % nst)
    out_shape = [
        jax.ShapeDtypeStruct((T, ATT_W), BF16),
        jax.ShapeDtypeStruct((B, ATT_W, S), BF16),
        jax.ShapeDtypeStruct((T, ATT_W), BF16),
        jax.ShapeDtypeStruct((T, ATT_W), BF16),
        jax.ShapeDtypeStruct((B, ATT_W, S), BF16),
        jax.ShapeDtypeStruct((T, ATT_W), BF16),
        jax.ShapeDtypeStruct((T, SSD_INNER), F32),
        jax.ShapeDtypeStruct((T, SSD_CONV_CH), F32),
        jax.ShapeDtypeStruct((T, LANES), F32),
    ]
    att = pl.BlockSpec((TM_IN, ATT_W), row)
    att_t = pl.BlockSpec((1, ATT_W, TM_IN), tr)
    return pl.pallas_call(
        _inproj_kernel,
        grid=(T // TM_IN,),
        in_specs=[
            pl.BlockSpec((TM_IN, D_MODEL), row),
            pl.BlockSpec((1, D_MODEL), const2),
            pl.BlockSpec((D_MODEL, IN_W), const2),
            pl.BlockSpec((ATT_W, ATT_W), const2),
            pl.BlockSpec((4, 1, ATT_W), lambda i: (0, 0, 0)),
            pl.BlockSpec((3, TM_IN, ATT_W), lambda i: (0, i % nst, 0)),
            pl.BlockSpec((1, LANES), const2),
        ],
        out_specs=[att, att_t, att, att, att_t, att,
                   pl.BlockSpec((TM_IN, SSD_INNER), row),
                   pl.BlockSpec((TM_IN, SSD_CONV_CH), row),
                   pl.BlockSpec((TM_IN, LANES), row)],
        out_shape=out_shape,
        compiler_params=_cparams(1),
        name="inproj",
    )(x2d, g, w, gm, qkg, rope, sb)


def _fox_scan_kernel(sm_ref, tri_ref, pq_ref, pk_ref, oq_ref, ok_ref, augq_ref, augkt_ref):
    S = sm_ref.shape[1]
    blk = tri_ref.shape[0]
    carry = jnp.zeros((1, LANES), F32)
    for b in range(S // blk):
        rows = slice(b * blk, (b + 1) * blk)
        c = _sum3(_dot(tri_ref[...], _split3(sm_ref[0, rows, :]))) + carry
        carry = c[blk - 1:blk, :]
        c3 = _split3(c)
        for h in range(ATT_HEADS):
            cols = slice(LANES * h, LANES * (h + 1))
            augq_ref[0, rows, cols] = (_dot(c3, pq_ref[h]) + oq_ref[h]).astype(BF16)
            augkt_ref[0, cols, rows] = (_dot(c3, pk_ref[h]) + ok_ref[h]).T.astype(BF16)


def _fox_scan(small, consts, B, S):
    tri, pq, pk, oq, ok = consts
    c2 = lambda b: (0, 0)
    c3 = lambda b: (0, 0, 0)
    return pl.pallas_call(
        _fox_scan_kernel,
        grid=(B,),
        in_specs=[
            pl.BlockSpec((1, S, LANES), lambda b: (b, 0, 0)),
            pl.BlockSpec(tri.shape, c2),
            pl.BlockSpec(pq.shape, c3),
            pl.BlockSpec(pk.shape, c3),
            pl.BlockSpec(oq.shape, c3),
            pl.BlockSpec(ok.shape, c3),
        ],
        out_specs=[pl.BlockSpec((1, S, ATT_HEADS * LANES), lambda b: (b, 0, 0)),
                   pl.BlockSpec((1, ATT_HEADS * LANES, S), lambda b: (b, 0, 0))],
        out_shape=[jax.ShapeDtypeStruct((B, S, ATT_HEADS * LANES), BF16),
                   jax.ShapeDtypeStruct((B, ATT_HEADS * LANES, S), BF16)],
        compiler_params=_cparams(1),
        name="fox_scan",
    )(small.reshape(B, S, LANES), tri, pq, pk, oq, ok)


def _attn_kernel(*refs, fox, nk):
    if fox:
        q_ref, kt_ref, v_ref, lm_ref, augq_ref, augkt_ref, o_ref, kt_scr, v_scr = refs
    else:
        q_ref, kt_ref, v_ref, lm_ref, o_ref, kt_scr, v_scr = refs
    qi = pl.program_id(1)

    @pl.when(qi == 0)
    def _prep():
        row = lax.broadcasted_iota(jnp.int32, (LANES, TK), 0)
        lane = lax.broadcasted_iota(jnp.int32, (TK, LANES), 1)
        for h in range(ATT_HEADS):
            p, mem = divmod(h, 2)
            pair = slice(LANES * p, LANES * (p + 1))
            for j in range(nk):
                keys = slice(j * TK, (j + 1) * TK)
                kd = kt_ref[0, pair, keys]
                if fox:
                    other = augkt_ref[0, LANES * h:LANES * (h + 1), keys]
                else:
                    other = jnp.zeros_like(kd)
                kt_scr[h, j] = jnp.where((row >> 6) == mem, kd, other)
                vd = v_ref[keys, pair]
                v_scr[h, j] = jnp.where((lane >> 6) == mem, vd, jnp.ones_like(vd))

    qlane = lax.broadcasted_iota(jnp.int32, (TQ, LANES), 1)
    outs = []
    for h in range(ATT_HEADS):
        p, mem = divmod(h, 2)
        qd = q_ref[:, LANES * p:LANES * (p + 1)]
        if fox:
            other = augq_ref[:, LANES * h:LANES * (h + 1)]
        else:
            other = jnp.zeros_like(qd)
        qa = jnp.where((qlane >> 6) == mem, qd, other)

        def block(j, carry, table):
            m, acc = carry
            s = _dot(qa, kt_scr[h, j])
            if table is not None:
                s = s + table
            m_new = jnp.maximum(m, jnp.max(s, axis=1, keepdims=True))
            alpha = jnp.exp(m - m_new)
            pr = jnp.exp(s - m_new).astype(BF16)
            return m_new, alpha * acc + _dot(pr, v_scr[h, j])

        init = (jnp.full((TQ, 1), NEG, F32), jnp.zeros((TQ, LANES), F32))
        if fox:
            carry = lax.fori_loop(0, qi, lambda j, c: block(j, c, None), init)
            _, acc = block(qi, carry, lm_ref[0])
        else:
            _, acc = lax.fori_loop(0, qi + 1, lambda j, c: block(j, c, lm_ref[qi - j]), init)
        outs.append(acc / pltpu.roll(acc, HEAD_DIM, 1))
    for p in range(ATT_HEADS // 2):
        o_ref[:, LANES * p:LANES * (p + 1)] = jnp.where(
            (qlane >> 6) == 0, outs[2 * p], outs[2 * p + 1]).astype(BF16)


def _attention(q, kt, v, lm, aug, B, S):
    fox = aug is not None
    nq, nk = S // TQ, S // TK
    in_specs = [
        pl.BlockSpec((TQ, ATT_W), lambda b, i: (b * nq + i, 0)),
        pl.BlockSpec((1, ATT_W, S), lambda b, i: (b, 0, 0)),
        pl.BlockSpec((S, ATT_W), lambda b, i: (b, 0)),
        pl.BlockSpec(lm.shape, lambda b, i: (0, 0, 0)),
    ]
    args = [q, kt, v, lm]
    if fox:
        in_specs += [pl.BlockSpec((TQ, ATT_HEADS * LANES), lambda b, i: (b * nq + i, 0)),
                     pl.BlockSpec((1, ATT_HEADS * LANES, S), lambda b, i: (b, 0, 0))]
        args += [aug[0].reshape(B * S, ATT_HEADS * LANES), aug[1]]
    return pl.pallas_call(
        functools.partial(_attn_kernel, fox=fox, nk=nk),
        grid=(B, nq),
        in_specs=in_specs,
        out_specs=pl.BlockSpec((TQ, ATT_W), lambda b, i: (b * nq + i, 0)),
        out_shape=jax.ShapeDtypeStruct((B * S, ATT_W), BF16),
        scratch_shapes=[pltpu.VMEM((ATT_HEADS, nk, LANES, TK), BF16),
                        pltpu.VMEM((ATT_HEADS, nk, TK, LANES), BF16)],
        compiler_params=_cparams(2),
        name="fox_attn" if fox else "dil_attn",
    )(*args)


def _ssd_kernel(xbc_ref, z_ref, sm_ref, cw_ref, cb_ref, arow_ref, dx_ref, ng_ref,
                tri_ref, pexp_ref, pq_ref, pk_ref, oq_ref, ok_ref, o_ref, buf, state):
    Q = SSD_CHUNK

    @pl.when(pl.program_id(1) == 0)
    def _reset():
        buf[0:8, :] = jnp.zeros((8, SSD_CONV_CH), F32)
        state[...] = jnp.zeros(state.shape, F32)

    xb = xbc_ref[...]
    buf[8:8 + Q, :] = xb
    conv = cb_ref[...]
    for k in range(SSD_CONV):
        off = 8 - (SSD_CONV - 1) + k
        conv = conv + cw_ref[k:k + 1, :] * buf[off:off + Q, :]
    buf[0:8, :] = xb[Q - 8:Q, :]
    act = conv * jax.nn.sigmoid(conv)
    xs = act[:, 0:SSD_INNER]
    bm = act[:, SSD_INNER:SSD_INNER + 2 * SSD_STATE]
    cm = act[:, SSD_INNER + 2 * SSD_STATE:]

    dt = sm_ref[...]
    acs = _sum3(_dot(tri_ref[...], _split3(dt * arow_ref[...])))
    acs3 = _split3(acs)
    ax = _dot(acs3, pexp_ref[...])
    dtx = _dot(_split3(dt), pexp_ref[...])
    last = ax[Q - 1:Q, :]
    ea = jnp.exp(ax)
    cdec = jnp.exp(last)
    xc = xs * dtx
    xcb = xc.astype(BF16)
    xcd = (xc * jnp.exp(last - ax)).astype(BF16)
    uq = (_dot(acs3, pq_ref[...]) + oq_ref[...]).astype(BF16)
    uk = (_dot(acs3, pk_ref[...]) + ok_ref[...]).astype(BF16)

    tril = (lax.broadcasted_iota(jnp.int32, (Q, Q), 0) >= lax.broadcasted_iota(jnp.int32, (Q, Q), 1))
    first = lax.broadcasted_iota(jnp.int32, (Q, LANES), 1) < HEAD_DIM
    ys = []
    for g in range(2):
        gs = slice(SSD_STATE * g, SSD_STATE * (g + 1))
        bg = bm[:, gs]
        cg = cm[:, gs].astype(BF16)
        cbm = _dot_nt(cg, bg.astype(BF16))
        bgt = bg.T.astype(BF16)
        for pp in range(2):
            p = 2 * g + pp
            ps = slice(LANES * p, LANES * (p + 1))
            ms = []
            for mem in range(2):
                hs = slice(LANES * (2 * p + mem), LANES * (2 * p + mem + 1))
                dm = _dot_nt(uq[:, hs], uk[:, hs])
                ms.append((cbm * jnp.exp(jnp.where(tril, dm, NEG))).astype(BF16))
            xp = xcb[:, ps]
            zero = jnp.zeros_like(xp)
            xcat = jnp.concatenate([jnp.where(first, xp, zero), jnp.where(first, zero, xp)], axis=0)
            y_diag = _dot(jnp.concatenate(ms, axis=1), xcat)
            st = state[p]
            y_off = _dot(cg, st.astype(BF16)) * ea[:, ps]
            state[p] = cdec[:, ps] * st + _dot(bgt, xcd[:, ps])
            ys.append(y_diag + y_off + xs[:, ps] * dx_ref[:, ps])
    y = jnp.concatenate(ys, axis=1)
    zz = z_ref[...]
    o_ref[...] = _rms(y * (zz * jax.nn.sigmoid(zz)), ng_ref[...]).astype(BF16)


def _ssd(xbc, z, small, cw, cb, arow, dx, ng, consts, B, S):
    nc = S // SSD_CHUNK
    row = lambda b, c: (b * nc + c, 0)
    c2 = lambda b, c: (0, 0)
    full = lambda a: pl.BlockSpec(a.shape, c2)
    return pl.pallas_call(
        _ssd_kernel,
        grid=(B, nc),
        in_specs=[pl.BlockSpec((SSD_CHUNK, SSD_CONV_CH), row),
                  pl.BlockSpec((SSD_CHUNK, SSD_INNER), row),
                  pl.BlockSpec((SSD_CHUNK, LANES), row),
                  full(cw), full(cb), full(arow), full(dx), full(ng)] + [full(a) for a in consts],
        out_specs=pl.BlockSpec((SSD_CHUNK, SSD_INNER), row),
        out_shape=jax.ShapeDtypeStruct((B * S, SSD_INNER), BF16),
        scratch_shapes=[pltpu.VMEM((8 + SSD_CHUNK, SSD_CONV_CH), F32),
                        pltpu.VMEM((SSD_HEADS // 2, SSD_STATE, LANES), F32)],
        compiler_params=_cparams(2),
        name="ssd",
    )(xbc, z, small, cw, cb, arow, dx, ng, *consts)


def _kv_kernel(mem_ref, g_ref, w_ref, kg_ref, kt_ref, v_ref):
    m = _rms(mem_ref[0], g_ref[...]).astype(BF16)
    kv = _dot(m, w_ref[...])
    for h in range(XA_HEADS):
        hs = slice(XA_HEAD_DIM * h, XA_HEAD_DIM * (h + 1))
        kt_ref[0, hs, :] = _rms(kv[:, hs], kg_ref[...]).T.astype(BF16)
    v_ref[0] = kv[:, D_MODEL:].astype(BF16)


def _kv(mem, g, w, kg):
    B = mem.shape[0]
    c2 = lambda b: (0, 0)
    return pl.pallas_call(
        _kv_kernel,
        grid=(B,),
        in_specs=[pl.BlockSpec((1, MEM_LEN, D_MODEL), lambda b: (b, 0, 0)),
                  pl.BlockSpec((1, D_MODEL), c2),
                  pl.BlockSpec((D_MODEL, 2 * D_MODEL), c2),
                  pl.BlockSpec((1, XA_HEAD_DIM), c2)],
        out_specs=[pl.BlockSpec((1, D_MODEL, MEM_LEN), lambda b: (b, 0, 0)),
                   pl.BlockSpec((1, MEM_LEN, D_MODEL), lambda b: (b, 0, 0))],
        out_shape=[jax.ShapeDtypeStruct((B, D_MODEL, MEM_LEN), BF16),
                   jax.ShapeDtypeStruct((B, MEM_LEN, D_MODEL), BF16)],
        compiler_params=_cparams(1),
        name="mem_kv",
    )(mem, g, w, kg)


def _mid_kernel(x_ref, of_ref, od_ref, os_ref, wo_ref, g2_ref, wq_ref, qg_ref, kt_ref, v_ref,
                wxo_ref, g3_ref, wr_ref, rb_ref, tri_ref, x2_ref, ri_ref, cnt_ref, run):
    @pl.when(pl.program_id(0) == 0)
    def _reset():
        run[...] = jnp.zeros(run.shape, F32)

    x1 = (x_ref[...] + _dot(of_ref[...], wo_ref[0:ATT_W, :])
          + _dot(od_ref[...], wo_ref[ATT_W:2 * ATT_W, :])
          + _dot(os_ref[...], wo_ref[2 * ATT_W:, :]))

    q = _dot(_rms(x1, g2_ref[...]).astype(BF16), wq_ref[...])
    heads = []
    for h in range(XA_HEADS):
        hs = slice(XA_HEAD_DIM * h, XA_HEAD_DIM * (h + 1))
        qn = (_rms(q[:, hs], qg_ref[...]) * XA_HEAD_DIM ** -0.5).astype(BF16)
        s = _dot(qn, kt_ref[0, hs, :])
        e = jnp.exp(s - jnp.max(s, axis=1, keepdims=True))
        o = _dot(e.astype(BF16), v_ref[0, :, hs]) / jnp.sum(e, axis=1, keepdims=True)
        heads.append(o.astype(BF16))
    x2 = x1 + _dot(jnp.concatenate(heads, axis=1), wxo_ref[...])
    x2_ref[...] = x2

    h3 = _split3(_rms(x2, g3_ref[...]))
    hi, mid, lo = h3[:, 0:D_MODEL], h3[:, D_MODEL:2 * D_MODEL], h3[:, 2 * D_MODEL:]
    logits = (_dot(mid, wr_ref[1]) + _dot(hi, wr_ref[2]) + _dot(lo, wr_ref[0])
              + _dot(hi, wr_ref[1]) + _dot(mid, wr_ref[0]) + _dot(hi, wr_ref[0]) + rb_ref[...])

    lane = lax.broadcasted_iota(jnp.int32, logits.shape, 1)
    lanef = lane.astype(F32)
    big = float(LANES)

    def first_max(vals):
        top = jnp.max(vals, axis=1, keepdims=True)
        return top, jnp.min(jnp.where(vals == top, lanef, big), axis=1, keepdims=True)

    gl = jnp.where(lane < N_GROUPS, logits, NEG)
    gmax, gsel = first_max(gl)
    ggate = 1.0 / jnp.sum(jnp.exp(gl - gmax), axis=1, keepdims=True)
    grp = ((lane - N_GROUPS) >> 3).astype(F32)
    el = jnp.where(grp == gsel, logits, NEG)
    v1, i1 = first_max(el)
    v2, i2 = first_max(jnp.where(lanef == i1, NEG, el))
    t = jnp.exp(v2 - v1)
    p1 = 1.0 / (1.0 + t)
    e1 = i1 - N_GROUPS
    e2 = i2 - N_GROUPS

    hit1 = lanef == e1
    hit2 = lanef == e2
    onehot = jnp.where(hit1 | hit2, 1.0, 0.0)
    before = _dot(tri_ref[...], onehot.astype(BF16)) + run[...]
    r1 = jnp.sum(jnp.where(hit1, before, 0.0), axis=1, keepdims=True)
    r2 = jnp.sum(jnp.where(hit2, before, 0.0), axis=1, keepdims=True)
    run[...] = run[...] + jnp.sum(onehot, axis=0, keepdims=True)

    cols = (e1, e2, p1 * ggate, t * p1 * ggate, r1, r2)
    info = jnp.zeros(logits.shape, F32)
    for k, col in enumerate(cols):
        info = jnp.where(lane == k, col, info)
    ri_ref[...] = info
    cnt_ref[...] = jnp.broadcast_to(run[...], cnt_ref.shape)


def _mid(x2d, o_fox, o_dil, o_ssd, wo, g2, wq, qg, kt, v, wxo, g3, wr, rb, tri, S):
    T = x2d.shape[0]
    npb = S // TM_MID
    row = lambda i: (i, 0)
    c2 = lambda i: (0, 0)
    c3 = lambda i: (0, 0, 0)
    return pl.pallas_call(
        _mid_kernel,
        grid=(T // TM_MID,),
        in_specs=[pl.BlockSpec((TM_MID, D_MODEL), row),
                  pl.BlockSpec((TM_MID, ATT_W), row),
                  pl.BlockSpec((TM_MID, ATT_W), row),
                  pl.BlockSpec((TM_MID, SSD_INNER), row),
                  pl.BlockSpec((D_MODEL, D_MODEL), c2),
                  pl.BlockSpec((1, D_MODEL), c2),
                  pl.BlockSpec((D_MODEL, D_MODEL), c2),
                  pl.BlockSpec((1, XA_HEAD_DIM), c2),
                  pl.BlockSpec((1, D_MODEL, MEM_LEN), lambda i: (i // npb, 0, 0)),
                  pl.BlockSpec((1, MEM_LEN, D_MODEL), lambda i: (i // npb, 0, 0)),
                  pl.BlockSpec((D_MODEL, D_MODEL), c2),
                  pl.BlockSpec((1, D_MODEL), c2),
                  pl.BlockSpec((3, D_MODEL, LANES), c3),
                  pl.BlockSpec((1, LANES), c2),
                  pl.BlockSpec((TM_MID, TM_MID), c2)],
        out_specs=[pl.BlockSpec((TM_MID, D_MODEL), row),
                   pl.BlockSpec((TM_MID, LANES), row),
                   pl.BlockSpec((8, LANES), c2)],
        out_shape=[jax.ShapeDtypeStruct((T, D_MODEL), F32),
                   jax.ShapeDtypeStruct((T, LANES), F32),
                   jax.ShapeDtypeStruct((8, LANES), F32)],
        scratch_shapes=[pltpu.VMEM((1, LANES), F32)],
        compiler_params=_cparams(1),
        name="mid",
    )(x2d, o_fox, o_dil, o_ssd, wo, g2, wq, qg, kt, v, wxo, g3, wr, rb, tri)


def _row_copy(src, dst, sem, src_row, dst_row):
    return pltpu.make_async_copy(src.at[pl.ds(src_row, 1)], dst.at[pl.ds(dst_row, 1)], sem)


def _dispatch_kernel(dest_ref, x_hbm, xs_in, xs_out, sem):
    del xs_in
    base = pl.program_id(0) * TD

    def issue(r, carry):
        for k in range(2):
            _row_copy(x_hbm, xs_out, sem, base + r, dest_ref[2 * r + k]).start()
        return carry

    def drain(r, carry):
        for k in range(2):
            _row_copy(x_hbm, xs_out, sem, 0, 0).wait()
        return carry

    lax.fori_loop(0, TD, issue, 0)
    lax.fori_loop(0, TD, drain, 0)


def _dispatch(dest, x2, xs_zero):
    T = x2.shape[0]
    return pl.pallas_call(
        _dispatch_kernel,
        grid=(T // TD,),
        in_specs=[pl.BlockSpec((2 * TD,), lambda i: (i,), memory_space=pltpu.SMEM),
                  pl.BlockSpec(memory_space=pl.ANY),
                  pl.BlockSpec(memory_space=pl.ANY)],
        out_specs=pl.BlockSpec(memory_space=pl.ANY),
        out_shape=jax.ShapeDtypeStruct(xs_zero.shape, F32),
        scratch_shapes=[pltpu.SemaphoreType.DMA],
        input_output_aliases={2: 0},
        compiler_params=_cparams(1),
        name="moe_dispatch",
    )(dest, x2, xs_zero)


def _expert_kernel(be_ref, nu_ref, xs_ref, g3_ref, w1_ref, w3_ref, w2_ref, y_ref, w1s, w3s, w2s):
    i = pl.program_id(0)
    fresh = (i == 0) | (be_ref[i] != be_ref[jnp.maximum(i - 1, 0)])

    @pl.when(fresh)
    def _cast():
        w1s[...] = w1_ref[...].astype(BF16)
        w3s[...] = w3_ref[...].astype(BF16)
        w2s[...] = w2_ref[...].astype(BF16)

    @pl.when(i < nu_ref[0])
    def _run():
        xb = _rms(xs_ref[...], g3_ref[...]).astype(BF16)
        a = _dot(xb, w1s[...])
        b = _dot(xb, w3s[...])
        y_ref[...] = _dot((a * jax.nn.sigmoid(a) * b).astype(BF16), w2s[...])

    @pl.when(i >= nu_ref[0])
    def _skip():
        y_ref[...] = jnp.zeros(y_ref.shape, F32)


def _experts(blk_expert, n_used, xs, g3, w1, w3, w2, layer):
    nblk = xs.shape[0] // TM_MOE
    wmap = lambda i, be, nu: (layer, be[i], 0, 0)
    return pl.pallas_call(
        _expert_kernel,
        grid_spec=pltpu.PrefetchScalarGridSpec(
            num_scalar_prefetch=2,
            grid=(nblk,),
            in_specs=[pl.BlockSpec((TM_MOE, D_MODEL), lambda i, be, nu: (jnp.minimum(i, nu[0] - 1), 0)),
                      pl.BlockSpec((1, D_MODEL), lambda i, be, nu: (0, 0)),
                      pl.BlockSpec((None, None, D_MODEL, EXPERT_FF), wmap),
                      pl.BlockSpec((None, None, D_MODEL, EXPERT_FF), wmap),
                      pl.BlockSpec((None, None, EXPERT_FF, D_MODEL), wmap)],
            out_specs=pl.BlockSpec((TM_MOE, D_MODEL), lambda i, be, nu: (i, 0)),
            scratch_shapes=[pltpu.VMEM((D_MODEL, EXPERT_FF), BF16),
                            pltpu.VMEM((D_MODEL, EXPERT_FF), BF16),
                            pltpu.VMEM((EXPERT_FF, D_MODEL), BF16)]),
        out_shape=jax.ShapeDtypeStruct(xs.shape, F32),
        compiler_params=_cparams(1),
        name="moe_experts",
    )(blk_expert, n_used, xs, g3, w1, w3, w2)


def _combine_kernel(dest_ref, x2_ref, ri_ref, y_hbm, o_ref, buf, sem):
    def issue(r, carry):
        for k in range(2):
            pltpu.make_async_copy(y_hbm.at[pl.ds(dest_ref[2 * r + k], 1)],
                                  buf.at[k, pl.ds(r, 1)], sem).start()
        return carry

    def drain(r, carry):
        for k in range(2):
            pltpu.make_async_copy(y_hbm.at[pl.ds(0, 1)], buf.at[k, pl.ds(0, 1)], sem).wait()
        return carry

    lax.fori_loop(0, TC, issue, 0)
    lax.fori_loop(0, TC, drain, 0)
    info = ri_ref[...]
    o_ref[...] = x2_ref[...] + info[:, 2:3] * buf[0] + info[:, 3:4] * buf[1]


def _combine(dest, x2, rinfo, ybuf):
    T = x2.shape[0]
    row = lambda i: (i, 0)
    return pl.pallas_call(
        _combine_kernel,
        grid=(T // TC,),
        in_specs=[pl.BlockSpec((2 * TC,), lambda i: (i,), memory_space=pltpu.SMEM),
                  pl.BlockSpec((TC, D_MODEL), row),
                  pl.BlockSpec((TC, LANES), row),
                  pl.BlockSpec(memory_space=pl.ANY)],
        out_specs=pl.BlockSpec((TC, D_MODEL), row),
        out_shape=jax.ShapeDtypeStruct((T, D_MODEL), F32),
        scratch_shapes=[pltpu.VMEM((2, TC, D_MODEL), F32), pltpu.SemaphoreType.DMA],
        compiler_params=_cparams(1),
        name="moe_combine",
    )(dest, x2, rinfo, ybuf)


def _tri(n, strict):
    return jnp.asarray(np.tril(np.ones((n, n), np.float32), -1 if strict else 0), BF16)


def _rope_tables(S):
    half = ROPE_DIM // 2
    inv = jnp.power(ROPE_THETA, -2.0 * jnp.arange(half, dtype=F32) / ROPE_DIM)
    ang = jnp.arange(S).astype(F32)[:, None] * inv[None, :]
    cos, sin = jnp.cos(ang), jnp.sin(ang)
    d = np.arange(ATT_W) % HEAD_DIM
    idx = d % half
    c = jnp.where(d < ROPE_DIM, cos[:, idx], 1.0)
    s1 = jnp.where(d < half, -sin[:, idx], 0.0)
    s2 = jnp.where((d >= half) & (d < ROPE_DIM), sin[:, idx], 0.0)
    return jnp.stack([c, s1, s2]).astype(F32)


def _fox_consts():
    pq = np.zeros((ATT_HEADS, 3 * LANES, LANES), np.float32)
    pk = np.zeros_like(pq)
    oq = np.zeros((ATT_HEADS, 1, LANES), np.float32)
    ok = np.zeros_like(oq)
    for h in range(ATT_HEADS):
        off = HEAD_DIM if h % 2 == 0 else 0
        for k in range(3):
            pq[h, k * LANES + h, off + k] = 1.0
            oq[h, 0, off + 3 + k] = 1.0
            pk[h, k * LANES + h, off + 3 + k] = -1.0
            ok[h, 0, off + k] = 1.0
    return (_tri(256, False), jnp.asarray(pq, BF16), jnp.asarray(pk, BF16),
            jnp.asarray(oq), jnp.asarray(ok))


def _ssd_consts():
    pexp = np.zeros((3 * LANES, SSD_INNER), np.float32)
    pq = np.zeros((3 * LANES, SSD_HEADS * LANES), np.float32)
    pk = np.zeros_like(pq)
    oq = np.zeros((1, SSD_HEADS * LANES), np.float32)
    ok = np.zeros_like(oq)
    for h in range(SSD_HEADS):
        for k in range(3):
            src = k * LANES + DT_LANE0 + h
            pexp[src, HEAD_DIM * h:HEAD_DIM * (h + 1)] = 1.0
            pq[src, LANES * h + k] = 1.0
            oq[0, LANES * h + 3 + k] = 1.0
            pk[src, LANES * h + 3 + k] = -1.0
            ok[0, LANES * h + k] = 1.0
    return (_tri(SSD_CHUNK, False), jnp.asarray(pexp, BF16), jnp.asarray(pq, BF16),
            jnp.asarray(pk, BF16), jnp.asarray(oq), jnp.asarray(ok))


def _score_tables(S):
    nd = S // TK
    i = np.arange(TQ)[:, None]
    j = np.arange(TK)[None, :]
    causal = np.where(i >= j, 0.0, NEG).astype(np.float32)[None]
    dil = np.zeros((nd, TQ, TK), np.float32)
    for d in range(nd):
        delta = d * TK + i - j
        mult = np.zeros((TQ, TK), np.float64)
        for window, step in DIL_CONFIGS:
            mult += (delta >= 0) & (delta <= window) & (delta % step == 0)
        dil[d] = np.where(mult > 0, np.log(np.maximum(mult, 1.0)), NEG)
    return jnp.asarray(causal), jnp.asarray(dil)


def _group_matrix():
    g = np.arange(ATT_W) // HEAD_DIM
    return jnp.asarray((g[:, None] == g[None, :]).astype(np.float32), BF16)


def _pad_lanes(v, lane0):
    return jnp.zeros((1, LANES), F32).at[0, lane0:lane0 + v.shape[0]].set(v)


def _layer_params(l, w_in, fox_fgate_b, fox_qn_g, fox_kn_g, dil_qn_g, dil_kn_g, ssd_dt_bias, ssd_A_log,
                  ssd_D, router_wg, router_bg, router_we, router_be):
    w = w_in[l]
    w_r = jnp.concatenate([w[:, 0:768], w[:, 772:3076], w[:, 768:772], w[:, 3076:3084],
                           jnp.zeros((D_MODEL, LANES - 12), F32)], axis=1).astype(BF16)
    tile4 = lambda g: jnp.tile(g, ATT_HEADS)[None, :]
    qkg = jnp.stack([tile4(fox_qn_g[l]), tile4(fox_kn_g[l]), tile4(dil_qn_g[l]), tile4(dil_kn_g[l])])
    sb = _pad_lanes(fox_fgate_b[l], 0) + _pad_lanes(ssd_dt_bias[l], DT_LANE0)
    arow = _pad_lanes(-jnp.exp(ssd_A_log[l]), DT_LANE0)
    dx = jnp.repeat(ssd_D[l], HEAD_DIM)[None, :]
    wr = jnp.concatenate([router_wg[l], router_we[l],
                          jnp.zeros((D_MODEL, LANES - N_GROUPS - N_EXPERTS), F32)], axis=1)
    wr3 = _split3(wr).reshape(D_MODEL, 3, LANES).transpose(1, 0, 2)
    rb = _pad_lanes(router_bg[l], 0) + _pad_lanes(router_be[l], N_GROUPS)
    return w_r, qkg, sb, arow, dx, wr3, rb


def kernel(x, mem, norm1_g, w_in, fox_fgate_b, fox_qn_g, fox_kn_g, dil_qn_g, dil_kn_g, ssd_conv_w,
           ssd_conv_b, ssd_dt_bias, ssd_A_log, ssd_D, ssd_norm_g, w_out, norm2_g, mem_norm_g, xa_wq,
           xa_wkv, xa_qn_g, xa_kn_g, xa_wo, norm3_g, router_wg, router_bg, router_we, router_be,
           exp_w1, exp_w3, exp_w2):
    B, S, _ = x.shape
    T = B * S
    depth = w_in.shape[0]
    assert S % TM_IN == 0 and S % TQ == 0 and T % TD == 0 and S >= DIL_CONFIGS[-1][0]

    rope = _rope_tables(S)
    fox_consts = _fox_consts()
    ssd_consts = _ssd_consts()
    causal, dil_tab = _score_tables(S)
    gm = _group_matrix()
    tri_mid = _tri(TM_MID, True)
    nblk = (2 * T) // TM_MOE + N_EXPERTS

    x2d = x.reshape(T, D_MODEL)
    for l in range(depth):
        w_r, qkg, sb, arow, dx, wr3, rb = _layer_params(
            l, w_in, fox_fgate_b, fox_qn_g, fox_kn_g, dil_qn_g, dil_kn_g, ssd_dt_bias, ssd_A_log,
            ssd_D, router_wg, router_bg, router_we, router_be)

        fq, fkt, fv, dq, dkt, dv, z, xbc, small = _inproj(
            x2d, norm1_g[l][None, :], w_r, gm, qkg, rope, sb, B, S)
        aug = _fox_scan(small, fox_consts, B, S)
        o_fox = _attention(fq, fkt, fv, causal, aug, B, S)
        o_dil = _attention(dq, dkt, dv, dil_tab, None, B, S)
        o_ssd = _ssd(xbc, z, small, ssd_conv_w[l], ssd_conv_b[l][None, :], arow, dx,
                     ssd_norm_g[l][None, :], ssd_consts, B, S)

        kt, v = _kv(mem, mem_norm_g[l][None, :], xa_wkv[l].astype(BF16), xa_kn_g[l][None, :])
        x2, rinfo, cnt = _mid(x2d, o_fox, o_dil, o_ssd, w_out[l].astype(BF16), norm2_g[l][None, :],
                              xa_wq[l].astype(BF16), xa_qn_g[l][None, :], kt, v,
                              xa_wo[l].astype(BF16), norm3_g[l][None, :], wr3, rb, tri_mid, S)

        counts = cnt[0, :N_EXPERTS].astype(jnp.int32)
        padded = (counts + TM_MOE - 1) // TM_MOE * TM_MOE
        ends = jnp.cumsum(padded)
        eid = rinfo[:, 0:2].astype(jnp.int32)
        rank = rinfo[:, 4:6].astype(jnp.int32)
        dest = ((ends - padded)[eid] + rank).reshape(2 * T)
        n_used = (ends[-1:] // TM_MOE).astype(jnp.int32)
        blk_expert = jnp.minimum(
            jnp.searchsorted(ends, jnp.arange(nblk, dtype=jnp.int32) * TM_MOE, side="right"),
            N_EXPERTS - 1).astype(jnp.int32)

        xs = _dispatch(dest, x2, jnp.zeros((nblk * TM_MOE, D_MODEL), F32))
        ybuf = _experts(blk_expert, n_used, xs, norm3_g[l][None, :], exp_w1, exp_w3, exp_w2, l)
        x2d = _combine(dest, x2, rinfo, ybuf)
    return x2d.reshape(B, S, D_MODEL)
```

```python
import functools

import jax
import jax.numpy as jnp
import numpy as np
from jax import lax
from jax.experimental import pallas as pl
from jax.experimental.pallas import tpu as pltpu

F32 = jnp.float32
BF16 = jnp.bfloat16

D_MODEL = 1024
HEAD_DIM = 64
ATT_HEADS = 4
ATT_W = ATT_HEADS * HEAD_DIM
SSD_HEADS = 8
SSD_INNER = 512
SSD_STATE = 128
SSD_CONV = 4
SSD_CHUNK = 128
SSD_CONV_CH = 1024
XA_HEADS = 4
XA_HEAD_DIM = 256
MEM_LEN = 256
N_GROUPS = 4
EXPERTS_PER_GROUP = 8
N_EXPERTS = 32
EXPERT_FF = 512
DIL_CONFIGS = ((128, 1), (512, 4), (2048, 16))
ROPE_THETA = 500000.0
ROPE_DIM = 16
EPS = 1e-6
NEG = -1e30

LANES = 128
N_FGATE = 4
DT_LANE0 = 4
MAIN_W = 3 * ATT_W + 3 * ATT_W + SSD_INNER + SSD_CONV_CH
IN_W = MAIN_W + LANES

TM_IN = 512
TQ = 256
TK = 256
TM_MID = 256
TM_MOE = 256
TD = 512
TC = 512
VMEM_LIMIT = 48 * 1024 * 1024


def _cparams(n_axes):
    return pltpu.CompilerParams(dimension_semantics=("arbitrary",) * n_axes,
                                vmem_limit_bytes=VMEM_LIMIT)


def _rms(x, g):
    return x * lax.rsqrt(jnp.mean(x * x, axis=-1, keepdims=True) + EPS) * g


def _split3(x):
    hi = x.astype(BF16)
    r = x - hi.astype(F32)
    mid = r.astype(BF16)
    lo = (r - mid.astype(F32)).astype(BF16)
    return jnp.concatenate([hi, mid, lo], axis=1)


def _dot(a, b):
    return jnp.dot(a, b, preferred_element_type=F32)


def _dot_nt(a, b):
    return lax.dot_general(a, b, (((1,), (1,)), ((), ())), preferred_element_type=F32)


def _sum3(c):
    w = c.shape[1] // 3
    return c[:, 0:w] + c[:, w:2 * w] + c[:, 2 * w:3 * w]


def _inproj_kernel(x_ref, g_ref, wa_ref, wb_ref, ws_ref, gm_ref, qkg_ref, rope_ref, sb_ref,
                   fq_ref, fkt_ref, fv_ref, dq_ref, dkt_ref, dv_ref, z_ref, xbc_ref, sm_ref):
    h = _rms(x_ref[...], g_ref[...]).astype(BF16)
    na = wa_ref.shape[1]

    def proj(a, b):
        if b <= na:
            return _dot(h, wa_ref[:, a:b])
        return _dot(h, wb_ref[:, a - na:b - na])

    def head_norm(a, idx):
        ssq = _dot((a * a).astype(BF16), gm_ref[...])
        return a * lax.rsqrt(ssq * (1.0 / HEAD_DIM) + EPS) * qkg_ref[idx]

    def rope(a):
        return (a * rope_ref[0] + pltpu.roll(a, ATT_W - ROPE_DIM // 2, 1) * rope_ref[1]
                + pltpu.roll(a, ROPE_DIM // 2, 1) * rope_ref[2])

    scale = HEAD_DIM ** -0.5
    fq_ref[...] = (head_norm(proj(0, 256), 0) * scale).astype(BF16)
    fkt_ref[0] = head_norm(proj(256, 512), 1).T.astype(BF16)
    fv_ref[...] = proj(512, 768).astype(BF16)
    dq_ref[...] = (rope(head_norm(proj(768, 1024), 2)) * scale).astype(BF16)
    dkt_ref[0] = rope(head_norm(proj(1024, 1280), 3)).T.astype(BF16)
    dv_ref[...] = proj(1280, 1536).astype(BF16)
    z_ref[...] = proj(1536, 2048)
    xbc_ref[...] = proj(2048, MAIN_W)
    v = _dot(h, ws_ref[...]) + sb_ref[...]
    e = jnp.log1p(jnp.exp(-jnp.abs(v)))
    lane = lax.broadcasted_iota(jnp.int32, v.shape, 1)
    sm_ref[...] = jnp.where(lane < N_FGATE, jnp.minimum(v, 0.0) - e, jnp.maximum(v, 0.0) + e)


def _inproj(x2d, g, wa, wb, ws, gm, qkg, rope, sb, B, S):
    T = x2d.shape[0]
    nst = S // TM_IN
    row = lambda i: (i, 0)
    const2 = lambda i: (0, 0)
    tr = lambda i: (i // nst, 0, i % nst)
    out_shape = [
        jax.ShapeDtypeStruct((T, ATT_W), BF16),
        jax.ShapeDtypeStruct((B, ATT_W, S), BF16),
        jax.ShapeDtypeStruct((T, ATT_W), BF16),
        jax.ShapeDtypeStruct((T, ATT_W), BF16),
        jax.ShapeDtypeStruct((B, ATT_W, S), BF16),
        jax.ShapeDtypeStruct((T, ATT_W), BF16),
        jax.ShapeDtypeStruct((T, SSD_INNER), F32),
        jax.ShapeDtypeStruct((T, SSD_CONV_CH), F32),
        jax.ShapeDtypeStruct((T, LANES), F32),
    ]
    att = pl.BlockSpec((TM_IN, ATT_W), row)
    att_t = pl.BlockSpec((1, ATT_W, TM_IN), tr)
    return pl.pallas_call(
        _inproj_kernel,
        grid=(T // TM_IN,),
        in_specs=[
            pl.BlockSpec((TM_IN, D_MODEL), row),
            pl.BlockSpec((1, D_MODEL), const2),
            pl.BlockSpec(wa.shape, const2),
            pl.BlockSpec(wb.shape, const2),
            pl.BlockSpec(ws.shape, const2),
            pl.BlockSpec((ATT_W, ATT_W), const2),
            pl.BlockSpec((4, 1, ATT_W), lambda i: (0, 0, 0)),
            pl.BlockSpec((3, TM_IN, ATT_W), lambda i: (0, i % nst, 0)),
            pl.BlockSpec((1, LANES), const2),
        ],
        out_specs=[att, att_t, att, att, att_t, att,
                   pl.BlockSpec((TM_IN, SSD_INNER), row),
                   pl.BlockSpec((TM_IN, SSD_CONV_CH), row),
                   pl.BlockSpec((TM_IN, LANES), row)],
        out_shape=out_shape,
        compiler_params=_cparams(1),
        name="inproj",
    )(x2d, g, wa, wb, ws, gm, qkg, rope, sb)


def _fox_scan_kernel(sm_ref, tri_ref, pq_ref, pk_ref, oq_ref, ok_ref, augq_ref, augkt_ref):
    S = sm_ref.shape[1]
    blk = tri_ref.shape[0]
    carry = jnp.zeros((1, LANES), F32)
    for b in range(S // blk):
        rows = slice(b * blk, (b + 1) * blk)
        c = _sum3(_dot(tri_ref[...], _split3(sm_ref[0, rows, :]))) + carry
        carry = c[blk - 1:blk, :]
        c3 = _split3(c)
        for h in range(ATT_HEADS):
            cols = slice(LANES * h, LANES * (h + 1))
            augq_ref[0, rows, cols] = (_dot(c3, pq_ref[h]) + oq_ref[h]).astype(BF16)
            augkt_ref[0, cols, rows] = (_dot(c3, pk_ref[h]) + ok_ref[h]).T.astype(BF16)


def _fox_scan(small, consts, B, S):
    tri, pq, pk, oq, ok = consts
    c2 = lambda b: (0, 0)
    c3 = lambda b: (0, 0, 0)
    return pl.pallas_call(
        _fox_scan_kernel,
        grid=(B,),
        in_specs=[
            pl.BlockSpec((1, S, LANES), lambda b: (b, 0, 0)),
            pl.BlockSpec(tri.shape, c2),
            pl.BlockSpec(pq.shape, c3),
            pl.BlockSpec(pk.shape, c3),
            pl.BlockSpec(oq.shape, c3),
            pl.BlockSpec(ok.shape, c3),
        ],
        out_specs=[pl.BlockSpec((1, S, ATT_HEADS * LANES), lambda b: (b, 0, 0)),
                   pl.BlockSpec((1, ATT_HEADS * LANES, S), lambda b: (b, 0, 0))],
        out_shape=[jax.ShapeDtypeStruct((B, S, ATT_HEADS * LANES), BF16),
                   jax.ShapeDtypeStruct((B, ATT_HEADS * LANES, S), BF16)],
        compiler_params=_cparams(1),
        name="fox_scan",
    )(small.reshape(B, S, LANES), tri, pq, pk, oq, ok)


def _attn_kernel(*refs, fox, nk):
    if fox:
        q_ref, kt_ref, v_ref, lm_ref, augq_ref, augkt_ref, o_ref, kt_scr, v_scr = refs
    else:
        q_ref, kt_ref, v_ref, lm_ref, o_ref, kt_scr, v_scr = refs
    qi = pl.program_id(1)

    @pl.when(qi == 0)
    def _prep():
        row = lax.broadcasted_iota(jnp.int32, (LANES, TK), 0)
        lane = lax.broadcasted_iota(jnp.int32, (TK, LANES), 1)
        for h in range(ATT_HEADS):
            p, mem = divmod(h, 2)
            pair = slice(LANES * p, LANES * (p + 1))
            for j in range(nk):
                keys = slice(j * TK, (j + 1) * TK)
                kd = kt_ref[0, pair, keys]
                if fox:
                    other = augkt_ref[0, LANES * h:LANES * (h + 1), keys]
                else:
                    other = jnp.zeros_like(kd)
                kt_scr[h, j] = jnp.where((row >> 6) == mem, kd, other)
                vd = v_ref[keys, pair]
                v_scr[h, j] = jnp.where((lane >> 6) == mem, vd, jnp.ones_like(vd))

    qlane = lax.broadcasted_iota(jnp.int32, (TQ, LANES), 1)
    qas = []
    for h in range(ATT_HEADS):
        p, mem = divmod(h, 2)
        qd = q_ref[:, LANES * p:LANES * (p + 1)]
        if fox:
            other = augq_ref[:, LANES * h:LANES * (h + 1)]
        else:
            other = jnp.zeros_like(qd)
        qas.append(jnp.where((qlane >> 6) == mem, qd, other))

    def step(j, carry, table):
        new = []
        for h in range(ATT_HEADS):
            m, acc = carry[h]
            s = _dot(qas[h], kt_scr[h, j])
            if table is not None:
                s = s + table
            m_new = jnp.maximum(m, jnp.max(s, axis=1, keepdims=True))
            alpha = jnp.exp(m - m_new)
            pr = jnp.exp(s - m_new).astype(BF16)
            new.append((m_new, alpha * acc + _dot(pr, v_scr[h, j])))
        return tuple(new)

    init = tuple((jnp.full((TQ, 1), NEG, F32), jnp.zeros((TQ, LANES), F32))
                 for _ in range(ATT_HEADS))
    if fox:
        carry = lax.fori_loop(0, qi, lambda j, c: step(j, c, None), init)
        carry = step(qi, carry, lm_ref[0])
    else:
        carry = lax.fori_loop(0, qi + 1, lambda j, c: step(j, c, lm_ref[qi - j]), init)
    outs = [acc / pltpu.roll(acc, HEAD_DIM, 1) for _, acc in carry]
    for p in range(ATT_HEADS // 2):
        o_ref[:, LANES * p:LANES * (p + 1)] = jnp.where(
            (qlane >> 6) == 0, outs[2 * p], outs[2 * p + 1]).astype(BF16)


def _attention(q, kt, v, lm, aug, B, S):
    fox = aug is not None
    nq, nk = S // TQ, S // TK
    in_specs = [
        pl.BlockSpec((TQ, ATT_W), lambda b, i: (b * nq + i, 0)),
        pl.BlockSpec((1, ATT_W, S), lambda b, i: (b, 0, 0)),
        pl.BlockSpec((S, ATT_W), lambda b, i: (b, 0)),
        pl.BlockSpec(lm.shape, lambda b, i: (0, 0, 0)),
    ]
    args = [q, kt, v, lm]
    if fox:
        in_specs += [pl.BlockSpec((TQ, ATT_HEADS * LANES), lambda b, i: (b * nq + i, 0)),
                     pl.BlockSpec((1, ATT_HEADS * LANES, S), lambda b, i: (b, 0, 0))]
        args += [aug[0].reshape(B * S, ATT_HEADS * LANES), aug[1]]
    return pl.pallas_call(
        functools.partial(_attn_kernel, fox=fox, nk=nk),
        grid=(B, nq),
        in_specs=in_specs,
        out_specs=pl.BlockSpec((TQ, ATT_W), lambda b, i: (b * nq + i, 0)),
        out_shape=jax.ShapeDtypeStruct((B * S, ATT_W), BF16),
        scratch_shapes=[pltpu.VMEM((ATT_HEADS, nk, LANES, TK), BF16),
                        pltpu.VMEM((ATT_HEADS, nk, TK, LANES), BF16)],
        compiler_params=_cparams(2),
        name="fox_attn" if fox else "dil_attn",
    )(*args)


def _ssd_kernel(xbc_ref, z_ref, sm_ref, cw_ref, cb_ref, arow_ref, dx_ref, ng_ref,
                tri_ref, pexp_ref, pq_ref, pk_ref, oq_ref, ok_ref, o_ref, buf, state):
    Q = SSD_CHUNK

    @pl.when(pl.program_id(1) == 0)
    def _reset():
        buf[0:8, :] = jnp.zeros((8, SSD_CONV_CH), F32)
        state[...] = jnp.zeros(state.shape, F32)

    xb = xbc_ref[...]
    buf[8:8 + Q, :] = xb
    conv = cb_ref[...]
    for k in range(SSD_CONV):
        off = 8 - (SSD_CONV - 1) + k
        conv = conv + cw_ref[k:k + 1, :] * buf[off:off + Q, :]
    buf[0:8, :] = xb[Q - 8:Q, :]
    act = conv * jax.nn.sigmoid(conv)
    xs = act[:, 0:SSD_INNER]
    bm = act[:, SSD_INNER:SSD_INNER + 2 * SSD_STATE]
    cm = act[:, SSD_INNER + 2 * SSD_STATE:]

    dt = sm_ref[...]
    acs = _sum3(_dot(tri_ref[...], _split3(dt * arow_ref[...])))
    acs3 = _split3(acs)
    ax = _dot(acs3, pexp_ref[...])
    dtx = _dot(_split3(dt), pexp_ref[...])
    last = ax[Q - 1:Q, :]
    ea = jnp.exp(ax)
    cdec = jnp.exp(last)
    xc = xs * dtx
    xcb = xc.astype(BF16)
    xcd = (xc * jnp.exp(last - ax)).astype(BF16)
    uq = (_dot(acs3, pq_ref[...]) + oq_ref[...]).astype(BF16)
    uk = (_dot(acs3, pk_ref[...]) + ok_ref[...]).astype(BF16)

    tril = (lax.broadcasted_iota(jnp.int32, (Q, Q), 0) >= lax.broadcasted_iota(jnp.int32, (Q, Q), 1))
    first = lax.broadcasted_iota(jnp.int32, (Q, LANES), 1) < HEAD_DIM
    ys = []
    for g in range(2):
        gs = slice(SSD_STATE * g, SSD_STATE * (g + 1))
        bg = bm[:, gs]
        cg = cm[:, gs].astype(BF16)
        cbm = _dot_nt(cg, bg.astype(BF16))
        bgt = bg.T.astype(BF16)
        for pp in range(2):
            p = 2 * g + pp
            ps = slice(LANES * p, LANES * (p + 1))
            ms = []
            for mem in range(2):
                hs = slice(LANES * (2 * p + mem), LANES * (2 * p + mem + 1))
                dm = _dot_nt(uq[:, hs], uk[:, hs])
                ms.append((cbm * jnp.exp(jnp.where(tril, dm, NEG))).astype(BF16))
            xp = xcb[:, ps]
            zero = jnp.zeros_like(xp)
            xcat = jnp.concatenate([jnp.where(first, xp, zero), jnp.where(first, zero, xp)], axis=0)
            y_diag = _dot(jnp.concatenate(ms, axis=1), xcat)
            st = state[p]
            y_off = _dot(cg, st.astype(BF16)) * ea[:, ps]
            state[p] = cdec[:, ps] * st + _dot(bgt, xcd[:, ps])
            ys.append(y_diag + y_off + xs[:, ps] * dx_ref[:, ps])
    y = jnp.concatenate(ys, axis=1)
    zz = z_ref[...]
    o_ref[...] = _rms(y * (zz * jax.nn.sigmoid(zz)), ng_ref[...]).astype(BF16)


def _ssd(xbc, z, small, cw, cb, arow, dx, ng, consts, B, S):
    nc = S // SSD_CHUNK
    row = lambda b, c: (b * nc + c, 0)
    c2 = lambda b, c: (0, 0)
    full = lambda a: pl.BlockSpec(a.shape, c2)
    return pl.pallas_call(
        _ssd_kernel,
        grid=(B, nc),
        in_specs=[pl.BlockSpec((SSD_CHUNK, SSD_CONV_CH), row),
                  pl.BlockSpec((SSD_CHUNK, SSD_INNER), row),
                  pl.BlockSpec((SSD_CHUNK, LANES), row),
                  full(cw), full(cb), full(arow), full(dx), full(ng)] + [full(a) for a in consts],
        out_specs=pl.BlockSpec((SSD_CHUNK, SSD_INNER), row),
        out_shape=jax.ShapeDtypeStruct((B * S, SSD_INNER), BF16),
        scratch_shapes=[pltpu.VMEM((8 + SSD_CHUNK, SSD_CONV_CH), F32),
                        pltpu.VMEM((SSD_HEADS // 2, SSD_STATE, LANES), F32)],
        compiler_params=_cparams(2),
        name="ssd",
    )(xbc, z, small, cw, cb, arow, dx, ng, *consts)


def _kv_kernel(mem_ref, g_ref, w_ref, kg_ref, kt_ref, v_ref):
    m = _rms(mem_ref[0], g_ref[...]).astype(BF16)
    kv = _dot(m, w_ref[...])
    for h in range(XA_HEADS):
        hs = slice(XA_HEAD_DIM * h, XA_HEAD_DIM * (h + 1))
        kt_ref[0, hs, :] = _rms(kv[:, hs], kg_ref[...]).T.astype(BF16)
    v_ref[0] = kv[:, D_MODEL:].astype(BF16)


def _kv(mem, g, w, kg):
    B = mem.shape[0]
    c2 = lambda b: (0, 0)
    return pl.pallas_call(
        _kv_kernel,
        grid=(B,),
        in_specs=[pl.BlockSpec((1, MEM_LEN, D_MODEL), lambda b: (b, 0, 0)),
                  pl.BlockSpec((1, D_MODEL), c2),
                  pl.BlockSpec((D_MODEL, 2 * D_MODEL), c2),
                  pl.BlockSpec((1, XA_HEAD_DIM), c2)],
        out_specs=[pl.BlockSpec((1, D_MODEL, MEM_LEN), lambda b: (b, 0, 0)),
                   pl.BlockSpec((1, MEM_LEN, D_MODEL), lambda b: (b, 0, 0))],
        out_shape=[jax.ShapeDtypeStruct((B, D_MODEL, MEM_LEN), BF16),
                   jax.ShapeDtypeStruct((B, MEM_LEN, D_MODEL), BF16)],
        compiler_params=_cparams(1),
        name="mem_kv",
    )(mem, g, w, kg)


def _mid_kernel(x_ref, of_ref, od_ref, os_ref, wo_ref, g2_ref, wq_ref, qg_ref, kt_ref, v_ref,
                wxo_ref, g3_ref, wr_ref, rb_ref, tri_ref, x2_ref, ri_ref, cnt_ref, run):
    @pl.when(pl.program_id(0) == 0)
    def _reset():
        run[...] = jnp.zeros(run.shape, F32)

    x1 = (x_ref[...] + _dot(of_ref[...], wo_ref[0:ATT_W, :])
          + _dot(od_ref[...], wo_ref[ATT_W:2 * ATT_W, :])
          + _dot(os_ref[...], wo_ref[2 * ATT_W:, :]))

    q = _dot(_rms(x1, g2_ref[...]).astype(BF16), wq_ref[...])
    heads = []
    for h in range(XA_HEADS):
        hs = slice(XA_HEAD_DIM * h, XA_HEAD_DIM * (h + 1))
        qn = (_rms(q[:, hs], qg_ref[...]) * XA_HEAD_DIM ** -0.5).astype(BF16)
        s = _dot(qn, kt_ref[0, hs, :])
        e = jnp.exp(s - jnp.max(s, axis=1, keepdims=True))
        o = _dot(e.astype(BF16), v_ref[0, :, hs]) / jnp.sum(e, axis=1, keepdims=True)
        heads.append(o.astype(BF16))
    x2 = x1 + _dot(jnp.concatenate(heads, axis=1), wxo_ref[...])
    x2_ref[...] = x2

    h3 = _split3(_rms(x2, g3_ref[...]))
    hi, mid, lo = h3[:, 0:D_MODEL], h3[:, D_MODEL:2 * D_MODEL], h3[:, 2 * D_MODEL:]
    logits = (_dot(mid, wr_ref[1]) + _dot(hi, wr_ref[2]) + _dot(lo, wr_ref[0])
              + _dot(hi, wr_ref[1]) + _dot(mid, wr_ref[0]) + _dot(hi, wr_ref[0]) + rb_ref[...])

    lane = lax.broadcasted_iota(jnp.int32, logits.shape, 1)
    lanef = lane.astype(F32)
    big = float(LANES)

    def first_max(vals):
        top = jnp.max(vals, axis=1, keepdims=True)
        return top, jnp.min(jnp.where(vals == top, lanef, big), axis=1, keepdims=True)

    gl = jnp.where(lane < N_GROUPS, logits, NEG)
    gmax, gsel = first_max(gl)
    ggate = 1.0 / jnp.sum(jnp.exp(gl - gmax), axis=1, keepdims=True)
    grp = ((lane - N_GROUPS) >> 3).astype(F32)
    el = jnp.where(grp == gsel, logits, NEG)
    v1, i1 = first_max(el)
    v2, i2 = first_max(jnp.where(lanef == i1, NEG, el))
    t = jnp.exp(v2 - v1)
    p1 = 1.0 / (1.0 + t)
    e1 = i1 - N_GROUPS
    e2 = i2 - N_GROUPS

    hit1 = lanef == e1
    hit2 = lanef == e2
    onehot = jnp.where(hit1 | hit2, 1.0, 0.0)
    before = _dot(tri_ref[...], onehot.astype(BF16)) + run[...]
    r1 = jnp.sum(jnp.where(hit1, before, 0.0), axis=1, keepdims=True)
    r2 = jnp.sum(jnp.where(hit2, before, 0.0), axis=1, keepdims=True)
    run[...] = run[...] + jnp.sum(onehot, axis=0, keepdims=True)

    cols = (e1, e2, p1 * ggate, t * p1 * ggate, r1, r2)
    info = jnp.zeros(logits.shape, F32)
    for k, col in enumerate(cols):
        info = jnp.where(lane == k, col, info)
    ri_ref[...] = info
    cnt_ref[...] = jnp.broadcast_to(run[...], cnt_ref.shape)


def _mid(x2d, o_fox, o_dil, o_ssd, wo, g2, wq, qg, kt, v, wxo, g3, wr, rb, tri, S):
    T = x2d.shape[0]
    npb = S // TM_MID
    row = lambda i: (i, 0)
    c2 = lambda i: (0, 0)
    c3 = lambda i: (0, 0, 0)
    return pl.pallas_call(
        _mid_kernel,
        grid=(T // TM_MID,),
        in_specs=[pl.BlockSpec((TM_MID, D_MODEL), row),
                  pl.BlockSpec((TM_MID, ATT_W), row),
                  pl.BlockSpec((TM_MID, ATT_W), row),
                  pl.BlockSpec((TM_MID, SSD_INNER), row),
                  pl.BlockSpec((D_MODEL, D_MODEL), c2),
                  pl.BlockSpec((1, D_MODEL), c2),
                  pl.BlockSpec((D_MODEL, D_MODEL), c2),
                  pl.BlockSpec((1, XA_HEAD_DIM), c2),
                  pl.BlockSpec((1, D_MODEL, MEM_LEN), lambda i: (i // npb, 0, 0)),
                  pl.BlockSpec((1, MEM_LEN, D_MODEL), lambda i: (i // npb, 0, 0)),
                  pl.BlockSpec((D_MODEL, D_MODEL), c2),
                  pl.BlockSpec((1, D_MODEL), c2),
                  pl.BlockSpec((3, D_MODEL, LANES), c3),
                  pl.BlockSpec((1, LANES), c2),
                  pl.BlockSpec((TM_MID, TM_MID), c2)],
        out_specs=[pl.BlockSpec((TM_MID, D_MODEL), row),
                   pl.BlockSpec((TM_MID, LANES), row),
                   pl.BlockSpec((8, LANES), c2)],
        out_shape=[jax.ShapeDtypeStruct((T, D_MODEL), F32),
                   jax.ShapeDtypeStruct((T, LANES), F32),
                   jax.ShapeDtypeStruct((8, LANES), F32)],
        scratch_shapes=[pltpu.VMEM((1, LANES), F32)],
        compiler_params=_cparams(1),
        name="mid",
    )(x2d, o_fox, o_dil, o_ssd, wo, g2, wq, qg, kt, v, wxo, g3, wr, rb, tri)


def _row_copy(src, dst, sem, src_row, dst_row):
    return pltpu.make_async_copy(src.at[pl.ds(src_row, 1)], dst.at[pl.ds(dst_row, 1)], sem)


def _dispatch_kernel(dest_ref, x_ref, xs_in, xs_out, sem):
    del xs_in

    def issue(r, carry):
        for k in range(2):
            _row_copy(x_ref, xs_out, sem, r, dest_ref[2 * r + k]).start()
        return carry

    def drain(r, carry):
        for k in range(2):
            _row_copy(x_ref, xs_out, sem, 0, 0).wait()
        return carry

    lax.fori_loop(0, TD, issue, 0, unroll=8)
    lax.fori_loop(0, TD, drain, 0, unroll=8)


def _dispatch(dest, x2, xs_zero):
    T = x2.shape[0]
    return pl.pallas_call(
        _dispatch_kernel,
        grid=(T // TD,),
        in_specs=[pl.BlockSpec((2 * TD,), lambda i: (i,), memory_space=pltpu.SMEM),
                  pl.BlockSpec((TD, D_MODEL), lambda i: (i, 0)),
                  pl.BlockSpec(memory_space=pl.ANY)],
        out_specs=pl.BlockSpec(memory_space=pl.ANY),
        out_shape=jax.ShapeDtypeStruct(xs_zero.shape, F32),
        scratch_shapes=[pltpu.SemaphoreType.DMA],
        input_output_aliases={2: 0},
        compiler_params=_cparams(1),
        name="moe_dispatch",
    )(dest, x2, xs_zero)


def _expert_kernel(be_ref, nu_ref, xs_ref, g3_ref, w1_ref, w3_ref, w2_ref, y_ref, w1s, w3s, w2s):
    i = pl.program_id(0)
    fresh = (i == 0) | (be_ref[i] != be_ref[jnp.maximum(i - 1, 0)])

    @pl.when(fresh)
    def _cast():
        w1s[...] = w1_ref[...].astype(BF16)
        w3s[...] = w3_ref[...].astype(BF16)
        w2s[...] = w2_ref[...].astype(BF16)

    @pl.when(i < nu_ref[0])
    def _run():
        xb = _rms(xs_ref[...], g3_ref[...]).astype(BF16)
        a = _dot(xb, w1s[...])
        b = _dot(xb, w3s[...])
        y_ref[...] = _dot((a * jax.nn.sigmoid(a) * b).astype(BF16), w2s[...])

    @pl.when(i >= nu_ref[0])
    def _skip():
        y_ref[...] = jnp.zeros(y_ref.shape, F32)


def _experts(blk_expert, n_used, xs, g3, w1, w3, w2, layer):
    nblk = xs.shape[0] // TM_MOE
    wmap = lambda i, be, nu: (layer, be[i], 0, 0)
    return pl.pallas_call(
        _expert_kernel,
        grid_spec=pltpu.PrefetchScalarGridSpec(
            num_scalar_prefetch=2,
            grid=(nblk,),
            in_specs=[pl.BlockSpec((TM_MOE, D_MODEL), lambda i, be, nu: (jnp.minimum(i, nu[0] - 1), 0)),
                      pl.BlockSpec((1, D_MODEL), lambda i, be, nu: (0, 0)),
                      pl.BlockSpec((None, None, D_MODEL, EXPERT_FF), wmap),
                      pl.BlockSpec((None, None, D_MODEL, EXPERT_FF), wmap),
                      pl.BlockSpec((None, None, EXPERT_FF, D_MODEL), wmap)],
            out_specs=pl.BlockSpec((TM_MOE, D_MODEL), lambda i, be, nu: (i, 0)),
            scratch_shapes=[pltpu.VMEM((D_MODEL, EXPERT_FF), BF16),
                            pltpu.VMEM((D_MODEL, EXPERT_FF), BF16),
                            pltpu.VMEM((EXPERT_FF, D_MODEL), BF16)]),
        out_shape=jax.ShapeDtypeStruct(xs.shape, F32),
        compiler_params=_cparams(1),
        name="moe_experts",
    )(blk_expert, n_used, xs, g3, w1, w3, w2)


def _combine_kernel(dest_ref, x2_ref, ri_ref, y_hbm, o_ref, buf, sem):
    def issue(r, carry):
        for k in range(2):
            pltpu.make_async_copy(y_hbm.at[pl.ds(dest_ref[2 * r + k], 1)],
                                  buf.at[k, pl.ds(r, 1)], sem).start()
        return carry

    def drain(r, carry):
        for k in range(2):
            pltpu.make_async_copy(y_hbm.at[pl.ds(0, 1)], buf.at[k, pl.ds(0, 1)], sem).wait()
        return carry

    lax.fori_loop(0, TC, issue, 0, unroll=8)
    lax.fori_loop(0, TC, drain, 0, unroll=8)
    info = ri_ref[...]
    o_ref[...] = x2_ref[...] + info[:, 2:3] * buf[0] + info[:, 3:4] * buf[1]


def _combine(dest, x2, rinfo, ybuf):
    T = x2.shape[0]
    row = lambda i: (i, 0)
    return pl.pallas_call(
        _combine_kernel,
        grid=(T // TC,),
        in_specs=[pl.BlockSpec((2 * TC,), lambda i: (i,), memory_space=pltpu.SMEM),
                  pl.BlockSpec((TC, D_MODEL), row),
                  pl.BlockSpec((TC, LANES), row),
                  pl.BlockSpec(memory_space=pl.ANY)],
        out_specs=pl.BlockSpec((TC, D_MODEL), row),
        out_shape=jax.ShapeDtypeStruct((T, D_MODEL), F32),
        scratch_shapes=[pltpu.VMEM((2, TC, D_MODEL), F32), pltpu.SemaphoreType.DMA],
        compiler_params=_cparams(1),
        name="moe_combine",
    )(dest, x2, rinfo, ybuf)


def _tri(n, strict):
    return jnp.asarray(np.tril(np.ones((n, n), np.float32), -1 if strict else 0), BF16)


def _rope_tables(S):
    half = ROPE_DIM // 2
    inv = jnp.power(ROPE_THETA, -2.0 * jnp.arange(half, dtype=F32) / ROPE_DIM)
    ang = jnp.arange(S).astype(F32)[:, None] * inv[None, :]
    cos, sin = jnp.cos(ang), jnp.sin(ang)
    d = np.arange(ATT_W) % HEAD_DIM
    idx = d % half
    c = jnp.where(d < ROPE_DIM, cos[:, idx], 1.0)
    s1 = jnp.where(d < half, -sin[:, idx], 0.0)
    s2 = jnp.where((d >= half) & (d < ROPE_DIM), sin[:, idx], 0.0)
    return jnp.stack([c, s1, s2]).astype(F32)


def _fox_consts():
    pq = np.zeros((ATT_HEADS, 3 * LANES, LANES), np.float32)
    pk = np.zeros_like(pq)
    oq = np.zeros((ATT_HEADS, 1, LANES), np.float32)
    ok = np.zeros_like(oq)
    for h in range(ATT_HEADS):
        off = HEAD_DIM if h % 2 == 0 else 0
        for k in range(3):
            pq[h, k * LANES + h, off + k] = 1.0
            oq[h, 0, off + 3 + k] = 1.0
            pk[h, k * LANES + h, off + 3 + k] = -1.0
            ok[h, 0, off + k] = 1.0
    return (_tri(256, False), jnp.asarray(pq, BF16), jnp.asarray(pk, BF16),
            jnp.asarray(oq), jnp.asarray(ok))


def _ssd_consts():
    pexp = np.zeros((3 * LANES, SSD_INNER), np.float32)
    pq = np.zeros((3 * LANES, SSD_HEADS * LANES), np.float32)
    pk = np.zeros_like(pq)
    oq = np.zeros((1, SSD_HEADS * LANES), np.float32)
    ok = np.zeros_like(oq)
    for h in range(SSD_HEADS):
        for k in range(3):
            src = k * LANES + DT_LANE0 + h
            pexp[src, HEAD_DIM * h:HEAD_DIM * (h + 1)] = 1.0
            pq[src, LANES * h + k] = 1.0
            oq[0, LANES * h + 3 + k] = 1.0
            pk[src, LANES * h + 3 + k] = -1.0
            ok[0, LANES * h + k] = 1.0
    return (_tri(SSD_CHUNK, False), jnp.asarray(pexp, BF16), jnp.asarray(pq, BF16),
            jnp.asarray(pk, BF16), jnp.asarray(oq), jnp.asarray(ok))


def _score_tables(S):
    nd = S // TK
    i = np.arange(TQ)[:, None]
    j = np.arange(TK)[None, :]
    causal = np.where(i >= j, 0.0, NEG).astype(np.float32)[None]
    dil = np.zeros((nd, TQ, TK), np.float32)
    for d in range(nd):
        delta = d * TK + i - j
        mult = np.zeros((TQ, TK), np.float64)
        for window, step in DIL_CONFIGS:
            mult += (delta >= 0) & (delta <= window) & (delta % step == 0)
        dil[d] = np.where(mult > 0, np.log(np.maximum(mult, 1.0)), NEG)
    return jnp.asarray(causal), jnp.asarray(dil)


def _group_matrix():
    g = np.arange(ATT_W) // HEAD_DIM
    return jnp.asarray((g[:, None] == g[None, :]).astype(np.float32), BF16)


def _pad_lanes(v, lane0):
    return jnp.zeros((1, LANES), F32).at[0, lane0:lane0 + v.shape[0]].set(v)


def _layer_params(l, w_in, fox_fgate_b, fox_qn_g, fox_kn_g, dil_qn_g, dil_kn_g, ssd_dt_bias, ssd_A_log,
                  ssd_D, router_wg, router_bg, router_we, router_be):
    w = w_in[l]
    w_r = (w[:, 0:768].astype(BF16), w[:, 772:3076].astype(BF16),
           jnp.concatenate([w[:, 768:772], w[:, 3076:3084],
                            jnp.zeros((D_MODEL, LANES - 12), F32)], axis=1).astype(BF16))
    tile4 = lambda g: jnp.tile(g, ATT_HEADS)[None, :]
    qkg = jnp.stack([tile4(fox_qn_g[l]), tile4(fox_kn_g[l]), tile4(dil_qn_g[l]), tile4(dil_kn_g[l])])
    sb = _pad_lanes(fox_fgate_b[l], 0) + _pad_lanes(ssd_dt_bias[l], DT_LANE0)
    arow = _pad_lanes(-jnp.exp(ssd_A_log[l]), DT_LANE0)
    dx = jnp.repeat(ssd_D[l], HEAD_DIM)[None, :]
    wr = jnp.concatenate([router_wg[l], router_we[l],
                          jnp.zeros((D_MODEL, LANES - N_GROUPS - N_EXPERTS), F32)], axis=1)
    wr3 = _split3(wr).reshape(D_MODEL, 3, LANES).transpose(1, 0, 2)
    rb = _pad_lanes(router_bg[l], 0) + _pad_lanes(router_be[l], N_GROUPS)
    return w_r, qkg, sb, arow, dx, wr3, rb


def kernel(x, mem, norm1_g, w_in, fox_fgate_b, fox_qn_g, fox_kn_g, dil_qn_g, dil_kn_g, ssd_conv_w,
           ssd_conv_b, ssd_dt_bias, ssd_A_log, ssd_D, ssd_norm_g, w_out, norm2_g, mem_norm_g, xa_wq,
           xa_wkv, xa_qn_g, xa_kn_g, xa_wo, norm3_g, router_wg, router_bg, router_we, router_be,
           exp_w1, exp_w3, exp_w2):
    B, S, _ = x.shape
    T = B * S
    depth = w_in.shape[0]
    assert S % TM_IN == 0 and S % TQ == 0 and T % TD == 0 and S >= DIL_CONFIGS[-1][0]

    rope = _rope_tables(S)
    fox_consts = _fox_consts()
    ssd_consts = _ssd_consts()
    causal, dil_tab = _score_tables(S)
    gm = _group_matrix()
    tri_mid = _tri(TM_MID, True)
    nblk = (2 * T) // TM_MOE + N_EXPERTS

    x2d = x.reshape(T, D_MODEL)
    for l in range(depth):
        w_r, qkg, sb, arow, dx, wr3, rb = _layer_params(
            l, w_in, fox_fgate_b, fox_qn_g, fox_kn_g, dil_qn_g, dil_kn_g, ssd_dt_bias, ssd_A_log,
            ssd_D, router_wg, router_bg, router_we, router_be)

        fq, fkt, fv, dq, dkt, dv, z, xbc, small = _inproj(
            x2d, norm1_g[l][None, :], *w_r, gm, qkg, rope, sb, B, S)
        aug = _fox_scan(small, fox_consts, B, S)
        o_fox = _attention(fq, fkt, fv, causal, aug, B, S)
        o_dil = _attention(dq, dkt, dv, dil_tab, None, B, S)
        o_ssd = _ssd(xbc, z, small, ssd_conv_w[l], ssd_conv_b[l][None, :], arow, dx,
                     ssd_norm_g[l][None, :], ssd_consts, B, S)

        kt, v = _kv(mem, mem_norm_g[l][None, :], xa_wkv[l].astype(BF16), xa_kn_g[l][None, :])
        x2, rinfo, cnt = _mid(x2d, o_fox, o_dil, o_ssd, w_out[l].astype(BF16), norm2_g[l][None, :],
                              xa_wq[l].astype(BF16), xa_qn_g[l][None, :], kt, v,
                              xa_wo[l].astype(BF16), norm3_g[l][None, :], wr3, rb, tri_mid, S)

        counts = cnt[0, :N_EXPERTS].astype(jnp.int32)
        padded = (counts + TM_MOE - 1) // TM_MOE * TM_MOE
        ends = jnp.cumsum(padded)
        eid = rinfo[:, 0:2].astype(jnp.int32)
        rank = rinfo[:, 4:6].astype(jnp.int32)
        dest = ((ends - padded)[eid] + rank).reshape(2 * T)
        n_used = (ends[-1:] // TM_MOE).astype(jnp.int32)
        blk_start = jnp.arange(nblk, dtype=jnp.int32) * TM_MOE
        blk_expert = jnp.minimum(jnp.sum(ends[None, :] <= blk_start[:, None], axis=1),
                                 N_EXPERTS - 1).astype(jnp.int32)

        xs = _dispatch(dest, x2, jnp.zeros((nblk * TM_MOE, D_MODEL), F32))
        ybuf = _experts(blk_expert, n_used, xs, norm3_g[l][None, :], exp_w1, exp_w3, exp_w2, l)
        x2d = _combine(dest, x2, rinfo, ybuf)
    return x2d.reshape(B, S, D_MODEL)
```

```python
import functools

import jax
import jax.numpy as jnp
import numpy as np
from jax import lax
from jax.experimental import pallas as pl
from jax.experimental.pallas import tpu as pltpu

F32 = jnp.float32
BF16 = jnp.bfloat16

D_MODEL = 1024
HEAD_DIM = 64
ATT_HEADS = 4
ATT_W = ATT_HEADS * HEAD_DIM
SSD_HEADS = 8
SSD_INNER = 512
SSD_STATE = 128
SSD_CONV = 4
SSD_CHUNK = 128
SSD_CONV_CH = 1024
XA_HEADS = 4
XA_HEAD_DIM = 256
MEM_LEN = 256
N_GROUPS = 4
EXPERTS_PER_GROUP = 8
N_EXPERTS = 32
EXPERT_FF = 512
DIL_CONFIGS = ((128, 1), (512, 4), (2048, 16))
ROPE_THETA = 500000.0
ROPE_DIM = 16
EPS = 1e-6
NEG = -1e30

LANES = 128
N_FGATE = 4
DT_LANE0 = 4
MAIN_W = 3 * ATT_W + 3 * ATT_W + SSD_INNER + SSD_CONV_CH
IN_W = MAIN_W + LANES

TM_IN = 512
TQ = 256
TK = 256
TM_MID = 256
TM_MOE = 256
TD = 1024
TC = 1024
VMEM_LIMIT = 48 * 1024 * 1024


def _cparams(n_axes):
    return pltpu.CompilerParams(dimension_semantics=("arbitrary",) * n_axes,
                                vmem_limit_bytes=VMEM_LIMIT)


def _rms(x, g):
    return x * lax.rsqrt(jnp.mean(x * x, axis=-1, keepdims=True) + EPS) * g


def _split3(x):
    hi = x.astype(BF16)
    r = x - hi.astype(F32)
    mid = r.astype(BF16)
    lo = (r - mid.astype(F32)).astype(BF16)
    return jnp.concatenate([hi, mid, lo], axis=1)


def _dot(a, b):
    return jnp.dot(a, b, preferred_element_type=F32)


def _dot_nt(a, b):
    return lax.dot_general(a, b, (((1,), (1,)), ((), ())), preferred_element_type=F32)


def _sum3(c):
    w = c.shape[1] // 3
    return c[:, 0:w] + c[:, w:2 * w] + c[:, 2 * w:3 * w]


FF_COL = 3 * ATT_W
DQ_COL = FF_COL + N_FGATE
DT_COL = DQ_COL + MAIN_W - 3 * ATT_W
IN_SRC_W = DT_COL + 8
WPREP_ROWS = 256


def _wprep_kernel(w_ref, tail_ref, sh_ref, shs_ref, wa_ref, wb_ref, ws_ref):
    wa_ref[...] = w_ref[:, 0:FF_COL].astype(BF16)
    tail = tail_ref[...].astype(BF16)
    nb = (MAIN_W - FF_COL) // LANES
    for n in range(nb):
        lo = FF_COL + LANES * n
        if n + 1 < nb:
            pair = w_ref[:, lo:lo + 2 * LANES].astype(BF16)
        else:
            pair = jnp.concatenate([w_ref[:, lo:lo + LANES].astype(BF16), tail], axis=1)
        wb_ref[:, LANES * n:LANES * (n + 1)] = _dot(pair, sh_ref[...]).astype(BF16)
    small = jnp.concatenate([w_ref[:, FF_COL:FF_COL + LANES].astype(BF16), tail], axis=1)
    ws_ref[...] = _dot(small, shs_ref[...]).astype(BF16)


def _wprep(w_in, layer, tail, sh, shs):
    row = lambda i: (i, 0)
    c2 = lambda i: (0, 0)
    wb_w = MAIN_W - FF_COL
    return pl.pallas_call(
        _wprep_kernel,
        grid=(D_MODEL // WPREP_ROWS,),
        in_specs=[pl.BlockSpec((None, WPREP_ROWS, MAIN_W), lambda i: (layer, i, 0)),
                  pl.BlockSpec((WPREP_ROWS, LANES), row),
                  pl.BlockSpec(sh.shape, c2), pl.BlockSpec(shs.shape, c2)],
        out_specs=[pl.BlockSpec((WPREP_ROWS, FF_COL), row),
                   pl.BlockSpec((WPREP_ROWS, wb_w), row),
                   pl.BlockSpec((WPREP_ROWS, LANES), row)],
        out_shape=[jax.ShapeDtypeStruct((D_MODEL, FF_COL), BF16),
                   jax.ShapeDtypeStruct((D_MODEL, wb_w), BF16),
                   jax.ShapeDtypeStruct((D_MODEL, LANES), BF16)],
        compiler_params=_cparams(1),
        name="wprep",
    )(w_in, tail, sh, shs)


def _wprep_consts():
    sh = np.zeros((2 * LANES, LANES), np.float32)
    shs = np.zeros((2 * LANES, LANES), np.float32)
    off = DQ_COL - FF_COL
    for j in range(LANES):
        sh[j + off, j] = 1.0
    for k in range(N_FGATE):
        shs[k, k] = 1.0
    for k in range(SSD_HEADS):
        shs[LANES + DT_COL - MAIN_W + k, DT_LANE0 + k] = 1.0
    return jnp.asarray(sh, BF16), jnp.asarray(shs, BF16)


def _inproj_kernel(x_ref, g_ref, wa_ref, wb_ref, ws_ref, gm_ref, qkg_ref, rope_ref, sb_ref,
                   fq_ref, fkt_ref, fv_ref, dq_ref, dkt_ref, dv_ref, z_ref, xbc_ref, sm_ref):
    h = _rms(x_ref[...], g_ref[...]).astype(BF16)
    na = wa_ref.shape[1]

    def proj(a, b):
        if b <= na:
            return _dot(h, wa_ref[:, a:b])
        return _dot(h, wb_ref[:, a - na:b - na])

    def head_norm(a, idx):
        ssq = _dot((a * a).astype(BF16), gm_ref[...])
        return a * lax.rsqrt(ssq * (1.0 / HEAD_DIM) + EPS) * qkg_ref[idx]

    def rope(a):
        return (a * rope_ref[0] + pltpu.roll(a, ATT_W - ROPE_DIM // 2, 1) * rope_ref[1]
                + pltpu.roll(a, ROPE_DIM // 2, 1) * rope_ref[2])

    scale = HEAD_DIM ** -0.5
    fq_ref[...] = (head_norm(proj(0, 256), 0) * scale).astype(BF16)
    fkt_ref[0] = head_norm(proj(256, 512), 1).T.astype(BF16)
    fv_ref[...] = proj(512, 768).astype(BF16)
    dq_ref[...] = (rope(head_norm(proj(768, 1024), 2)) * scale).astype(BF16)
    dkt_ref[0] = rope(head_norm(proj(1024, 1280), 3)).T.astype(BF16)
    dv_ref[...] = proj(1280, 1536).astype(BF16)
    z_ref[...] = proj(1536, 2048)
    xbc_ref[...] = proj(2048, MAIN_W)
    v = _dot(h, ws_ref[...]) + sb_ref[...]
    e = jnp.log1p(jnp.exp(-jnp.abs(v)))
    lane = lax.broadcasted_iota(jnp.int32, v.shape, 1)
    sm_ref[...] = jnp.where(lane < N_FGATE, jnp.minimum(v, 0.0) - e, jnp.maximum(v, 0.0) + e)


def _inproj(x2d, g, wa, wb, ws, gm, qkg, rope, sb, B, S):
    T = x2d.shape[0]
    nst = S // TM_IN
    row = lambda i: (i, 0)
    const2 = lambda i: (0, 0)
    tr = lambda i: (i // nst, 0, i % nst)
    out_shape = [
        jax.ShapeDtypeStruct((T, ATT_W), BF16),
        jax.ShapeDtypeStruct((B, ATT_W, S), BF16),
        jax.ShapeDtypeStruct((T, ATT_W), BF16),
        jax.ShapeDtypeStruct((T, ATT_W), BF16),
        jax.ShapeDtypeStruct((B, ATT_W, S), BF16),
        jax.ShapeDtypeStruct((T, ATT_W), BF16),
        jax.ShapeDtypeStruct((T, SSD_INNER), F32),
        jax.ShapeDtypeStruct((T, SSD_CONV_CH), F32),
        jax.ShapeDtypeStruct((T, LANES), F32),
    ]
    att = pl.BlockSpec((TM_IN, ATT_W), row)
    att_t = pl.BlockSpec((1, ATT_W, TM_IN), tr)
    return pl.pallas_call(
        _inproj_kernel,
        grid=(T // TM_IN,),
        in_specs=[
            pl.BlockSpec((TM_IN, D_MODEL), row),
            pl.BlockSpec((1, D_MODEL), const2),
            pl.BlockSpec(wa.shape, const2),
            pl.BlockSpec(wb.shape, const2),
            pl.BlockSpec(ws.shape, const2),
            pl.BlockSpec((ATT_W, ATT_W), const2),
            pl.BlockSpec((4, 1, ATT_W), lambda i: (0, 0, 0)),
            pl.BlockSpec((3, TM_IN, ATT_W), lambda i: (0, i % nst, 0)),
            pl.BlockSpec((1, LANES), const2),
        ],
        out_specs=[att, att_t, att, att, att_t, att,
                   pl.BlockSpec((TM_IN, SSD_INNER), row),
                   pl.BlockSpec((TM_IN, SSD_CONV_CH), row),
                   pl.BlockSpec((TM_IN, LANES), row)],
        out_shape=out_shape,
        compiler_params=_cparams(1),
        name="inproj",
    )(x2d, g, wa, wb, ws, gm, qkg, rope, sb)


def _fox_scan_kernel(sm_ref, tri_ref, pq_ref, pk_ref, oq_ref, ok_ref, augq_ref, augkt_ref):
    S = sm_ref.shape[1]
    blk = tri_ref.shape[0]
    carry = jnp.zeros((1, LANES), F32)
    for b in range(S // blk):
        rows = slice(b * blk, (b + 1) * blk)
        c = _sum3(_dot(tri_ref[...], _split3(sm_ref[0, rows, :]))) + carry
        carry = c[blk - 1:blk, :]
        c3 = _split3(c)
        for h in range(ATT_HEADS):
            cols = slice(LANES * h, LANES * (h + 1))
            augq_ref[0, rows, cols] = (_dot(c3, pq_ref[h]) + oq_ref[h]).astype(BF16)
            augkt_ref[0, cols, rows] = (_dot(c3, pk_ref[h]) + ok_ref[h]).T.astype(BF16)


def _fox_scan(small, consts, B, S):
    tri, pq, pk, oq, ok = consts
    c2 = lambda b: (0, 0)
    c3 = lambda b: (0, 0, 0)
    return pl.pallas_call(
        _fox_scan_kernel,
        grid=(B,),
        in_specs=[
            pl.BlockSpec((1, S, LANES), lambda b: (b, 0, 0)),
            pl.BlockSpec(tri.shape, c2),
            pl.BlockSpec(pq.shape, c3),
            pl.BlockSpec(pk.shape, c3),
            pl.BlockSpec(oq.shape, c3),
            pl.BlockSpec(ok.shape, c3),
        ],
        out_specs=[pl.BlockSpec((1, S, ATT_HEADS * LANES), lambda b: (b, 0, 0)),
                   pl.BlockSpec((1, ATT_HEADS * LANES, S), lambda b: (b, 0, 0))],
        out_shape=[jax.ShapeDtypeStruct((B, S, ATT_HEADS * LANES), BF16),
                   jax.ShapeDtypeStruct((B, ATT_HEADS * LANES, S), BF16)],
        compiler_params=_cparams(1),
        name="fox_scan",
    )(small.reshape(B, S, LANES), tri, pq, pk, oq, ok)


def _attn_kernel(*refs, fox, nk):
    if fox:
        q_ref, kt_ref, v_ref, lm_ref, augq_ref, augkt_ref, o_ref, kt_scr, v_scr = refs
    else:
        q_ref, kt_ref, v_ref, lm_ref, o_ref, kt_scr, v_scr = refs
    qi = pl.program_id(1)

    @pl.when(qi == 0)
    def _prep():
        row = lax.broadcasted_iota(jnp.int32, (LANES, TK), 0)
        lane = lax.broadcasted_iota(jnp.int32, (TK, LANES), 1)
        for h in range(ATT_HEADS):
            p, mem = divmod(h, 2)
            pair = slice(LANES * p, LANES * (p + 1))
            for j in range(nk):
                keys = slice(j * TK, (j + 1) * TK)
                kd = kt_ref[0, pair, keys]
                if fox:
                    other = augkt_ref[0, LANES * h:LANES * (h + 1), keys]
                else:
                    other = jnp.zeros_like(kd)
                kt_scr[h, j] = jnp.where((row >> 6) == mem, kd, other)
                vd = v_ref[keys, pair]
                v_scr[h, j] = jnp.where((lane >> 6) == mem, vd, jnp.ones_like(vd))

    qlane = lax.broadcasted_iota(jnp.int32, (TQ, LANES), 1)
    qas = []
    for h in range(ATT_HEADS):
        p, mem = divmod(h, 2)
        qd = q_ref[:, LANES * p:LANES * (p + 1)]
        if fox:
            other = augq_ref[:, LANES * h:LANES * (h + 1)]
        else:
            other = jnp.zeros_like(qd)
        qas.append(jnp.where((qlane >> 6) == mem, qd, other))

    def step(j, carry, table):
        new = []
        for h in range(ATT_HEADS):
            m, acc = carry[h]
            s = _dot(qas[h], kt_scr[h, j])
            if table is not None:
                s = s + table
            m_new = jnp.maximum(m, jnp.max(s, axis=1, keepdims=True))
            alpha = jnp.exp(m - m_new)
            pr = jnp.exp(s - m_new).astype(BF16)
            new.append((m_new, alpha * acc + _dot(pr, v_scr[h, j])))
        return tuple(new)

    init = tuple((jnp.full((TQ, 1), NEG, F32), jnp.zeros((TQ, LANES), F32))
                 for _ in range(ATT_HEADS))
    if fox:
        carry = lax.fori_loop(0, qi, lambda j, c: step(j, c, None), init)
        carry = step(qi, carry, lm_ref[0])
    else:
        carry = lax.fori_loop(0, qi + 1, lambda j, c: step(j, c, lm_ref[qi - j]), init)
    outs = [acc / pltpu.roll(acc, HEAD_DIM, 1) for _, acc in carry]
    for p in range(ATT_HEADS // 2):
        o_ref[:, LANES * p:LANES * (p + 1)] = jnp.where(
            (qlane >> 6) == 0, outs[2 * p], outs[2 * p + 1]).astype(BF16)


def _attention(q, kt, v, lm, aug, B, S):
    fox = aug is not None
    nq, nk = S // TQ, S // TK
    in_specs = [
        pl.BlockSpec((TQ, ATT_W), lambda b, i: (b * nq + i, 0)),
        pl.BlockSpec((1, ATT_W, S), lambda b, i: (b, 0, 0)),
        pl.BlockSpec((S, ATT_W), lambda b, i: (b, 0)),
        pl.BlockSpec(lm.shape, lambda b, i: (0, 0, 0)),
    ]
    args = [q, kt, v, lm]
    if fox:
        in_specs += [pl.BlockSpec((TQ, ATT_HEADS * LANES), lambda b, i: (b * nq + i, 0)),
                     pl.BlockSpec((1, ATT_HEADS * LANES, S), lambda b, i: (b, 0, 0))]
        args += [aug[0].reshape(B * S, ATT_HEADS * LANES), aug[1]]
    return pl.pallas_call(
        functools.partial(_attn_kernel, fox=fox, nk=nk),
        grid=(B, nq),
        in_specs=in_specs,
        out_specs=pl.BlockSpec((TQ, ATT_W), lambda b, i: (b * nq + i, 0)),
        out_shape=jax.ShapeDtypeStruct((B * S, ATT_W), BF16),
        scratch_shapes=[pltpu.VMEM((ATT_HEADS, nk, LANES, TK), BF16),
                        pltpu.VMEM((ATT_HEADS, nk, TK, LANES), BF16)],
        compiler_params=_cparams(2),
        name="fox_attn" if fox else "dil_attn",
    )(*args)


def _ssd_kernel(xbc_ref, z_ref, sm_ref, cw_ref, cb_ref, arow_ref, dx_ref, ng_ref,
                tri_ref, pexp_ref, pq_ref, pk_ref, oq_ref, ok_ref, o_ref, buf, state):
    Q = SSD_CHUNK

    @pl.when(pl.program_id(1) == 0)
    def _reset():
        buf[0:8, :] = jnp.zeros((8, SSD_CONV_CH), F32)
        state[...] = jnp.zeros(state.shape, F32)

    xb = xbc_ref[...]
    buf[8:8 + Q, :] = xb
    conv = cb_ref[...]
    for k in range(SSD_CONV):
        off = 8 - (SSD_CONV - 1) + k
        conv = conv + cw_ref[k:k + 1, :] * buf[off:off + Q, :]
    buf[0:8, :] = xb[Q - 8:Q, :]
    act = conv * jax.nn.sigmoid(conv)
    xs = act[:, 0:SSD_INNER]
    bm = act[:, SSD_INNER:SSD_INNER + 2 * SSD_STATE]
    cm = act[:, SSD_INNER + 2 * SSD_STATE:]

    dt = sm_ref[...]
    acs = _sum3(_dot(tri_ref[...], _split3(dt * arow_ref[...])))
    acs3 = _split3(acs)
    ax = _dot(acs3, pexp_ref[...])
    dtx = _dot(_split3(dt), pexp_ref[...])
    last = ax[Q - 1:Q, :]
    ea = jnp.exp(ax)
    cdec = jnp.exp(last)
    xc = xs * dtx
    xcb = xc.astype(BF16)
    xcd = (xc * jnp.exp(last - ax)).astype(BF16)
    uq = (_dot(acs3, pq_ref[...]) + oq_ref[...]).astype(BF16)
    uk = (_dot(acs3, pk_ref[...]) + ok_ref[...]).astype(BF16)

    tril = (lax.broadcasted_iota(jnp.int32, (Q, Q), 0) >= lax.broadcasted_iota(jnp.int32, (Q, Q), 1))
    first = lax.broadcasted_iota(jnp.int32, (Q, LANES), 1) < HEAD_DIM
    ys = []
    for g in range(2):
        gs = slice(SSD_STATE * g, SSD_STATE * (g + 1))
        bg = bm[:, gs]
        cg = cm[:, gs].astype(BF16)
        cbm = _dot_nt(cg, bg.astype(BF16))
        bgt = bg.T.astype(BF16)
        for pp in range(2):
            p = 2 * g + pp
            ps = slice(LANES * p, LANES * (p + 1))
            ms = []
            for mem in range(2):
                hs = slice(LANES * (2 * p + mem), LANES * (2 * p + mem + 1))
                dm = _dot_nt(uq[:, hs], uk[:, hs])
                ms.append((cbm * jnp.exp(jnp.where(tril, dm, NEG))).astype(BF16))
            xp = xcb[:, ps]
            zero = jnp.zeros_like(xp)
            xcat = jnp.concatenate([jnp.where(first, xp, zero), jnp.where(first, zero, xp)], axis=0)
            y_diag = _dot(jnp.concatenate(ms, axis=1), xcat)
            st = state[p]
            y_off = _dot(cg, st.astype(BF16)) * ea[:, ps]
            state[p] = cdec[:, ps] * st + _dot(bgt, xcd[:, ps])
            ys.append(y_diag + y_off + xs[:, ps] * dx_ref[:, ps])
    y = jnp.concatenate(ys, axis=1)
    zz = z_ref[...]
    o_ref[...] = _rms(y * (zz * jax.nn.sigmoid(zz)), ng_ref[...]).astype(BF16)


def _ssd(xbc, z, small, cw, cb, arow, dx, ng, consts, B, S):
    nc = S // SSD_CHUNK
    row = lambda b, c: (b * nc + c, 0)
    c2 = lambda b, c: (0, 0)
    full = lambda a: pl.BlockSpec(a.shape, c2)
    return pl.pallas_call(
        _ssd_kernel,
        grid=(B, nc),
        in_specs=[pl.BlockSpec((SSD_CHUNK, SSD_CONV_CH), row),
                  pl.BlockSpec((SSD_CHUNK, SSD_INNER), row),
                  pl.BlockSpec((SSD_CHUNK, LANES), row),
                  full(cw), full(cb), full(arow), full(dx), full(ng)] + [full(a) for a in consts],
        out_specs=pl.BlockSpec((SSD_CHUNK, SSD_INNER), row),
        out_shape=jax.ShapeDtypeStruct((B * S, SSD_INNER), BF16),
        scratch_shapes=[pltpu.VMEM((8 + SSD_CHUNK, SSD_CONV_CH), F32),
                        pltpu.VMEM((SSD_HEADS // 2, SSD_STATE, LANES), F32)],
        compiler_params=_cparams(2),
        name="ssd",
    )(xbc, z, small, cw, cb, arow, dx, ng, *consts)


def _kv_kernel(mem_ref, g_ref, w_ref, kg_ref, kt_ref, v_ref):
    m = _rms(mem_ref[0], g_ref[...]).astype(BF16)
    kv = _dot(m, w_ref[...])
    for h in range(XA_HEADS):
        hs = slice(XA_HEAD_DIM * h, XA_HEAD_DIM * (h + 1))
        kt_ref[0, hs, :] = _rms(kv[:, hs], kg_ref[...]).T.astype(BF16)
    v_ref[0] = kv[:, D_MODEL:].astype(BF16)


def _kv(mem, g, w, kg):
    B = mem.shape[0]
    c2 = lambda b: (0, 0)
    return pl.pallas_call(
        _kv_kernel,
        grid=(B,),
        in_specs=[pl.BlockSpec((1, MEM_LEN, D_MODEL), lambda b: (b, 0, 0)),
                  pl.BlockSpec((1, D_MODEL), c2),
                  pl.BlockSpec((D_MODEL, 2 * D_MODEL), c2),
                  pl.BlockSpec((1, XA_HEAD_DIM), c2)],
        out_specs=[pl.BlockSpec((1, D_MODEL, MEM_LEN), lambda b: (b, 0, 0)),
                   pl.BlockSpec((1, MEM_LEN, D_MODEL), lambda b: (b, 0, 0))],
        out_shape=[jax.ShapeDtypeStruct((B, D_MODEL, MEM_LEN), BF16),
                   jax.ShapeDtypeStruct((B, MEM_LEN, D_MODEL), BF16)],
        compiler_params=_cparams(1),
        name="mem_kv",
    )(mem, g, w, kg)


ROW_SUB = D_MODEL // LANES


def _chunk(n, c):
    return pl.ds(c, n, stride=ROW_SUB)


def _to_row_tiles(ref, x):
    n = x.shape[0]
    for c in range(ROW_SUB):
        ref[_chunk(n, c), :] = x[:, LANES * c:LANES * (c + 1)]


def _from_row_tiles(ref):
    n = ref.shape[0] // ROW_SUB
    return jnp.concatenate([ref[_chunk(n, c), :] for c in range(ROW_SUB)], axis=1)


def _row_copy(src, dst, sem, src_row, dst_row):
    return pltpu.make_async_copy(
        src.at[pl.ds(pl.multiple_of(src_row * ROW_SUB, ROW_SUB), ROW_SUB)],
        dst.at[pl.ds(pl.multiple_of(dst_row * ROW_SUB, ROW_SUB), ROW_SUB)], sem)


def _mid_kernel(x_ref, of_ref, od_ref, os_ref, wo_ref, g2_ref, wq_ref, qg_ref, kt_ref, v_ref,
                wxo_ref, g3_ref, wra_ref, wrb_ref, rb_ref, tri_ref,
                x2_ref, h3_ref, ri_ref, cnt_ref, run):
    @pl.when(pl.program_id(0) == 0)
    def _reset():
        run[...] = jnp.zeros(run.shape, F32)

    x1 = (x_ref[...] + _dot(of_ref[...], wo_ref[0:ATT_W, :])
          + _dot(od_ref[...], wo_ref[ATT_W:2 * ATT_W, :])
          + _dot(os_ref[...], wo_ref[2 * ATT_W:, :]))

    q = _dot(_rms(x1, g2_ref[...]).astype(BF16), wq_ref[...])
    heads = []
    for h in range(XA_HEADS):
        hs = slice(XA_HEAD_DIM * h, XA_HEAD_DIM * (h + 1))
        qn = (_rms(q[:, hs], qg_ref[...]) * XA_HEAD_DIM ** -0.5).astype(BF16)
        s = _dot(qn, kt_ref[0, hs, :])
        e = jnp.exp(s - jnp.max(s, axis=1, keepdims=True))
        o = _dot(e.astype(BF16), v_ref[0, :, hs]) / jnp.sum(e, axis=1, keepdims=True)
        heads.append(o.astype(BF16))
    x2 = x1 + _dot(jnp.concatenate(heads, axis=1), wxo_ref[...])
    x2_ref[...] = x2

    h3 = _rms(x2, g3_ref[...])
    _to_row_tiles(h3_ref, h3)

    hi = h3.astype(BF16)
    mid = (h3 - hi.astype(F32)).astype(BF16)
    both = _dot(hi, wra_ref[...])
    logits = _dot(mid, wrb_ref[...]) + both[:, LANES:] + both[:, 0:LANES] + rb_ref[...]

    lane = lax.broadcasted_iota(jnp.int32, logits.shape, 1)
    lanef = lane.astype(F32)
    big = float(LANES)

    def first_max(vals):
        top = jnp.max(vals, axis=1, keepdims=True)
        return top, jnp.min(jnp.where(vals == top, lanef, big), axis=1, keepdims=True)

    gl = jnp.where(lane < N_GROUPS, logits, NEG)
    gmax, gsel = first_max(gl)
    ggate = 1.0 / jnp.sum(jnp.exp(gl - gmax), axis=1, keepdims=True)
    grp = ((lane - N_GROUPS) >> 3).astype(F32)
    el = jnp.where(grp == gsel, logits, NEG)
    v1, i1 = first_max(el)
    v2, i2 = first_max(jnp.where(lanef == i1, NEG, el))
    t = jnp.exp(v2 - v1)
    p1 = 1.0 / (1.0 + t)
    e1 = i1 - N_GROUPS
    e2 = i2 - N_GROUPS

    hit1 = lanef == e1
    hit2 = lanef == e2
    onehot = jnp.where(hit1 | hit2, 1.0, 0.0)
    before = _dot(tri_ref[...], onehot.astype(BF16)) + run[...]
    r1 = jnp.sum(jnp.where(hit1, before, 0.0), axis=1, keepdims=True)
    r2 = jnp.sum(jnp.where(hit2, before, 0.0), axis=1, keepdims=True)
    run[...] = run[...] + jnp.sum(onehot, axis=0, keepdims=True)

    cols = (e1, e2, p1 * ggate, t * p1 * ggate, r1, r2)
    info = jnp.zeros(logits.shape, F32)
    for k, col in enumerate(cols):
        info = jnp.where(lane == k, col, info)
    ri_ref[...] = info
    cnt_ref[...] = jnp.broadcast_to(run[...], cnt_ref.shape)


def _mid(x2d, o_fox, o_dil, o_ssd, wo, g2, wq, qg, kt, v, wxo, g3, wra, wrb, rb, tri, S):
    T = x2d.shape[0]
    npb = S // TM_MID
    row = lambda i: (i, 0)
    c2 = lambda i: (0, 0)
    return pl.pallas_call(
        _mid_kernel,
        grid=(T // TM_MID,),
        in_specs=[pl.BlockSpec((TM_MID, D_MODEL), row),
                  pl.BlockSpec((TM_MID, ATT_W), row),
                  pl.BlockSpec((TM_MID, ATT_W), row),
                  pl.BlockSpec((TM_MID, SSD_INNER), row),
                  pl.BlockSpec((D_MODEL, D_MODEL), c2),
                  pl.BlockSpec((1, D_MODEL), c2),
                  pl.BlockSpec((D_MODEL, D_MODEL), c2),
                  pl.BlockSpec((1, XA_HEAD_DIM), c2),
                  pl.BlockSpec((1, D_MODEL, MEM_LEN), lambda i: (i // npb, 0, 0)),
                  pl.BlockSpec((1, MEM_LEN, D_MODEL), lambda i: (i // npb, 0, 0)),
                  pl.BlockSpec((D_MODEL, D_MODEL), c2),
                  pl.BlockSpec((1, D_MODEL), c2),
                  pl.BlockSpec((D_MODEL, 2 * LANES), c2),
                  pl.BlockSpec((D_MODEL, LANES), c2),
                  pl.BlockSpec((1, LANES), c2),
                  pl.BlockSpec((TM_MID, TM_MID), c2)],
        out_specs=[pl.BlockSpec((TM_MID, D_MODEL), row),
                   pl.BlockSpec((TM_MID * ROW_SUB, LANES), row),
                   pl.BlockSpec((TM_MID, LANES), row),
                   pl.BlockSpec((8, LANES), c2)],
        out_shape=[jax.ShapeDtypeStruct((T, D_MODEL), F32),
                   jax.ShapeDtypeStruct((T * ROW_SUB, LANES), F32),
                   jax.ShapeDtypeStruct((T, LANES), F32),
                   jax.ShapeDtypeStruct((8, LANES), F32)],
        scratch_shapes=[pltpu.VMEM((1, LANES), F32)],
        compiler_params=_cparams(1),
        name="mid",
    )(x2d, o_fox, o_dil, o_ssd, wo, g2, wq, qg, kt, v, wxo, g3, wra, wrb, rb, tri)


def _dispatch_kernel(d0_ref, d1_ref, h_ref, xs_in, xs_out, sem):
    del xs_in

    def issue(r, carry):
        _row_copy(h_ref, xs_out, sem, r, d0_ref[r]).start()
        _row_copy(h_ref, xs_out, sem, r, d1_ref[r]).start()
        return carry

    def drain(r, carry):
        _row_copy(h_ref, xs_out, sem, 0, 0).wait()
        _row_copy(h_ref, xs_out, sem, 0, 0).wait()
        return carry

    lax.fori_loop(0, TD, issue, 0, unroll=8)
    lax.fori_loop(0, TD, drain, 0, unroll=8)


def _dispatch(dest0, dest1, h3t, xs_init):
    T = h3t.shape[0] // ROW_SUB
    idx = pl.BlockSpec((TD,), lambda i: (i,), memory_space=pltpu.SMEM)
    return pl.pallas_call(
        _dispatch_kernel,
        grid=(T // TD,),
        in_specs=[idx, idx,
                  pl.BlockSpec((TD * ROW_SUB, LANES), lambda i: (i, 0)),
                  pl.BlockSpec(memory_space=pl.ANY)],
        out_specs=pl.BlockSpec(memory_space=pl.ANY),
        out_shape=jax.ShapeDtypeStruct(xs_init.shape, F32),
        scratch_shapes=[pltpu.SemaphoreType.DMA],
        input_output_aliases={3: 0},
        compiler_params=_cparams(1),
        name="moe_dispatch",
    )(dest0, dest1, h3t, xs_init)


def _expert_kernel(be_ref, nu_ref, xs_ref, w1_ref, w3_ref, w2_ref, y_ref, w1s, w3s, w2s):
    i = pl.program_id(0)
    fresh = (i == 0) | (be_ref[i] != be_ref[jnp.maximum(i - 1, 0)])

    @pl.when(fresh)
    def _cast():
        w1s[...] = w1_ref[...].astype(BF16)
        w3s[...] = w3_ref[...].astype(BF16)
        w2s[...] = w2_ref[...].astype(BF16)

    @pl.when(i < nu_ref[0])
    def _run():
        xb = _from_row_tiles(xs_ref).astype(BF16)
        a = _dot(xb, w1s[...])
        b = _dot(xb, w3s[...])
        _to_row_tiles(y_ref, _dot((a * jax.nn.sigmoid(a) * b).astype(BF16), w2s[...]))

    @pl.when(i >= nu_ref[0])
    def _skip():
        y_ref[...] = jnp.zeros(y_ref.shape, F32)


def _experts(blk_expert, n_used, xs, w1, w3, w2, layer):
    nblk = xs.shape[0] // (TM_MOE * ROW_SUB)
    wmap = lambda i, be, nu: (layer, be[i], 0, 0)
    return pl.pallas_call(
        _expert_kernel,
        grid_spec=pltpu.PrefetchScalarGridSpec(
            num_scalar_prefetch=2,
            grid=(nblk,),
            in_specs=[pl.BlockSpec((TM_MOE * ROW_SUB, LANES),
                                   lambda i, be, nu: (jnp.minimum(i, nu[0] - 1), 0)),
                      pl.BlockSpec((None, None, D_MODEL, EXPERT_FF), wmap),
                      pl.BlockSpec((None, None, D_MODEL, EXPERT_FF), wmap),
                      pl.BlockSpec((None, None, EXPERT_FF, D_MODEL), wmap)],
            out_specs=pl.BlockSpec((TM_MOE * ROW_SUB, LANES), lambda i, be, nu: (i, 0)),
            scratch_shapes=[pltpu.VMEM((D_MODEL, EXPERT_FF), BF16),
                            pltpu.VMEM((D_MODEL, EXPERT_FF), BF16),
                            pltpu.VMEM((EXPERT_FF, D_MODEL), BF16)]),
        out_shape=jax.ShapeDtypeStruct(xs.shape, F32),
        compiler_params=_cparams(1),
        name="moe_experts",
    )(blk_expert, n_used, xs, w1, w3, w2)


def _combine_kernel(d0_ref, d1_ref, x2_ref, ri_ref, y_hbm, o_ref, buf, sem):
    def issue(r, carry):
        _row_copy(y_hbm, buf.at[0], sem, d0_ref[r], r).start()
        _row_copy(y_hbm, buf.at[1], sem, d1_ref[r], r).start()
        return carry

    def drain(r, carry):
        _row_copy(y_hbm, buf.at[0], sem, 0, 0).wait()
        _row_copy(y_hbm, buf.at[1], sem, 0, 0).wait()
        return carry

    lax.fori_loop(0, TC, issue, 0, unroll=8)
    lax.fori_loop(0, TC, drain, 0, unroll=8)
    info = ri_ref[...]
    g0 = info[:, 2:3]
    g1 = info[:, 3:4]
    for c in range(D_MODEL // LANES):
        cols = slice(LANES * c, LANES * (c + 1))
        o_ref[:, cols] = (x2_ref[:, cols] + g0 * buf[0, _chunk(TC, c), :]
                          + g1 * buf[1, _chunk(TC, c), :])


def _combine(dest0, dest1, x2, rinfo, ybuf):
    T = x2.shape[0]
    row = lambda i: (i, 0)
    idx = pl.BlockSpec((TC,), lambda i: (i,), memory_space=pltpu.SMEM)
    return pl.pallas_call(
        _combine_kernel,
        grid=(T // TC,),
        in_specs=[idx, idx,
                  pl.BlockSpec((TC, D_MODEL), row),
                  pl.BlockSpec((TC, LANES), row),
                  pl.BlockSpec(memory_space=pl.ANY)],
        out_specs=pl.BlockSpec((TC, D_MODEL), row),
        out_shape=jax.ShapeDtypeStruct((T, D_MODEL), F32),
        scratch_shapes=[pltpu.VMEM((2, TC * ROW_SUB, LANES), F32), pltpu.SemaphoreType.DMA],
        compiler_params=_cparams(1),
        name="moe_combine",
    )(dest0, dest1, x2, rinfo, ybuf)


def _tri(n, strict):
    return jnp.asarray(np.tril(np.ones((n, n), np.float32), -1 if strict else 0), BF16)


def _rope_tables(S):
    half = ROPE_DIM // 2
    inv = jnp.power(ROPE_THETA, -2.0 * jnp.arange(half, dtype=F32) / ROPE_DIM)
    ang = jnp.arange(S).astype(F32)[:, None] * inv[None, :]
    cos, sin = jnp.cos(ang), jnp.sin(ang)
    d = np.arange(ATT_W) % HEAD_DIM
    idx = d % half
    c = jnp.where(d < ROPE_DIM, cos[:, idx], 1.0)
    s1 = jnp.where(d < half, -sin[:, idx], 0.0)
    s2 = jnp.where((d >= half) & (d < ROPE_DIM), sin[:, idx], 0.0)
    return jnp.stack([c, s1, s2]).astype(F32)


def _fox_consts():
    pq = np.zeros((ATT_HEADS, 3 * LANES, LANES), np.float32)
    pk = np.zeros_like(pq)
    oq = np.zeros((ATT_HEADS, 1, LANES), np.float32)
    ok = np.zeros_like(oq)
    for h in range(ATT_HEADS):
        off = HEAD_DIM if h % 2 == 0 else 0
        for k in range(3):
            pq[h, k * LANES + h, off + k] = 1.0
            oq[h, 0, off + 3 + k] = 1.0
            pk[h, k * LANES + h, off + 3 + k] = -1.0
            ok[h, 0, off + k] = 1.0
    return (_tri(256, False), jnp.asarray(pq, BF16), jnp.asarray(pk, BF16),
            jnp.asarray(oq), jnp.asarray(ok))


def _ssd_consts():
    pexp = np.zeros((3 * LANES, SSD_INNER), np.float32)
    pq = np.zeros((3 * LANES, SSD_HEADS * LANES), np.float32)
    pk = np.zeros_like(pq)
    oq = np.zeros((1, SSD_HEADS * LANES), np.float32)
    ok = np.zeros_like(oq)
    for h in range(SSD_HEADS):
        for k in range(3):
            src = k * LANES + DT_LANE0 + h
            pexp[src, HEAD_DIM * h:HEAD_DIM * (h + 1)] = 1.0
            pq[src, LANES * h + k] = 1.0
            oq[0, LANES * h + 3 + k] = 1.0
            pk[src, LANES * h + 3 + k] = -1.0
            ok[0, LANES * h + k] = 1.0
    return (_tri(SSD_CHUNK, False), jnp.asarray(pexp, BF16), jnp.asarray(pq, BF16),
            jnp.asarray(pk, BF16), jnp.asarray(oq), jnp.asarray(ok))


def _score_tables(S):
    nd = S // TK
    i = np.arange(TQ)[:, None]
    j = np.arange(TK)[None, :]
    causal = np.where(i >= j, 0.0, NEG).astype(np.float32)[None]
    dil = np.zeros((nd, TQ, TK), np.float32)
    for d in range(nd):
        delta = d * TK + i - j
        mult = np.zeros((TQ, TK), np.float64)
        for window, step in DIL_CONFIGS:
            mult += (delta >= 0) & (delta <= window) & (delta % step == 0)
        dil[d] = np.where(mult > 0, np.log(np.maximum(mult, 1.0)), NEG)
    return jnp.asarray(causal), jnp.asarray(dil)


def _group_matrix():
    g = np.arange(ATT_W) // HEAD_DIM
    return jnp.asarray((g[:, None] == g[None, :]).astype(np.float32), BF16)


def _pad_lanes(v, lane0):
    return jnp.zeros((1, LANES), F32).at[0, lane0:lane0 + v.shape[0]].set(v)


def _layer_params(l, w_in, wprep_consts, fox_fgate_b, fox_qn_g, fox_kn_g, dil_qn_g, dil_kn_g,
                  ssd_dt_bias, ssd_A_log, ssd_D, router_wg, router_bg, router_we, router_be):
    tail = jnp.pad(w_in[l, :, MAIN_W:], ((0, 0), (0, IN_W - IN_SRC_W)))
    w_r = _wprep(w_in, l, tail, *wprep_consts)
    tile4 = lambda g: jnp.tile(g, ATT_HEADS)[None, :]
    qkg = jnp.stack([tile4(fox_qn_g[l]), tile4(fox_kn_g[l]), tile4(dil_qn_g[l]), tile4(dil_kn_g[l])])
    sb = _pad_lanes(fox_fgate_b[l], 0) + _pad_lanes(ssd_dt_bias[l], DT_LANE0)
    arow = _pad_lanes(-jnp.exp(ssd_A_log[l]), DT_LANE0)
    dx = jnp.repeat(ssd_D[l], HEAD_DIM)[None, :]
    wr = jnp.concatenate([router_wg[l], router_we[l],
                          jnp.zeros((D_MODEL, LANES - N_GROUPS - N_EXPERTS), F32)], axis=1)
    wr3 = _split3(wr)
    rb = _pad_lanes(router_bg[l], 0) + _pad_lanes(router_be[l], N_GROUPS)
    return w_r, qkg, sb, arow, dx, wr3[:, 0:2 * LANES], wr3[:, 0:LANES], rb


def kernel(x, mem, norm1_g, w_in, fox_fgate_b, fox_qn_g, fox_kn_g, dil_qn_g, dil_kn_g, ssd_conv_w,
           ssd_conv_b, ssd_dt_bias, ssd_A_log, ssd_D, ssd_norm_g, w_out, norm2_g, mem_norm_g, xa_wq,
           xa_wkv, xa_qn_g, xa_kn_g, xa_wo, norm3_g, router_wg, router_bg, router_we, router_be,
           exp_w1, exp_w3, exp_w2):
    B, S, _ = x.shape
    T = B * S
    depth = w_in.shape[0]
    assert S % TM_IN == 0 and S % TQ == 0 and T % TD == 0 and S >= DIL_CONFIGS[-1][0]

    rope = _rope_tables(S)
    fox_consts = _fox_consts()
    ssd_consts = _ssd_consts()
    causal, dil_tab = _score_tables(S)
    gm = _group_matrix()
    wprep_consts = _wprep_consts()
    tri_mid = _tri(TM_MID, True)
    nblk = (2 * T) // TM_MOE + N_EXPERTS

    x2d = x.reshape(T, D_MODEL)
    xs = jnp.zeros((nblk * TM_MOE * ROW_SUB, LANES), F32)
    for l in range(depth):
        w_r, qkg, sb, arow, dx, wra, wrb, rb = _layer_params(
            l, w_in, wprep_consts, fox_fgate_b, fox_qn_g, fox_kn_g, dil_qn_g, dil_kn_g,
            ssd_dt_bias, ssd_A_log, ssd_D, router_wg, router_bg, router_we, router_be)

        fq, fkt, fv, dq, dkt, dv, z, xbc, small = _inproj(
            x2d, norm1_g[l][None, :], *w_r, gm, qkg, rope, sb, B, S)
        aug = _fox_scan(small, fox_consts, B, S)
        o_fox = _attention(fq, fkt, fv, causal, aug, B, S)
        o_dil = _attention(dq, dkt, dv, dil_tab, None, B, S)
        o_ssd = _ssd(xbc, z, small, ssd_conv_w[l], ssd_conv_b[l][None, :], arow, dx,
                     ssd_norm_g[l][None, :], ssd_consts, B, S)

        kt, v = _kv(mem, mem_norm_g[l][None, :], xa_wkv[l].astype(BF16), xa_kn_g[l][None, :])
        x2, h3t, rinfo, cnt = _mid(x2d, o_fox, o_dil, o_ssd, w_out[l].astype(BF16),
                                   norm2_g[l][None, :], xa_wq[l].astype(BF16), xa_qn_g[l][None, :],
                                   kt, v, xa_wo[l].astype(BF16), norm3_g[l][None, :], wra, wrb, rb,
                                   tri_mid, S)

        counts = cnt[0, :N_EXPERTS].astype(jnp.int32)
        padded = (counts + TM_MOE - 1) // TM_MOE * TM_MOE
        ends = jnp.cumsum(padded)
        starts = ends - padded
        dest0 = starts[rinfo[:, 0].astype(jnp.int32)] + rinfo[:, 4].astype(jnp.int32)
        dest1 = starts[rinfo[:, 1].astype(jnp.int32)] + rinfo[:, 5].astype(jnp.int32)
        n_used = (ends[-1:] // TM_MOE).astype(jnp.int32)
        blk_start = jnp.arange(nblk, dtype=jnp.int32) * TM_MOE
        blk_expert = jnp.minimum(jnp.sum(ends[None, :] <= blk_start[:, None], axis=1),
                                 N_EXPERTS - 1).astype(jnp.int32)

        xs = _dispatch(dest0, dest1, h3t, xs)
        ybuf = _experts(blk_expert, n_used, xs, exp_w1, exp_w3, exp_w2, l)
        x2d = _combine(dest0, dest1, x2, rinfo, ybuf)
    return x2d.reshape(B, S, D_MODEL)
```

```python
import functools

import jax
import jax.numpy as jnp
import numpy as np
from jax import lax
from jax.experimental import pallas as pl
from jax.experimental.pallas import tpu as pltpu

F32 = jnp.float32
BF16 = jnp.bfloat16

D_MODEL = 1024
HEAD_DIM = 64
ATT_HEADS = 4
ATT_W = ATT_HEADS * HEAD_DIM
SSD_HEADS = 8
SSD_INNER = 512
SSD_STATE = 128
SSD_CONV = 4
SSD_CHUNK = 128
SSD_CONV_CH = 1024
XA_HEADS = 4
XA_HEAD_DIM = 256
MEM_LEN = 256
N_GROUPS = 4
EXPERTS_PER_GROUP = 8
N_EXPERTS = 32
EXPERT_FF = 512
DIL_CONFIGS = ((128, 1), (512, 4), (2048, 16))
ROPE_THETA = 500000.0
ROPE_DIM = 16
EPS = 1e-6
NEG = -1e30

LANES = 128
N_FGATE = 4
DT_LANE0 = 4
MAIN_W = 3 * ATT_W + 3 * ATT_W + SSD_INNER + SSD_CONV_CH
IN_W = MAIN_W + LANES

TM_IN = 512
TQ = 256
TK = 256
TM_MID = 512
MID_SUB = 256
TM_MOE = 256
TD = 1024
TC = 1024
VMEM_LIMIT = 48 * 1024 * 1024


def _cparams(n_axes):
    return pltpu.CompilerParams(dimension_semantics=("arbitrary",) * n_axes,
                                vmem_limit_bytes=VMEM_LIMIT)


def _rms(x, g):
    return x * lax.rsqrt(jnp.mean(x * x, axis=-1, keepdims=True) + EPS) * g


def _split3(x):
    hi = x.astype(BF16)
    r = x - hi.astype(F32)
    mid = r.astype(BF16)
    lo = (r - mid.astype(F32)).astype(BF16)
    return jnp.concatenate([hi, mid, lo], axis=1)


def _dot(a, b):
    return jnp.dot(a, b, preferred_element_type=F32)


def _dot_nt(a, b):
    return lax.dot_general(a, b, (((1,), (1,)), ((), ())), preferred_element_type=F32)


def _sum3(c):
    w = c.shape[1] // 3
    return c[:, 0:w] + c[:, w:2 * w] + c[:, 2 * w:3 * w]


FF_COL = 3 * ATT_W
DQ_COL = FF_COL + N_FGATE
DT_COL = DQ_COL + MAIN_W - 3 * ATT_W
IN_SRC_W = DT_COL + 8
WPREP_ROWS = 256


def _wprep_kernel(w_ref, tail_ref, sh_ref, shs_ref, wa_ref, wb_ref, ws_ref):
    wa_ref[...] = w_ref[:, 0:FF_COL].astype(BF16)
    lane = lax.broadcasted_iota(jnp.int32, tail_ref.shape, 1)
    tail = jnp.where(lane < IN_SRC_W - MAIN_W, tail_ref[...], 0.0).astype(BF16)
    nb = (MAIN_W - FF_COL) // LANES
    for n in range(nb):
        lo = FF_COL + LANES * n
        if n + 1 < nb:
            pair = w_ref[:, lo:lo + 2 * LANES].astype(BF16)
        else:
            pair = jnp.concatenate([w_ref[:, lo:lo + LANES].astype(BF16), tail], axis=1)
        wb_ref[:, LANES * n:LANES * (n + 1)] = _dot(pair, sh_ref[...]).astype(BF16)
    small = jnp.concatenate([w_ref[:, FF_COL:FF_COL + LANES].astype(BF16), tail], axis=1)
    ws_ref[...] = _dot(small, shs_ref[...]).astype(BF16)


def _wprep(w_in, layer, sh, shs):
    row = lambda i: (i, 0)
    c2 = lambda i: (0, 0)
    wb_w = MAIN_W - FF_COL
    return pl.pallas_call(
        _wprep_kernel,
        grid=(D_MODEL // WPREP_ROWS,),
        in_specs=[pl.BlockSpec((None, WPREP_ROWS, MAIN_W), lambda i: (layer, i, 0)),
                  pl.BlockSpec((None, WPREP_ROWS, LANES), lambda i: (layer, i, MAIN_W // LANES)),
                  pl.BlockSpec(sh.shape, c2), pl.BlockSpec(shs.shape, c2)],
        out_specs=[pl.BlockSpec((WPREP_ROWS, FF_COL), row),
                   pl.BlockSpec((WPREP_ROWS, wb_w), row),
                   pl.BlockSpec((WPREP_ROWS, LANES), row)],
        out_shape=[jax.ShapeDtypeStruct((D_MODEL, FF_COL), BF16),
                   jax.ShapeDtypeStruct((D_MODEL, wb_w), BF16),
                   jax.ShapeDtypeStruct((D_MODEL, LANES), BF16)],
        compiler_params=_cparams(1),
        name="wprep",
    )(w_in, w_in, sh, shs)


def _wprep_consts():
    sh = np.zeros((2 * LANES, LANES), np.float32)
    shs = np.zeros((2 * LANES, LANES), np.float32)
    off = DQ_COL - FF_COL
    for j in range(LANES):
        sh[j + off, j] = 1.0
    for k in range(N_FGATE):
        shs[k, k] = 1.0
    for k in range(SSD_HEADS):
        shs[LANES + DT_COL - MAIN_W + k, DT_LANE0 + k] = 1.0
    return jnp.asarray(sh, BF16), jnp.asarray(shs, BF16)


def _inproj_kernel(x_ref, g_ref, wa_ref, wb_ref, ws_ref, gm_ref, qkg_ref, rope_ref, sb_ref,
                   fq_ref, fkt_ref, fv_ref, dq_ref, dkt_ref, dv_ref, z_ref, xbc_ref, sm_ref):
    h = _rms(x_ref[...], g_ref[...]).astype(BF16)
    na = wa_ref.shape[1]

    def proj(a, b):
        if b <= na:
            return _dot(h, wa_ref[:, a:b])
        return _dot(h, wb_ref[:, a - na:b - na])

    def head_norm(a, idx):
        ssq = _dot((a * a).astype(BF16), gm_ref[...])
        return a * lax.rsqrt(ssq * (1.0 / HEAD_DIM) + EPS) * qkg_ref[idx]

    def rope(a):
        return (a * rope_ref[0] + pltpu.roll(a, ATT_W - ROPE_DIM // 2, 1) * rope_ref[1]
                + pltpu.roll(a, ROPE_DIM // 2, 1) * rope_ref[2])

    scale = HEAD_DIM ** -0.5
    fq_ref[...] = (head_norm(proj(0, 256), 0) * scale).astype(BF16)
    fkt_ref[0] = head_norm(proj(256, 512), 1).T.astype(BF16)
    fv_ref[...] = proj(512, 768).astype(BF16)
    dq_ref[...] = (rope(head_norm(proj(768, 1024), 2)) * scale).astype(BF16)
    dkt_ref[0] = rope(head_norm(proj(1024, 1280), 3)).T.astype(BF16)
    dv_ref[...] = proj(1280, 1536).astype(BF16)
    z_ref[...] = proj(1536, 2048)
    xbc_ref[...] = proj(2048, MAIN_W)
    v = _dot(h, ws_ref[...]) + sb_ref[...]
    e = jnp.log1p(jnp.exp(-jnp.abs(v)))
    lane = lax.broadcasted_iota(jnp.int32, v.shape, 1)
    sm_ref[...] = jnp.where(lane < N_FGATE, jnp.minimum(v, 0.0) - e, jnp.maximum(v, 0.0) + e)


def _inproj(x2d, g, wa, wb, ws, gm, qkg, rope, sb, B, S):
    T = x2d.shape[0]
    nst = S // TM_IN
    row = lambda i: (i, 0)
    const2 = lambda i: (0, 0)
    tr = lambda i: (i // nst, 0, i % nst)
    out_shape = [
        jax.ShapeDtypeStruct((T, ATT_W), BF16),
        jax.ShapeDtypeStruct((B, ATT_W, S), BF16),
        jax.ShapeDtypeStruct((T, ATT_W), BF16),
        jax.ShapeDtypeStruct((T, ATT_W), BF16),
        jax.ShapeDtypeStruct((B, ATT_W, S), BF16),
        jax.ShapeDtypeStruct((T, ATT_W), BF16),
        jax.ShapeDtypeStruct((T, SSD_INNER), F32),
        jax.ShapeDtypeStruct((T, SSD_CONV_CH), F32),
        jax.ShapeDtypeStruct((T, LANES), F32),
    ]
    att = pl.BlockSpec((TM_IN, ATT_W), row)
    att_t = pl.BlockSpec((1, ATT_W, TM_IN), tr)
    return pl.pallas_call(
        _inproj_kernel,
        grid=(T // TM_IN,),
        in_specs=[
            pl.BlockSpec((TM_IN, D_MODEL), row),
            pl.BlockSpec((1, D_MODEL), const2),
            pl.BlockSpec(wa.shape, const2),
            pl.BlockSpec(wb.shape, const2),
            pl.BlockSpec(ws.shape, const2),
            pl.BlockSpec((ATT_W, ATT_W), const2),
            pl.BlockSpec((4, 1, ATT_W), lambda i: (0, 0, 0)),
            pl.BlockSpec((3, TM_IN, ATT_W), lambda i: (0, i % nst, 0)),
            pl.BlockSpec((1, LANES), const2),
        ],
        out_specs=[att, att_t, att, att, att_t, att,
                   pl.BlockSpec((TM_IN, SSD_INNER), row),
                   pl.BlockSpec((TM_IN, SSD_CONV_CH), row),
                   pl.BlockSpec((TM_IN, LANES), row)],
        out_shape=out_shape,
        compiler_params=_cparams(1),
        name="inproj",
    )(x2d, g, wa, wb, ws, gm, qkg, rope, sb)


def _fox_scan_kernel(sm_ref, tri_ref, pq_ref, pk_ref, oq_ref, ok_ref, augq_ref, augkt_ref):
    S = sm_ref.shape[1]
    blk = tri_ref.shape[0]
    carry = jnp.zeros((1, LANES), F32)
    for b in range(S // blk):
        rows = slice(b * blk, (b + 1) * blk)
        c = _sum3(_dot(tri_ref[...], _split3(sm_ref[0, rows, :]))) + carry
        carry = c[blk - 1:blk, :]
        c3 = _split3(c)
        for h in range(ATT_HEADS):
            cols = slice(LANES * h, LANES * (h + 1))
            augq_ref[0, rows, cols] = (_dot(c3, pq_ref[h]) + oq_ref[h]).astype(BF16)
            augkt_ref[0, cols, rows] = (_dot(c3, pk_ref[h]) + ok_ref[h]).T.astype(BF16)


def _fox_scan(small, consts, B, S):
    tri, pq, pk, oq, ok = consts
    c2 = lambda b: (0, 0)
    c3 = lambda b: (0, 0, 0)
    return pl.pallas_call(
        _fox_scan_kernel,
        grid=(B,),
        in_specs=[
            pl.BlockSpec((1, S, LANES), lambda b: (b, 0, 0)),
            pl.BlockSpec(tri.shape, c2),
            pl.BlockSpec(pq.shape, c3),
            pl.BlockSpec(pk.shape, c3),
            pl.BlockSpec(oq.shape, c3),
            pl.BlockSpec(ok.shape, c3),
        ],
        out_specs=[pl.BlockSpec((1, S, ATT_HEADS * LANES), lambda b: (b, 0, 0)),
                   pl.BlockSpec((1, ATT_HEADS * LANES, S), lambda b: (b, 0, 0))],
        out_shape=[jax.ShapeDtypeStruct((B, S, ATT_HEADS * LANES), BF16),
                   jax.ShapeDtypeStruct((B, ATT_HEADS * LANES, S), BF16)],
        compiler_params=_cparams(1),
        name="fox_scan",
    )(small.reshape(B, S, LANES), tri, pq, pk, oq, ok)


def _attn_kernel(*refs, fox, nk):
    if fox:
        q_ref, kt_ref, v_ref, lm_ref, augq_ref, augkt_ref, o_ref, kt_scr, v_scr = refs
    else:
        q_ref, kt_ref, v_ref, lm_ref, o_ref, kt_scr, v_scr = refs
    qi = pl.program_id(1)

    @pl.when(qi == 0)
    def _prep():
        row = lax.broadcasted_iota(jnp.int32, (LANES, TK), 0)
        lane = lax.broadcasted_iota(jnp.int32, (TK, LANES), 1)
        for h in range(ATT_HEADS):
            p, mem = divmod(h, 2)
            pair = slice(LANES * p, LANES * (p + 1))
            for j in range(nk):
                keys = slice(j * TK, (j + 1) * TK)
                kd = kt_ref[0, pair, keys]
                if fox:
                    other = augkt_ref[0, LANES * h:LANES * (h + 1), keys]
                else:
                    other = jnp.zeros_like(kd)
                kt_scr[h, j] = jnp.where((row >> 6) == mem, kd, other)
                vd = v_ref[keys, pair]
                v_scr[h, j] = jnp.where((lane >> 6) == mem, vd, jnp.ones_like(vd))

    qlane = lax.broadcasted_iota(jnp.int32, (TQ, LANES), 1)
    qas = []
    for h in range(ATT_HEADS):
        p, mem = divmod(h, 2)
        qd = q_ref[:, LANES * p:LANES * (p + 1)]
        if fox:
            other = augq_ref[:, LANES * h:LANES * (h + 1)]
        else:
            other = jnp.zeros_like(qd)
        qas.append(jnp.where((qlane >> 6) == mem, qd, other))

    def step(j, carry, table):
        new = []
        for h in range(ATT_HEADS):
            m, acc = carry[h]
            s = _dot(qas[h], kt_scr[h, j])
            if table is not None:
                s = s + table
            m_new = jnp.maximum(m, jnp.max(s, axis=1, keepdims=True))
            alpha = jnp.exp(m - m_new)
            pr = jnp.exp(s - m_new).astype(BF16)
            new.append((m_new, alpha * acc + _dot(pr, v_scr[h, j])))
        return tuple(new)

    init = tuple((jnp.full((TQ, 1), NEG, F32), jnp.zeros((TQ, LANES), F32))
                 for _ in range(ATT_HEADS))
    if fox:
        carry = lax.fori_loop(0, qi, lambda j, c: step(j, c, None), init)
        carry = step(qi, carry, lm_ref[0])
    else:
        carry = lax.fori_loop(0, qi + 1, lambda j, c: step(j, c, lm_ref[qi - j]), init)
    outs = [acc / pltpu.roll(acc, HEAD_DIM, 1) for _, acc in carry]
    for p in range(ATT_HEADS // 2):
        o_ref[:, LANES * p:LANES * (p + 1)] = jnp.where(
            (qlane >> 6) == 0, outs[2 * p], outs[2 * p + 1]).astype(BF16)


def _attention(q, kt, v, lm, aug, B, S):
    fox = aug is not None
    nq, nk = S // TQ, S // TK
    in_specs = [
        pl.BlockSpec((TQ, ATT_W), lambda b, i: (b * nq + i, 0)),
        pl.BlockSpec((1, ATT_W, S), lambda b, i: (b, 0, 0)),
        pl.BlockSpec((S, ATT_W), lambda b, i: (b, 0)),
        pl.BlockSpec(lm.shape, lambda b, i: (0, 0, 0)),
    ]
    args = [q, kt, v, lm]
    if fox:
        in_specs += [pl.BlockSpec((TQ, ATT_HEADS * LANES), lambda b, i: (b * nq + i, 0)),
                     pl.BlockSpec((1, ATT_HEADS * LANES, S), lambda b, i: (b, 0, 0))]
        args += [aug[0].reshape(B * S, ATT_HEADS * LANES), aug[1]]
    return pl.pallas_call(
        functools.partial(_attn_kernel, fox=fox, nk=nk),
        grid=(B, nq),
        in_specs=in_specs,
        out_specs=pl.BlockSpec((TQ, ATT_W), lambda b, i: (b * nq + i, 0)),
        out_shape=jax.ShapeDtypeStruct((B * S, ATT_W), BF16),
        scratch_shapes=[pltpu.VMEM((ATT_HEADS, nk, LANES, TK), BF16),
                        pltpu.VMEM((ATT_HEADS, nk, TK, LANES), BF16)],
        compiler_params=_cparams(2),
        name="fox_attn" if fox else "dil_attn",
    )(*args)


def _ssd_kernel(xbc_ref, z_ref, sm_ref, cw_ref, cb_ref, arow_ref, dx_ref, ng_ref,
                tri_ref, pexp_ref, pq_ref, pk_ref, oq_ref, ok_ref, o_ref, buf, state):
    Q = SSD_CHUNK

    @pl.when(pl.program_id(1) == 0)
    def _reset():
        buf[0:8, :] = jnp.zeros((8, SSD_CONV_CH), F32)
        state[...] = jnp.zeros(state.shape, F32)

    xb = xbc_ref[...]
    buf[8:8 + Q, :] = xb
    conv = cb_ref[...]
    for k in range(SSD_CONV):
        off = 8 - (SSD_CONV - 1) + k
        conv = conv + cw_ref[k:k + 1, :] * buf[off:off + Q, :]
    buf[0:8, :] = xb[Q - 8:Q, :]
    act = conv * jax.nn.sigmoid(conv)
    xs = act[:, 0:SSD_INNER]
    bm = act[:, SSD_INNER:SSD_INNER + 2 * SSD_STATE]
    cm = act[:, SSD_INNER + 2 * SSD_STATE:]

    dt = sm_ref[...]
    acs = _sum3(_dot(tri_ref[...], _split3(dt * arow_ref[...])))
    acs3 = _split3(acs)
    ax = _dot(acs3, pexp_ref[...])
    dtx = _dot(_split3(dt), pexp_ref[...])
    last = ax[Q - 1:Q, :]
    ea = jnp.exp(ax)
    cdec = jnp.exp(last)
    xc = xs * dtx
    xcb = xc.astype(BF16)
    xcd = (xc * jnp.exp(last - ax)).astype(BF16)
    uq = (_dot(acs3, pq_ref[...]) + oq_ref[...]).astype(BF16)
    uk = (_dot(acs3, pk_ref[...]) + ok_ref[...]).astype(BF16)

    tril = (lax.broadcasted_iota(jnp.int32, (Q, Q), 0) >= lax.broadcasted_iota(jnp.int32, (Q, Q), 1))
    first = lax.broadcasted_iota(jnp.int32, (Q, LANES), 1) < HEAD_DIM
    ys = []
    for g in range(2):
        gs = slice(SSD_STATE * g, SSD_STATE * (g + 1))
        bg = bm[:, gs]
        cg = cm[:, gs].astype(BF16)
        cbm = _dot_nt(cg, bg.astype(BF16))
        bgt = bg.T.astype(BF16)
        for pp in range(2):
            p = 2 * g + pp
            ps = slice(LANES * p, LANES * (p + 1))
            ms = []
            for mem in range(2):
                hs = slice(LANES * (2 * p + mem), LANES * (2 * p + mem + 1))
                dm = _dot_nt(uq[:, hs], uk[:, hs])
                ms.append((cbm * jnp.exp(jnp.where(tril, dm, NEG))).astype(BF16))
            xp = xcb[:, ps]
            zero = jnp.zeros_like(xp)
            xcat = jnp.concatenate([jnp.where(first, xp, zero), jnp.where(first, zero, xp)], axis=0)
            y_diag = _dot(jnp.concatenate(ms, axis=1), xcat)
            st = state[p]
            y_off = _dot(cg, st.astype(BF16)) * ea[:, ps]
            state[p] = cdec[:, ps] * st + _dot(bgt, xcd[:, ps])
            ys.append(y_diag + y_off + xs[:, ps] * dx_ref[:, ps])
    y = jnp.concatenate(ys, axis=1)
    zz = z_ref[...]
    o_ref[...] = _rms(y * (zz * jax.nn.sigmoid(zz)), ng_ref[...]).astype(BF16)


def _ssd(xbc, z, small, cw, cb, arow, dx, ng, consts, B, S):
    nc = S // SSD_CHUNK
    row = lambda b, c: (b * nc + c, 0)
    c2 = lambda b, c: (0, 0)
    full = lambda a: pl.BlockSpec(a.shape, c2)
    return pl.pallas_call(
        _ssd_kernel,
        grid=(B, nc),
        in_specs=[pl.BlockSpec((SSD_CHUNK, SSD_CONV_CH), row),
                  pl.BlockSpec((SSD_CHUNK, SSD_INNER), row),
                  pl.BlockSpec((SSD_CHUNK, LANES), row),
                  full(cw), full(cb), full(arow), full(dx), full(ng)] + [full(a) for a in consts],
        out_specs=pl.BlockSpec((SSD_CHUNK, SSD_INNER), row),
        out_shape=jax.ShapeDtypeStruct((B * S, SSD_INNER), BF16),
        scratch_shapes=[pltpu.VMEM((8 + SSD_CHUNK, SSD_CONV_CH), F32),
                        pltpu.VMEM((SSD_HEADS // 2, SSD_STATE, LANES), F32)],
        compiler_params=_cparams(2),
        name="ssd",
    )(xbc, z, small, cw, cb, arow, dx, ng, *consts)


def _kv_kernel(mem_ref, g_ref, w_ref, kg_ref, kt_ref, v_ref):
    m = _rms(mem_ref[0], g_ref[...]).astype(BF16)
    kv = _dot(m, w_ref[...])
    for h in range(XA_HEADS):
        hs = slice(XA_HEAD_DIM * h, XA_HEAD_DIM * (h + 1))
        kt_ref[0, hs, :] = _rms(kv[:, hs], kg_ref[...]).T.astype(BF16)
    v_ref[0] = kv[:, D_MODEL:].astype(BF16)


def _kv(mem, g, w, kg):
    B = mem.shape[0]
    c2 = lambda b: (0, 0)
    return pl.pallas_call(
        _kv_kernel,
        grid=(B,),
        in_specs=[pl.BlockSpec((1, MEM_LEN, D_MODEL), lambda b: (b, 0, 0)),
                  pl.BlockSpec((1, D_MODEL), c2),
                  pl.BlockSpec((D_MODEL, 2 * D_MODEL), c2),
                  pl.BlockSpec((1, XA_HEAD_DIM), c2)],
        out_specs=[pl.BlockSpec((1, D_MODEL, MEM_LEN), lambda b: (b, 0, 0)),
                   pl.BlockSpec((1, MEM_LEN, D_MODEL), lambda b: (b, 0, 0))],
        out_shape=[jax.ShapeDtypeStruct((B, D_MODEL, MEM_LEN), BF16),
                   jax.ShapeDtypeStruct((B, MEM_LEN, D_MODEL), BF16)],
        compiler_params=_cparams(1),
        name="mem_kv",
    )(mem, g, w, kg)


ROW_SUB = D_MODEL // LANES


def _chunk(n, c):
    return pl.ds(c, n, stride=ROW_SUB)


def _to_row_tiles(ref, x, row0=0):
    n = x.shape[0]
    for c in range(ROW_SUB):
        ref[pl.ds(row0 * ROW_SUB + c, n, stride=ROW_SUB), :] = x[:, LANES * c:LANES * (c + 1)]


def _from_row_tiles(ref):
    n = ref.shape[0] // ROW_SUB
    return jnp.concatenate([ref[_chunk(n, c), :] for c in range(ROW_SUB)], axis=1)


def _row_copy(src, dst, sem, src_row, dst_row):
    return pltpu.make_async_copy(
        src.at[pl.ds(pl.multiple_of(src_row * ROW_SUB, ROW_SUB), ROW_SUB)],
        dst.at[pl.ds(pl.multiple_of(dst_row * ROW_SUB, ROW_SUB), ROW_SUB)], sem)


def _mid_kernel(x_ref, of_ref, od_ref, os_ref, wo_ref, g2_ref, wq_ref, qg_ref, kt_ref, v_ref,
                wxo_ref, g3_ref, wra_ref, wrb_ref, rb_ref, tri_ref,
                x2_ref, h3_ref, ri_ref, cnt_ref, run):
    @pl.when(pl.program_id(0) == 0)
    def _reset():
        run[...] = jnp.zeros(run.shape, F32)

    for sub in range(TM_MID // MID_SUB):
        _mid_rows(slice(MID_SUB * sub, MID_SUB * (sub + 1)), MID_SUB * sub,
                  x_ref, of_ref, od_ref, os_ref, wo_ref, g2_ref, wq_ref, qg_ref, kt_ref, v_ref,
                  wxo_ref, g3_ref, wra_ref, wrb_ref, rb_ref, tri_ref, x2_ref, h3_ref, ri_ref, run)
    cnt_ref[...] = jnp.broadcast_to(run[...], cnt_ref.shape)


def _mid_rows(rows, row0, x_ref, of_ref, od_ref, os_ref, wo_ref, g2_ref, wq_ref, qg_ref, kt_ref, v_ref,
              wxo_ref, g3_ref, wra_ref, wrb_ref, rb_ref, tri_ref, x2_ref, h3_ref, ri_ref, run):
    x1 = (x_ref[rows, :] + _dot(of_ref[rows, :], wo_ref[0:ATT_W, :])
          + _dot(od_ref[rows, :], wo_ref[ATT_W:2 * ATT_W, :])
          + _dot(os_ref[rows, :], wo_ref[2 * ATT_W:, :]))

    q = _dot(_rms(x1, g2_ref[...]).astype(BF16), wq_ref[...])
    heads = []
    for h in range(XA_HEADS):
        hs = slice(XA_HEAD_DIM * h, XA_HEAD_DIM * (h + 1))
        qn = (_rms(q[:, hs], qg_ref[...]) * XA_HEAD_DIM ** -0.5).astype(BF16)
        s = _dot(qn, kt_ref[0, hs, :])
        e = jnp.exp(s - jnp.max(s, axis=1, keepdims=True))
        o = _dot(e.astype(BF16), v_ref[0, :, hs]) / jnp.sum(e, axis=1, keepdims=True)
        heads.append(o.astype(BF16))
    x2 = x1 + _dot(jnp.concatenate(heads, axis=1), wxo_ref[...])
    x2_ref[rows, :] = x2

    h3 = _rms(x2, g3_ref[...])
    _to_row_tiles(h3_ref, h3, row0)

    hi = h3.astype(BF16)
    mid = (h3 - hi.astype(F32)).astype(BF16)
    both = _dot(hi, wra_ref[...])
    logits = _dot(mid, wrb_ref[...]) + both[:, LANES:] + both[:, 0:LANES] + rb_ref[...]

    lane = lax.broadcasted_iota(jnp.int32, logits.shape, 1)
    lanef = lane.astype(F32)
    big = float(LANES)

    def first_max(vals):
        top = jnp.max(vals, axis=1, keepdims=True)
        return top, jnp.min(jnp.where(vals == top, lanef, big), axis=1, keepdims=True)

    gl = jnp.where(lane < N_GROUPS, logits, NEG)
    gmax, gsel = first_max(gl)
    ggate = 1.0 / jnp.sum(jnp.exp(gl - gmax), axis=1, keepdims=True)
    grp = ((lane - N_GROUPS) >> 3).astype(F32)
    el = jnp.where(grp == gsel, logits, NEG)
    v1, i1 = first_max(el)
    v2, i2 = first_max(jnp.where(lanef == i1, NEG, el))
    t = jnp.exp(v2 - v1)
    p1 = 1.0 / (1.0 + t)
    e1 = i1 - N_GROUPS
    e2 = i2 - N_GROUPS

    hit1 = lanef == e1
    hit2 = lanef == e2
    onehot = jnp.where(hit1 | hit2, 1.0, 0.0)
    before = _dot(tri_ref[...], onehot.astype(BF16)) + run[...]
    r1 = jnp.sum(jnp.where(hit1, before, 0.0), axis=1, keepdims=True)
    r2 = jnp.sum(jnp.where(hit2, before, 0.0), axis=1, keepdims=True)
    run[...] = run[...] + jnp.sum(onehot, axis=0, keepdims=True)

    cols = (e1, e2, p1 * ggate, t * p1 * ggate, r1, r2)
    info = jnp.zeros(logits.shape, F32)
    for k, col in enumerate(cols):
        info = jnp.where(lane == k, col, info)
    ri_ref[rows, :] = info


def _mid(x2d, o_fox, o_dil, o_ssd, wo, g2, wq, qg, kt, v, wxo, g3, wra, wrb, rb, tri, S):
    T = x2d.shape[0]
    npb = S // TM_MID
    row = lambda i: (i, 0)
    c2 = lambda i: (0, 0)
    return pl.pallas_call(
        _mid_kernel,
        grid=(T // TM_MID,),
        in_specs=[pl.BlockSpec((TM_MID, D_MODEL), row),
                  pl.BlockSpec((TM_MID, ATT_W), row),
                  pl.BlockSpec((TM_MID, ATT_W), row),
                  pl.BlockSpec((TM_MID, SSD_INNER), row),
                  pl.BlockSpec((D_MODEL, D_MODEL), c2),
                  pl.BlockSpec((1, D_MODEL), c2),
                  pl.BlockSpec((D_MODEL, D_MODEL), c2),
                  pl.BlockSpec((1, XA_HEAD_DIM), c2),
                  pl.BlockSpec((1, D_MODEL, MEM_LEN), lambda i: (i // npb, 0, 0)),
                  pl.BlockSpec((1, MEM_LEN, D_MODEL), lambda i: (i // npb, 0, 0)),
                  pl.BlockSpec((D_MODEL, D_MODEL), c2),
                  pl.BlockSpec((1, D_MODEL), c2),
                  pl.BlockSpec((D_MODEL, 2 * LANES), c2),
                  pl.BlockSpec((D_MODEL, LANES), c2),
                  pl.BlockSpec((1, LANES), c2),
                  pl.BlockSpec((MID_SUB, MID_SUB), c2)],
        out_specs=[pl.BlockSpec((TM_MID, D_MODEL), row),
                   pl.BlockSpec((TM_MID * ROW_SUB, LANES), row),
                   pl.BlockSpec((TM_MID, LANES), row),
                   pl.BlockSpec((8, LANES), c2)],
        out_shape=[jax.ShapeDtypeStruct((T, D_MODEL), F32),
                   jax.ShapeDtypeStruct((T * ROW_SUB, LANES), F32),
                   jax.ShapeDtypeStruct((T, LANES), F32),
                   jax.ShapeDtypeStruct((8, LANES), F32)],
        scratch_shapes=[pltpu.VMEM((1, LANES), F32)],
        compiler_params=_cparams(1),
        name="mid",
    )(x2d, o_fox, o_dil, o_ssd, wo, g2, wq, qg, kt, v, wxo, g3, wra, wrb, rb, tri)


def _dest_kernel(ri_ref, st_ref, o_ref):
    info = ri_ref[...].T
    n = info.shape[1]
    expert = lax.broadcasted_iota(jnp.int32, (N_EXPERTS, n), 0).astype(F32)
    row = lax.broadcasted_iota(jnp.int32, (8, n), 0)
    out = jnp.zeros((8, n), F32)
    for k in range(2):
        start = jnp.sum(jnp.where(expert == info[k:k + 1, :], st_ref[...], 0.0), axis=0, keepdims=True)
        out = jnp.where(row == k, start + info[4 + k:5 + k, :], out)
    o_ref[...] = out.astype(jnp.int32)


def _dest(rinfo, starts):
    T = rinfo.shape[0]
    return pl.pallas_call(
        _dest_kernel,
        grid=(T // TD,),
        in_specs=[pl.BlockSpec((TD, LANES), lambda i: (i, 0)),
                  pl.BlockSpec((N_EXPERTS, 1), lambda i: (0, 0))],
        out_specs=pl.BlockSpec((8, TD), lambda i: (0, i)),
        out_shape=jax.ShapeDtypeStruct((8, T), jnp.int32),
        compiler_params=_cparams(1),
        name="moe_dest",
    )(rinfo, starts)


def _dispatch_kernel(d0_ref, d1_ref, h_ref, xs_in, xs_out, sem):
    del xs_in

    def issue(r, carry):
        _row_copy(h_ref, xs_out, sem, r, d0_ref[r]).start()
        _row_copy(h_ref, xs_out, sem, r, d1_ref[r]).start()
        return carry

    def drain(r, carry):
        _row_copy(h_ref, xs_out, sem, 0, 0).wait()
        _row_copy(h_ref, xs_out, sem, 0, 0).wait()
        return carry

    lax.fori_loop(0, TD, issue, 0, unroll=8)
    lax.fori_loop(0, TD, drain, 0, unroll=8)


def _dispatch(dest0, dest1, h3t, xs_init):
    T = h3t.shape[0] // ROW_SUB
    idx = pl.BlockSpec((TD,), lambda i: (i,), memory_space=pltpu.SMEM)
    return pl.pallas_call(
        _dispatch_kernel,
        grid=(T // TD,),
        in_specs=[idx, idx,
                  pl.BlockSpec((TD * ROW_SUB, LANES), lambda i: (i, 0)),
                  pl.BlockSpec(memory_space=pl.ANY)],
        out_specs=pl.BlockSpec(memory_space=pl.ANY),
        out_shape=jax.ShapeDtypeStruct(xs_init.shape, F32),
        scratch_shapes=[pltpu.SemaphoreType.DMA],
        input_output_aliases={3: 0},
        compiler_params=_cparams(1),
        name="moe_dispatch",
    )(dest0, dest1, h3t, xs_init)


def _expert_kernel(be_ref, nu_ref, xs_ref, w1_ref, w3_ref, w2_ref, y_ref, w1s, w3s, w2s):
    i = pl.program_id(0)
    fresh = (i == 0) | (be_ref[i] != be_ref[jnp.maximum(i - 1, 0)])

    @pl.when(fresh)
    def _cast():
        w1s[...] = w1_ref[...].astype(BF16)
        w3s[...] = w3_ref[...].astype(BF16)
        w2s[...] = w2_ref[...].astype(BF16)

    @pl.when(i < nu_ref[0])
    def _run():
        xb = _from_row_tiles(xs_ref).astype(BF16)
        a = _dot(xb, w1s[...])
        b = _dot(xb, w3s[...])
        _to_row_tiles(y_ref, _dot((a * jax.nn.sigmoid(a) * b).astype(BF16), w2s[...]))

    @pl.when(i >= nu_ref[0])
    def _skip():
        y_ref[...] = jnp.zeros(y_ref.shape, F32)


def _experts(blk_expert, n_used, xs, w1, w3, w2, layer):
    nblk = xs.shape[0] // (TM_MOE * ROW_SUB)
    wmap = lambda i, be, nu: (layer, be[i], 0, 0)
    return pl.pallas_call(
        _expert_kernel,
        grid_spec=pltpu.PrefetchScalarGridSpec(
            num_scalar_prefetch=2,
            grid=(nblk,),
            in_specs=[pl.BlockSpec((TM_MOE * ROW_SUB, LANES),
                                   lambda i, be, nu: (jnp.minimum(i, nu[0] - 1), 0)),
                      pl.BlockSpec((None, None, D_MODEL, EXPERT_FF), wmap),
                      pl.BlockSpec((None, None, D_MODEL, EXPERT_FF), wmap),
                      pl.BlockSpec((None, None, EXPERT_FF, D_MODEL), wmap)],
            out_specs=pl.BlockSpec((TM_MOE * ROW_SUB, LANES), lambda i, be, nu: (i, 0)),
            scratch_shapes=[pltpu.VMEM((D_MODEL, EXPERT_FF), BF16),
                            pltpu.VMEM((D_MODEL, EXPERT_FF), BF16),
                            pltpu.VMEM((EXPERT_FF, D_MODEL), BF16)]),
        out_shape=jax.ShapeDtypeStruct(xs.shape, F32),
        compiler_params=_cparams(1),
        name="moe_experts",
    )(blk_expert, n_used, xs, w1, w3, w2)


def _combine_kernel(d0_ref, d1_ref, d0n_ref, d1n_ref, x2_ref, ri_ref, y_hbm, o_ref, buf, sem):
    i = pl.program_id(0)
    slot = i % 2

    def gather(d0, d1, s):
        def issue(r, carry):
            _row_copy(y_hbm, buf.at[s, 0], sem.at[s], d0[r], r).start()
            _row_copy(y_hbm, buf.at[s, 1], sem.at[s], d1[r], r).start()
            return carry
        lax.fori_loop(0, TC, issue, 0, unroll=8)

    @pl.when(i == 0)
    def _first():
        gather(d0_ref, d1_ref, 0)

    @pl.when(i + 1 < pl.num_programs(0))
    def _next():
        gather(d0n_ref, d1n_ref, 1 - slot)

    def drain(r, carry):
        _row_copy(y_hbm, buf.at[slot, 0], sem.at[slot], 0, 0).wait()
        _row_copy(y_hbm, buf.at[slot, 1], sem.at[slot], 0, 0).wait()
        return carry

    lax.fori_loop(0, TC, drain, 0, unroll=8)
    info = ri_ref[...]
    g0 = info[:, 2:3]
    g1 = info[:, 3:4]
    for c in range(D_MODEL // LANES):
        cols = slice(LANES * c, LANES * (c + 1))
        o_ref[:, cols] = (x2_ref[:, cols] + g0 * buf[slot, 0, _chunk(TC, c), :]
                          + g1 * buf[slot, 1, _chunk(TC, c), :])


def _combine(dest0, dest1, x2, rinfo, ybuf):
    T = x2.shape[0]
    row = lambda i: (i, 0)
    nsteps = T // TC
    idx = pl.BlockSpec((TC,), lambda i: (i,), memory_space=pltpu.SMEM)
    nxt = pl.BlockSpec((TC,), lambda i: (jnp.minimum(i + 1, nsteps - 1),), memory_space=pltpu.SMEM)
    return pl.pallas_call(
        _combine_kernel,
        grid=(nsteps,),
        in_specs=[idx, idx, nxt, nxt,
                  pl.BlockSpec((TC, D_MODEL), row),
                  pl.BlockSpec((TC, LANES), row),
                  pl.BlockSpec(memory_space=pl.ANY)],
        out_specs=pl.BlockSpec((TC, D_MODEL), row),
        out_shape=jax.ShapeDtypeStruct((T, D_MODEL), F32),
        scratch_shapes=[pltpu.VMEM((2, 2, TC * ROW_SUB, LANES), F32),
                        pltpu.SemaphoreType.DMA((2,))],
        compiler_params=_cparams(1),
        name="moe_combine",
    )(dest0, dest1, dest0, dest1, x2, rinfo, ybuf)


def _tri(n, strict):
    return jnp.asarray(np.tril(np.ones((n, n), np.float32), -1 if strict else 0), BF16)


def _rope_tables(S):
    half = ROPE_DIM // 2
    inv = jnp.power(ROPE_THETA, -2.0 * jnp.arange(half, dtype=F32) / ROPE_DIM)
    ang = jnp.arange(S).astype(F32)[:, None] * inv[None, :]
    cos, sin = jnp.cos(ang), jnp.sin(ang)
    d = np.arange(ATT_W) % HEAD_DIM
    idx = d % half
    c = jnp.where(d < ROPE_DIM, cos[:, idx], 1.0)
    s1 = jnp.where(d < half, -sin[:, idx], 0.0)
    s2 = jnp.where((d >= half) & (d < ROPE_DIM), sin[:, idx], 0.0)
    return jnp.stack([c, s1, s2]).astype(F32)


def _fox_consts():
    pq = np.zeros((ATT_HEADS, 3 * LANES, LANES), np.float32)
    pk = np.zeros_like(pq)
    oq = np.zeros((ATT_HEADS, 1, LANES), np.float32)
    ok = np.zeros_like(oq)
    for h in range(ATT_HEADS):
        off = HEAD_DIM if h % 2 == 0 else 0
        for k in range(3):
            pq[h, k * LANES + h, off + k] = 1.0
            oq[h, 0, off + 3 + k] = 1.0
            pk[h, k * LANES + h, off + 3 + k] = -1.0
            ok[h, 0, off + k] = 1.0
    return (_tri(256, False), jnp.asarray(pq, BF16), jnp.asarray(pk, BF16),
            jnp.asarray(oq), jnp.asarray(ok))


def _ssd_consts():
    pexp = np.zeros((3 * LANES, SSD_INNER), np.float32)
    pq = np.zeros((3 * LANES, SSD_HEADS * LANES), np.float32)
    pk = np.zeros_like(pq)
    oq = np.zeros((1, SSD_HEADS * LANES), np.float32)
    ok = np.zeros_like(oq)
    for h in range(SSD_HEADS):
        for k in range(3):
            src = k * LANES + DT_LANE0 + h
            pexp[src, HEAD_DIM * h:HEAD_DIM * (h + 1)] = 1.0
            pq[src, LANES * h + k] = 1.0
            oq[0, LANES * h + 3 + k] = 1.0
            pk[src, LANES * h + 3 + k] = -1.0
            ok[0, LANES * h + k] = 1.0
    return (_tri(SSD_CHUNK, False), jnp.asarray(pexp, BF16), jnp.asarray(pq, BF16),
            jnp.asarray(pk, BF16), jnp.asarray(oq), jnp.asarray(ok))


def _score_tables(S):
    nd = S // TK
    i = np.arange(TQ)[:, None]
    j = np.arange(TK)[None, :]
    causal = np.where(i >= j, 0.0, NEG).astype(np.float32)[None]
    dil = np.zeros((nd, TQ, TK), np.float32)
    for d in range(nd):
        delta = d * TK + i - j
        mult = np.zeros((TQ, TK), np.float64)
        for window, step in DIL_CONFIGS:
            mult += (delta >= 0) & (delta <= window) & (delta % step == 0)
        dil[d] = np.where(mult > 0, np.log(np.maximum(mult, 1.0)), NEG)
    return jnp.asarray(causal), jnp.asarray(dil)


def _group_matrix():
    g = np.arange(ATT_W) // HEAD_DIM
    return jnp.asarray((g[:, None] == g[None, :]).astype(np.float32), BF16)


def _pad_lanes(v, lane0):
    return jnp.zeros((1, LANES), F32).at[0, lane0:lane0 + v.shape[0]].set(v)


def _layer_params(l, w_in, wprep_consts, fox_fgate_b, fox_qn_g, fox_kn_g, dil_qn_g, dil_kn_g,
                  ssd_dt_bias, ssd_A_log, ssd_D, router_wg, router_bg, router_we, router_be):
    w_r = _wprep(w_in, l, *wprep_consts)
    tile4 = lambda g: jnp.tile(g, ATT_HEADS)[None, :]
    qkg = jnp.stack([tile4(fox_qn_g[l]), tile4(fox_kn_g[l]), tile4(dil_qn_g[l]), tile4(dil_kn_g[l])])
    sb = _pad_lanes(fox_fgate_b[l], 0) + _pad_lanes(ssd_dt_bias[l], DT_LANE0)
    arow = _pad_lanes(-jnp.exp(ssd_A_log[l]), DT_LANE0)
    dx = jnp.repeat(ssd_D[l], HEAD_DIM)[None, :]
    wr = jnp.concatenate([router_wg[l], router_we[l],
                          jnp.zeros((D_MODEL, LANES - N_GROUPS - N_EXPERTS), F32)], axis=1)
    wr3 = _split3(wr)
    rb = _pad_lanes(router_bg[l], 0) + _pad_lanes(router_be[l], N_GROUPS)
    return w_r, qkg, sb, arow, dx, wr3[:, 0:2 * LANES], wr3[:, 0:LANES], rb


def kernel(x, mem, norm1_g, w_in, fox_fgate_b, fox_qn_g, fox_kn_g, dil_qn_g, dil_kn_g, ssd_conv_w,
           ssd_conv_b, ssd_dt_bias, ssd_A_log, ssd_D, ssd_norm_g, w_out, norm2_g, mem_norm_g, xa_wq,
           xa_wkv, xa_qn_g, xa_kn_g, xa_wo, norm3_g, router_wg, router_bg, router_we, router_be,
           exp_w1, exp_w3, exp_w2):
    B, S, _ = x.shape
    T = B * S
    depth = w_in.shape[0]
    assert S % TM_IN == 0 and S % TQ == 0 and T % TD == 0 and S >= DIL_CONFIGS[-1][0]

    rope = _rope_tables(S)
    fox_consts = _fox_consts()
    ssd_consts = _ssd_consts()
    causal, dil_tab = _score_tables(S)
    gm = _group_matrix()
    wprep_consts = _wprep_consts()
    tri_mid = _tri(MID_SUB, True)
    nblk = (2 * T) // TM_MOE + N_EXPERTS

    x2d = x.reshape(T, D_MODEL)
    xs = jnp.zeros((nblk * TM_MOE * ROW_SUB, LANES), F32)
    for l in range(depth):
        w_r, qkg, sb, arow, dx, wra, wrb, rb = _layer_params(
            l, w_in, wprep_consts, fox_fgate_b, fox_qn_g, fox_kn_g, dil_qn_g, dil_kn_g,
            ssd_dt_bias, ssd_A_log, ssd_D, router_wg, router_bg, router_we, router_be)

        fq, fkt, fv, dq, dkt, dv, z, xbc, small = _inproj(
            x2d, norm1_g[l][None, :], *w_r, gm, qkg, rope, sb, B, S)
        aug = _fox_scan(small, fox_consts, B, S)
        o_fox = _attention(fq, fkt, fv, causal, aug, B, S)
        o_dil = _attention(dq, dkt, dv, dil_tab, None, B, S)
        o_ssd = _ssd(xbc, z, small, ssd_conv_w[l], ssd_conv_b[l][None, :], arow, dx,
                     ssd_norm_g[l][None, :], ssd_consts, B, S)

        kt, v = _kv(mem, mem_norm_g[l][None, :], xa_wkv[l].astype(BF16), xa_kn_g[l][None, :])
        x2, h3t, rinfo, cnt = _mid(x2d, o_fox, o_dil, o_ssd, w_out[l].astype(BF16),
                                   norm2_g[l][None, :], xa_wq[l].astype(BF16), xa_qn_g[l][None, :],
                                   kt, v, xa_wo[l].astype(BF16), norm3_g[l][None, :], wra, wrb, rb,
                                   tri_mid, S)

        counts = cnt[0, :N_EXPERTS].astype(jnp.int32)
        padded = (counts + TM_MOE - 1) // TM_MOE * TM_MOE
        ends = jnp.cumsum(padded)
        dest = _dest(rinfo, (ends - padded).astype(F32)[:, None])
        dest0, dest1 = dest[0], dest[1]
        n_used = (ends[-1:] // TM_MOE).astype(jnp.int32)
        blk_start = jnp.arange(nblk, dtype=jnp.int32) * TM_MOE
        blk_expert = jnp.minimum(jnp.sum(ends[None, :] <= blk_start[:, None], axis=1),
                                 N_EXPERTS - 1).astype(jnp.int32)

        xs = _dispatch(dest0, dest1, h3t, xs)
        ybuf = _experts(blk_expert, n_used, xs, exp_w1, exp_w3, exp_w2, l)
        x2d = _combine(dest0, dest1, x2, rinfo, ybuf)
    return x2d.reshape(B, S, D_MODEL)
```

```python
import functools

import jax
import jax.numpy as jnp
import numpy as np
from jax import lax
from jax.experimental import pallas as pl
from jax.experimental.pallas import tpu as pltpu

F32 = jnp.float32
BF16 = jnp.bfloat16

D_MODEL = 1024
HEAD_DIM = 64
ATT_HEADS = 4
ATT_W = ATT_HEADS * HEAD_DIM
SSD_HEADS = 8
SSD_INNER = 512
SSD_STATE = 128
SSD_CONV = 4
SSD_CHUNK = 128
SSD_CONV_CH = 1024
XA_HEADS = 4
XA_HEAD_DIM = 256
MEM_LEN = 256
N_GROUPS = 4
EXPERTS_PER_GROUP = 8
N_EXPERTS = 32
EXPERT_FF = 512
DIL_CONFIGS = ((128, 1), (512, 4), (2048, 16))
ROPE_THETA = 500000.0
ROPE_DIM = 16
EPS = 1e-6
NEG = -1e30

LANES = 128
N_FGATE = 4
DT_LANE0 = 4
MAIN_W = 3 * ATT_W + 3 * ATT_W + SSD_INNER + SSD_CONV_CH
IN_W = MAIN_W + LANES

TM_IN = 512
TQ = 256
TK = 256
TM_MID = 512
TM_MOE = 256
TD = 1024
TC = 1024
VMEM_LIMIT = 48 * 1024 * 1024


def _cparams(n_axes):
    return pltpu.CompilerParams(dimension_semantics=("arbitrary",) * n_axes,
                                vmem_limit_bytes=VMEM_LIMIT)


def _rms(x, g):
    return x * lax.rsqrt(jnp.mean(x * x, axis=-1, keepdims=True) + EPS) * g


def _split3(x):
    hi = x.astype(BF16)
    r = x - hi.astype(F32)
    mid = r.astype(BF16)
    lo = (r - mid.astype(F32)).astype(BF16)
    return jnp.concatenate([hi, mid, lo], axis=1)


def _dot(a, b):
    return jnp.dot(a, b, preferred_element_type=F32)


def _dot_nt(a, b):
    return lax.dot_general(a, b, (((1,), (1,)), ((), ())), preferred_element_type=F32)


def _sum3(c):
    w = c.shape[1] // 3
    return c[:, 0:w] + c[:, w:2 * w] + c[:, 2 * w:3 * w]


FF_COL = 3 * ATT_W
DQ_COL = FF_COL + N_FGATE
DT_COL = DQ_COL + MAIN_W - 3 * ATT_W
IN_SRC_W = DT_COL + 8
WPREP_ROWS = 256


def _wprep_kernel(w_ref, tail_ref, sh_ref, shs_ref, wa_ref, wb_ref, ws_ref):
    wa_ref[...] = w_ref[:, 0:FF_COL].astype(BF16)
    lane = lax.broadcasted_iota(jnp.int32, tail_ref.shape, 1)
    tail = jnp.where(lane < IN_SRC_W - MAIN_W, tail_ref[...], 0.0).astype(BF16)
    nb = (MAIN_W - FF_COL) // LANES
    for n in range(nb):
        lo = FF_COL + LANES * n
        if n + 1 < nb:
            pair = w_ref[:, lo:lo + 2 * LANES].astype(BF16)
        else:
            pair = jnp.concatenate([w_ref[:, lo:lo + LANES].astype(BF16), tail], axis=1)
        wb_ref[:, LANES * n:LANES * (n + 1)] = _dot(pair, sh_ref[...]).astype(BF16)
    small = jnp.concatenate([w_ref[:, FF_COL:FF_COL + LANES].astype(BF16), tail], axis=1)
    ws_ref[...] = _dot(small, shs_ref[...]).astype(BF16)


def _wprep(w_in, layer, sh, shs):
    row = lambda i: (i, 0)
    c2 = lambda i: (0, 0)
    wb_w = MAIN_W - FF_COL
    return pl.pallas_call(
        _wprep_kernel,
        grid=(D_MODEL // WPREP_ROWS,),
        in_specs=[pl.BlockSpec((None, WPREP_ROWS, MAIN_W), lambda i: (layer, i, 0)),
                  pl.BlockSpec((None, WPREP_ROWS, LANES), lambda i: (layer, i, MAIN_W // LANES)),
                  pl.BlockSpec(sh.shape, c2), pl.BlockSpec(shs.shape, c2)],
        out_specs=[pl.BlockSpec((WPREP_ROWS, FF_COL), row),
                   pl.BlockSpec((WPREP_ROWS, wb_w), row),
                   pl.BlockSpec((WPREP_ROWS, LANES), row)],
        out_shape=[jax.ShapeDtypeStruct((D_MODEL, FF_COL), BF16),
                   jax.ShapeDtypeStruct((D_MODEL, wb_w), BF16),
                   jax.ShapeDtypeStruct((D_MODEL, LANES), BF16)],
        compiler_params=_cparams(1),
        name="wprep",
    )(w_in, w_in, sh, shs)


def _wprep_consts():
    sh = np.zeros((2 * LANES, LANES), np.float32)
    shs = np.zeros((2 * LANES, LANES), np.float32)
    off = DQ_COL - FF_COL
    for j in range(LANES):
        sh[j + off, j] = 1.0
    for k in range(N_FGATE):
        shs[k, k] = 1.0
    for k in range(SSD_HEADS):
        shs[LANES + DT_COL - MAIN_W + k, DT_LANE0 + k] = 1.0
    return jnp.asarray(sh, BF16), jnp.asarray(shs, BF16)


def _inproj_kernel(x_ref, g_ref, wa_ref, wb_ref, ws_ref, gm_ref, qkg_ref, rope_ref, sb_ref,
                   fq_ref, fkt_ref, fv_ref, dq_ref, dkt_ref, dv_ref, z_ref, xbc_ref, sm_ref):
    h = _rms(x_ref[...], g_ref[...]).astype(BF16)
    na = wa_ref.shape[1]

    def proj(a, b):
        if b <= na:
            return _dot(h, wa_ref[:, a:b])
        return _dot(h, wb_ref[:, a - na:b - na])

    def head_norm(a, idx):
        ssq = _dot((a * a).astype(BF16), gm_ref[...])
        return a * lax.rsqrt(ssq * (1.0 / HEAD_DIM) + EPS) * qkg_ref[idx]

    def rope(a):
        return (a * rope_ref[0] + pltpu.roll(a, ATT_W - ROPE_DIM // 2, 1) * rope_ref[1]
                + pltpu.roll(a, ROPE_DIM // 2, 1) * rope_ref[2])

    scale = HEAD_DIM ** -0.5
    fq_ref[...] = (head_norm(proj(0, 256), 0) * scale).astype(BF16)
    fkt_ref[0] = head_norm(proj(256, 512), 1).T.astype(BF16)
    fv_ref[...] = proj(512, 768).astype(BF16)
    dq_ref[...] = (rope(head_norm(proj(768, 1024), 2)) * scale).astype(BF16)
    dkt_ref[0] = rope(head_norm(proj(1024, 1280), 3)).T.astype(BF16)
    dv_ref[...] = proj(1280, 1536).astype(BF16)
    z_ref[...] = proj(1536, 2048)
    xbc_ref[...] = proj(2048, MAIN_W)
    v = _dot(h, ws_ref[...]) + sb_ref[...]
    e = jnp.log1p(jnp.exp(-jnp.abs(v)))
    lane = lax.broadcasted_iota(jnp.int32, v.shape, 1)
    sm_ref[...] = jnp.where(lane < N_FGATE, jnp.minimum(v, 0.0) - e, jnp.maximum(v, 0.0) + e)


def _inproj(x2d, g, wa, wb, ws, gm, qkg, rope, sb, B, S):
    T = x2d.shape[0]
    nst = S // TM_IN
    row = lambda i: (i, 0)
    const2 = lambda i: (0, 0)
    tr = lambda i: (i // nst, 0, i % nst)
    out_shape = [
        jax.ShapeDtypeStruct((T, ATT_W), BF16),
        jax.ShapeDtypeStruct((B, ATT_W, S), BF16),
        jax.ShapeDtypeStruct((T, ATT_W), BF16),
        jax.ShapeDtypeStruct((T, ATT_W), BF16),
        jax.ShapeDtypeStruct((B, ATT_W, S), BF16),
        jax.ShapeDtypeStruct((T, ATT_W), BF16),
        jax.ShapeDtypeStruct((T, SSD_INNER), F32),
        jax.ShapeDtypeStruct((T, SSD_CONV_CH), F32),
        jax.ShapeDtypeStruct((T, LANES), F32),
    ]
    att = pl.BlockSpec((TM_IN, ATT_W), row)
    att_t = pl.BlockSpec((1, ATT_W, TM_IN), tr)
    return pl.pallas_call(
        _inproj_kernel,
        grid=(T // TM_IN,),
        in_specs=[
            pl.BlockSpec((TM_IN, D_MODEL), row),
            pl.BlockSpec((1, D_MODEL), const2),
            pl.BlockSpec(wa.shape, const2),
            pl.BlockSpec(wb.shape, const2),
            pl.BlockSpec(ws.shape, const2),
            pl.BlockSpec((ATT_W, ATT_W), const2),
            pl.BlockSpec((4, 1, ATT_W), lambda i: (0, 0, 0)),
            pl.BlockSpec((3, TM_IN, ATT_W), lambda i: (0, i % nst, 0)),
            pl.BlockSpec((1, LANES), const2),
        ],
        out_specs=[att, att_t, att, att, att_t, att,
                   pl.BlockSpec((TM_IN, SSD_INNER), row),
                   pl.BlockSpec((TM_IN, SSD_CONV_CH), row),
                   pl.BlockSpec((TM_IN, LANES), row)],
        out_shape=out_shape,
        compiler_params=_cparams(1),
        name="inproj",
    )(x2d, g, wa, wb, ws, gm, qkg, rope, sb)


def _fox_scan_kernel(sm_ref, tri_ref, pq_ref, pk_ref, oq_ref, ok_ref, augq_ref, augkt_ref):
    S = sm_ref.shape[1]
    blk = tri_ref.shape[0]
    carry = jnp.zeros((1, LANES), F32)
    for b in range(S // blk):
        rows = slice(b * blk, (b + 1) * blk)
        c = _sum3(_dot(tri_ref[...], _split3(sm_ref[0, rows, :]))) + carry
        carry = c[blk - 1:blk, :]
        c3 = _split3(c)
        for h in range(ATT_HEADS):
            cols = slice(LANES * h, LANES * (h + 1))
            augq_ref[0, rows, cols] = (_dot(c3, pq_ref[h]) + oq_ref[h]).astype(BF16)
            augkt_ref[0, cols, rows] = (_dot(c3, pk_ref[h]) + ok_ref[h]).T.astype(BF16)


def _fox_scan(small, consts, B, S):
    tri, pq, pk, oq, ok = consts
    c2 = lambda b: (0, 0)
    c3 = lambda b: (0, 0, 0)
    return pl.pallas_call(
        _fox_scan_kernel,
        grid=(B,),
        in_specs=[
            pl.BlockSpec((1, S, LANES), lambda b: (b, 0, 0)),
            pl.BlockSpec(tri.shape, c2),
            pl.BlockSpec(pq.shape, c3),
            pl.BlockSpec(pk.shape, c3),
            pl.BlockSpec(oq.shape, c3),
            pl.BlockSpec(ok.shape, c3),
        ],
        out_specs=[pl.BlockSpec((1, S, ATT_HEADS * LANES), lambda b: (b, 0, 0)),
                   pl.BlockSpec((1, ATT_HEADS * LANES, S), lambda b: (b, 0, 0))],
        out_shape=[jax.ShapeDtypeStruct((B, S, ATT_HEADS * LANES), BF16),
                   jax.ShapeDtypeStruct((B, ATT_HEADS * LANES, S), BF16)],
        compiler_params=_cparams(1),
        name="fox_scan",
    )(small.reshape(B, S, LANES), tri, pq, pk, oq, ok)


def _attn_kernel(*refs, fox, nk):
    if fox:
        q_ref, kt_ref, v_ref, lm_ref, augq_ref, augkt_ref, o_ref, kt_scr, v_scr = refs
    else:
        q_ref, kt_ref, v_ref, lm_ref, o_ref, kt_scr, v_scr = refs
    qi = pl.program_id(1)

    @pl.when(qi == 0)
    def _prep():
        row = lax.broadcasted_iota(jnp.int32, (LANES, TK), 0)
        lane = lax.broadcasted_iota(jnp.int32, (TK, LANES), 1)
        for h in range(ATT_HEADS):
            p, mem = divmod(h, 2)
            pair = slice(LANES * p, LANES * (p + 1))
            for j in range(nk):
                keys = slice(j * TK, (j + 1) * TK)
                kd = kt_ref[0, pair, keys]
                if fox:
                    other = augkt_ref[0, LANES * h:LANES * (h + 1), keys]
                else:
                    other = jnp.zeros_like(kd)
                kt_scr[h, j] = jnp.where((row >> 6) == mem, kd, other)
                vd = v_ref[keys, pair]
                v_scr[h, j] = jnp.where((lane >> 6) == mem, vd, jnp.ones_like(vd))

    qlane = lax.broadcasted_iota(jnp.int32, (TQ, LANES), 1)
    qas = []
    for h in range(ATT_HEADS):
        p, mem = divmod(h, 2)
        qd = q_ref[:, LANES * p:LANES * (p + 1)]
        if fox:
            other = augq_ref[:, LANES * h:LANES * (h + 1)]
        else:
            other = jnp.zeros_like(qd)
        qas.append(jnp.where((qlane >> 6) == mem, qd, other))

    def step(j, carry, table):
        new = []
        for h in range(ATT_HEADS):
            m, acc = carry[h]
            s = _dot(qas[h], kt_scr[h, j])
            if table is not None:
                s = s + table
            m_new = jnp.maximum(m, jnp.max(s, axis=1, keepdims=True))
            alpha = jnp.exp(m - m_new)
            pr = jnp.exp(s - m_new).astype(BF16)
            new.append((m_new, alpha * acc + _dot(pr, v_scr[h, j])))
        return tuple(new)

    init = tuple((jnp.full((TQ, 1), NEG, F32), jnp.zeros((TQ, LANES), F32))
                 for _ in range(ATT_HEADS))

    def run(n, table):
        def two(p, c):
            return step(2 * p + 1, step(2 * p, c, table(2 * p)), table(2 * p + 1))
        c = lax.fori_loop(0, n // 2, two, init)
        return lax.cond(n % 2 == 1, lambda c: step(n - 1, c, table(n - 1)), lambda c: c, c)

    if fox:
        carry = step(qi, run(qi, lambda j: None), lm_ref[0])
    else:
        carry = run(qi + 1, lambda j: lm_ref[qi - j])
    outs = [acc / pltpu.roll(acc, HEAD_DIM, 1) for _, acc in carry]
    for p in range(ATT_HEADS // 2):
        o_ref[:, LANES * p:LANES * (p + 1)] = jnp.where(
            (qlane >> 6) == 0, outs[2 * p], outs[2 * p + 1]).astype(BF16)


def _attention(q, kt, v, lm, aug, B, S):
    fox = aug is not None
    nq, nk = S // TQ, S // TK
    in_specs = [
        pl.BlockSpec((TQ, ATT_W), lambda b, i: (b * nq + i, 0)),
        pl.BlockSpec((1, ATT_W, S), lambda b, i: (b, 0, 0)),
        pl.BlockSpec((S, ATT_W), lambda b, i: (b, 0)),
        pl.BlockSpec(lm.shape, lambda b, i: (0, 0, 0)),
    ]
    args = [q, kt, v, lm]
    if fox:
        in_specs += [pl.BlockSpec((TQ, ATT_HEADS * LANES), lambda b, i: (b * nq + i, 0)),
                     pl.BlockSpec((1, ATT_HEADS * LANES, S), lambda b, i: (b, 0, 0))]
        args += [aug[0].reshape(B * S, ATT_HEADS * LANES), aug[1]]
    return pl.pallas_call(
        functools.partial(_attn_kernel, fox=fox, nk=nk),
        grid=(B, nq),
        in_specs=in_specs,
        out_specs=pl.BlockSpec((TQ, ATT_W), lambda b, i: (b * nq + i, 0)),
        out_shape=jax.ShapeDtypeStruct((B * S, ATT_W), BF16),
        scratch_shapes=[pltpu.VMEM((ATT_HEADS, nk, LANES, TK), BF16),
                        pltpu.VMEM((ATT_HEADS, nk, TK, LANES), BF16)],
        compiler_params=_cparams(2),
        name="fox_attn" if fox else "dil_attn",
    )(*args)


def _ssd_kernel(xbc_ref, z_ref, sm_ref, cw_ref, cb_ref, arow_ref, dx_ref, ng_ref,
                tri_ref, pexp_ref, pq_ref, pk_ref, oq_ref, ok_ref, o_ref, buf, state):
    Q = SSD_CHUNK

    @pl.when(pl.program_id(1) == 0)
    def _reset():
        buf[0:8, :] = jnp.zeros((8, SSD_CONV_CH), F32)
        state[...] = jnp.zeros(state.shape, F32)

    xb = xbc_ref[...]
    buf[8:8 + Q, :] = xb
    conv = cb_ref[...]
    for k in range(SSD_CONV):
        off = 8 - (SSD_CONV - 1) + k
        conv = conv + cw_ref[k:k + 1, :] * buf[off:off + Q, :]
    buf[0:8, :] = xb[Q - 8:Q, :]
    act = conv * jax.nn.sigmoid(conv)
    xs = act[:, 0:SSD_INNER]
    bm = act[:, SSD_INNER:SSD_INNER + 2 * SSD_STATE]
    cm = act[:, SSD_INNER + 2 * SSD_STATE:]

    dt = sm_ref[...]
    acs = _sum3(_dot(tri_ref[...], _split3(dt * arow_ref[...])))
    acs3 = _split3(acs)
    ax = _dot(acs3, pexp_ref[...])
    dtx = _dot(_split3(dt), pexp_ref[...])
    last = ax[Q - 1:Q, :]
    ea = jnp.exp(ax)
    cdec = jnp.exp(last)
    xc = xs * dtx
    xcb = xc.astype(BF16)
    xcd = (xc * jnp.exp(last - ax)).astype(BF16)
    uq = (_dot(acs3, pq_ref[...]) + oq_ref[...]).astype(BF16)
    uk = (_dot(acs3, pk_ref[...]) + ok_ref[...]).astype(BF16)

    tril = (lax.broadcasted_iota(jnp.int32, (Q, Q), 0) >= lax.broadcasted_iota(jnp.int32, (Q, Q), 1))
    first = lax.broadcasted_iota(jnp.int32, (Q, LANES), 1) < HEAD_DIM
    ys = []
    for g in range(2):
        gs = slice(SSD_STATE * g, SSD_STATE * (g + 1))
        bg = bm[:, gs]
        cg = cm[:, gs].astype(BF16)
        cbm = _dot_nt(cg, bg.astype(BF16))
        bgt = bg.T.astype(BF16)
        for pp in range(2):
            p = 2 * g + pp
            ps = slice(LANES * p, LANES * (p + 1))
            ms = []
            for mem in range(2):
                hs = slice(LANES * (2 * p + mem), LANES * (2 * p + mem + 1))
                dm = _dot_nt(uq[:, hs], uk[:, hs])
                ms.append((cbm * jnp.exp(jnp.where(tril, dm, NEG))).astype(BF16))
            xp = xcb[:, ps]
            zero = jnp.zeros_like(xp)
            xcat = jnp.concatenate([jnp.where(first, xp, zero), jnp.where(first, zero, xp)], axis=0)
            y_diag = _dot(jnp.concatenate(ms, axis=1), xcat)
            st = state[p]
            y_off = _dot(cg, st.astype(BF16)) * ea[:, ps]
            state[p] = cdec[:, ps] * st + _dot(bgt, xcd[:, ps])
            ys.append(y_diag + y_off + xs[:, ps] * dx_ref[:, ps])
    y = jnp.concatenate(ys, axis=1)
    zz = z_ref[...]
    o_ref[...] = _rms(y * (zz * jax.nn.sigmoid(zz)), ng_ref[...]).astype(BF16)


def _ssd(xbc, z, small, cw, cb, arow, dx, ng, consts, B, S):
    nc = S // SSD_CHUNK
    row = lambda b, c: (b * nc + c, 0)
    c2 = lambda b, c: (0, 0)
    full = lambda a: pl.BlockSpec(a.shape, c2)
    return pl.pallas_call(
        _ssd_kernel,
        grid=(B, nc),
        in_specs=[pl.BlockSpec((SSD_CHUNK, SSD_CONV_CH), row),
                  pl.BlockSpec((SSD_CHUNK, SSD_INNER), row),
                  pl.BlockSpec((SSD_CHUNK, LANES), row),
                  full(cw), full(cb), full(arow), full(dx), full(ng)] + [full(a) for a in consts],
        out_specs=pl.BlockSpec((SSD_CHUNK, SSD_INNER), row),
        out_shape=jax.ShapeDtypeStruct((B * S, SSD_INNER), BF16),
        scratch_shapes=[pltpu.VMEM((8 + SSD_CHUNK, SSD_CONV_CH), F32),
                        pltpu.VMEM((SSD_HEADS // 2, SSD_STATE, LANES), F32)],
        compiler_params=_cparams(2),
        name="ssd",
    )(xbc, z, small, cw, cb, arow, dx, ng, *consts)


def _kv_kernel(mem_ref, g_ref, w_ref, kg_ref, kt_ref, v_ref):
    m = _rms(mem_ref[0], g_ref[...]).astype(BF16)
    kv = _dot(m, w_ref[...])
    for h in range(XA_HEADS):
        hs = slice(XA_HEAD_DIM * h, XA_HEAD_DIM * (h + 1))
        kt_ref[0, hs, :] = _rms(kv[:, hs], kg_ref[...]).T.astype(BF16)
    v_ref[0] = kv[:, D_MODEL:].astype(BF16)


def _kv(mem, g, w, kg):
    B = mem.shape[0]
    c2 = lambda b: (0, 0)
    return pl.pallas_call(
        _kv_kernel,
        grid=(B,),
        in_specs=[pl.BlockSpec((1, MEM_LEN, D_MODEL), lambda b: (b, 0, 0)),
                  pl.BlockSpec((1, D_MODEL), c2),
                  pl.BlockSpec((D_MODEL, 2 * D_MODEL), c2),
                  pl.BlockSpec((1, XA_HEAD_DIM), c2)],
        out_specs=[pl.BlockSpec((1, D_MODEL, MEM_LEN), lambda b: (b, 0, 0)),
                   pl.BlockSpec((1, MEM_LEN, D_MODEL), lambda b: (b, 0, 0))],
        out_shape=[jax.ShapeDtypeStruct((B, D_MODEL, MEM_LEN), BF16),
                   jax.ShapeDtypeStruct((B, MEM_LEN, D_MODEL), BF16)],
        compiler_params=_cparams(1),
        name="mem_kv",
    )(mem, g, w, kg)


ROW_SUB = D_MODEL // LANES


def _chunk(n, c):
    return pl.ds(c, n, stride=ROW_SUB)


def _to_row_tiles(ref, x):
    n = x.shape[0]
    for c in range(ROW_SUB):
        ref[_chunk(n, c), :] = x[:, LANES * c:LANES * (c + 1)]


def _from_row_tiles(ref):
    n = ref.shape[0] // ROW_SUB
    return jnp.concatenate([ref[_chunk(n, c), :] for c in range(ROW_SUB)], axis=1)


def _row_copy(src, dst, sem, src_row, dst_row):
    return pltpu.make_async_copy(
        src.at[pl.ds(pl.multiple_of(src_row * ROW_SUB, ROW_SUB), ROW_SUB)],
        dst.at[pl.ds(pl.multiple_of(dst_row * ROW_SUB, ROW_SUB), ROW_SUB)], sem)


def _mid_kernel(x_ref, of_ref, od_ref, os_ref, wo_ref, g2_ref, wq_ref, qg_ref, kt_ref, v_ref,
                wxo_ref, g3_ref, wra_ref, wrb_ref, rb_ref, tri_ref,
                x2_ref, h3_ref, ri_ref, cnt_ref, run):
    @pl.when(pl.program_id(0) == 0)
    def _reset():
        run[...] = jnp.zeros(run.shape, F32)

    x1 = (x_ref[...] + _dot(of_ref[...], wo_ref[0:ATT_W, :])
          + _dot(od_ref[...], wo_ref[ATT_W:2 * ATT_W, :])
          + _dot(os_ref[...], wo_ref[2 * ATT_W:, :]))

    q = _dot(_rms(x1, g2_ref[...]).astype(BF16), wq_ref[...])
    heads = []
    for h in range(XA_HEADS):
        hs = slice(XA_HEAD_DIM * h, XA_HEAD_DIM * (h + 1))
        qn = (_rms(q[:, hs], qg_ref[...]) * XA_HEAD_DIM ** -0.5).astype(BF16)
        s = _dot(qn, kt_ref[0, hs, :])
        e = jnp.exp(s - jnp.max(s, axis=1, keepdims=True))
        o = _dot(e.astype(BF16), v_ref[0, :, hs]) / jnp.sum(e, axis=1, keepdims=True)
        heads.append(o.astype(BF16))
    x2 = x1 + _dot(jnp.concatenate(heads, axis=1), wxo_ref[...])
    x2_ref[...] = x2

    h3 = _rms(x2, g3_ref[...])
    _to_row_tiles(h3_ref, h3)

    hi = h3.astype(BF16)
    mid = (h3 - hi.astype(F32)).astype(BF16)
    both = _dot(hi, wra_ref[...])
    logits = _dot(mid, wrb_ref[...]) + both[:, LANES:] + both[:, 0:LANES] + rb_ref[...]

    lane = lax.broadcasted_iota(jnp.int32, logits.shape, 1)
    lanef = lane.astype(F32)
    big = float(LANES)

    def first_max(vals):
        top = jnp.max(vals, axis=1, keepdims=True)
        return top, jnp.min(jnp.where(vals == top, lanef, big), axis=1, keepdims=True)

    gl = jnp.where(lane < N_GROUPS, logits, NEG)
    gmax, gsel = first_max(gl)
    ggate = 1.0 / jnp.sum(jnp.exp(gl - gmax), axis=1, keepdims=True)
    grp = ((lane - N_GROUPS) >> 3).astype(F32)
    el = jnp.where(grp == gsel, logits, NEG)
    v1, i1 = first_max(el)
    v2, i2 = first_max(jnp.where(lanef == i1, NEG, el))
    t = jnp.exp(v2 - v1)
    p1 = 1.0 / (1.0 + t)
    e1 = i1 - N_GROUPS
    e2 = i2 - N_GROUPS

    hit1 = lanef == e1
    hit2 = lanef == e2
    onehot = jnp.where(hit1 | hit2, 1.0, 0.0)
    before = _dot(tri_ref[...], onehot.astype(BF16)) + run[...]
    r1 = jnp.sum(jnp.where(hit1, before, 0.0), axis=1, keepdims=True)
    r2 = jnp.sum(jnp.where(hit2, before, 0.0), axis=1, keepdims=True)
    run[...] = run[...] + jnp.sum(onehot, axis=0, keepdims=True)

    cols = (e1, e2, p1 * ggate, t * p1 * ggate, r1, r2)
    info = jnp.zeros(logits.shape, F32)
    for k, col in enumerate(cols):
        info = jnp.where(lane == k, col, info)
    ri_ref[...] = info
    cnt_ref[...] = jnp.broadcast_to(run[...], cnt_ref.shape)


def _mid(x2d, o_fox, o_dil, o_ssd, wo, g2, wq, qg, kt, v, wxo, g3, wra, wrb, rb, tri, S):
    T = x2d.shape[0]
    npb = S // TM_MID
    row = lambda i: (i, 0)
    c2 = lambda i: (0, 0)
    return pl.pallas_call(
        _mid_kernel,
        grid=(T // TM_MID,),
        in_specs=[pl.BlockSpec((TM_MID, D_MODEL), row),
                  pl.BlockSpec((TM_MID, ATT_W), row),
                  pl.BlockSpec((TM_MID, ATT_W), row),
                  pl.BlockSpec((TM_MID, SSD_INNER), row),
                  pl.BlockSpec((D_MODEL, D_MODEL), c2),
                  pl.BlockSpec((1, D_MODEL), c2),
                  pl.BlockSpec((D_MODEL, D_MODEL), c2),
                  pl.BlockSpec((1, XA_HEAD_DIM), c2),
                  pl.BlockSpec((1, D_MODEL, MEM_LEN), lambda i: (i // npb, 0, 0)),
                  pl.BlockSpec((1, MEM_LEN, D_MODEL), lambda i: (i // npb, 0, 0)),
                  pl.BlockSpec((D_MODEL, D_MODEL), c2),
                  pl.BlockSpec((1, D_MODEL), c2),
                  pl.BlockSpec((D_MODEL, 2 * LANES), c2),
                  pl.BlockSpec((D_MODEL, LANES), c2),
                  pl.BlockSpec((1, LANES), c2),
                  pl.BlockSpec((TM_MID, TM_MID), c2)],
        out_specs=[pl.BlockSpec((TM_MID, D_MODEL), row),
                   pl.BlockSpec((TM_MID * ROW_SUB, LANES), row),
                   pl.BlockSpec((TM_MID, LANES), row),
                   pl.BlockSpec((8, LANES), c2)],
        out_shape=[jax.ShapeDtypeStruct((T, D_MODEL), F32),
                   jax.ShapeDtypeStruct((T * ROW_SUB, LANES), F32),
                   jax.ShapeDtypeStruct((T, LANES), F32),
                   jax.ShapeDtypeStruct((8, LANES), F32)],
        scratch_shapes=[pltpu.VMEM((1, LANES), F32)],
        compiler_params=_cparams(1),
        name="mid",
    )(x2d, o_fox, o_dil, o_ssd, wo, g2, wq, qg, kt, v, wxo, g3, wra, wrb, rb, tri)


def _dest_kernel(ri_ref, st_ref, o_ref):
    info = ri_ref[...].T
    n = info.shape[1]
    expert = lax.broadcasted_iota(jnp.int32, (N_EXPERTS, n), 0).astype(F32)
    row = lax.broadcasted_iota(jnp.int32, (8, n), 0)
    out = jnp.zeros((8, n), F32)
    for k in range(2):
        start = jnp.sum(jnp.where(expert == info[k:k + 1, :], st_ref[...], 0.0), axis=0, keepdims=True)
        out = jnp.where(row == k, start + info[4 + k:5 + k, :], out)
    o_ref[...] = out.astype(jnp.int32)


def _dest(rinfo, starts):
    T = rinfo.shape[0]
    return pl.pallas_call(
        _dest_kernel,
        grid=(T // TD,),
        in_specs=[pl.BlockSpec((TD, LANES), lambda i: (i, 0)),
                  pl.BlockSpec((N_EXPERTS, 1), lambda i: (0, 0))],
        out_specs=pl.BlockSpec((8, TD), lambda i: (0, i)),
        out_shape=jax.ShapeDtypeStruct((8, T), jnp.int32),
        compiler_params=_cparams(1),
        name="moe_dest",
    )(rinfo, starts)


def _dispatch_kernel(d0_ref, d1_ref, h_ref, xs_in, xs_out, sem):
    del xs_in

    def issue(r, carry):
        _row_copy(h_ref, xs_out, sem, r, d0_ref[r]).start(priority=0)
        _row_copy(h_ref, xs_out, sem, r, d1_ref[r]).start(priority=1)
        return carry

    def drain(r, carry):
        _row_copy(h_ref, xs_out, sem, 0, 0).wait()
        _row_copy(h_ref, xs_out, sem, 0, 0).wait()
        return carry

    lax.fori_loop(0, TD, issue, 0, unroll=8)
    lax.fori_loop(0, TD, drain, 0, unroll=8)


def _dispatch(dest0, dest1, h3t, xs_init):
    T = h3t.shape[0] // ROW_SUB
    idx = pl.BlockSpec((TD,), lambda i: (i,), memory_space=pltpu.SMEM)
    return pl.pallas_call(
        _dispatch_kernel,
        grid=(T // TD,),
        in_specs=[idx, idx,
                  pl.BlockSpec((TD * ROW_SUB, LANES), lambda i: (i, 0)),
                  pl.BlockSpec(memory_space=pl.ANY)],
        out_specs=pl.BlockSpec(memory_space=pl.ANY),
        out_shape=jax.ShapeDtypeStruct(xs_init.shape, F32),
        scratch_shapes=[pltpu.SemaphoreType.DMA],
        input_output_aliases={3: 0},
        compiler_params=_cparams(1),
        name="moe_dispatch",
    )(dest0, dest1, h3t, xs_init)


def _expert_kernel(be_ref, nu_ref, xs_ref, w1_ref, w3_ref, w2_ref, y_ref, w1s, w3s, w2s):
    i = pl.program_id(0)
    fresh = (i == 0) | (be_ref[i] != be_ref[jnp.maximum(i - 1, 0)])

    @pl.when(fresh)
    def _cast():
        w1s[...] = w1_ref[...].astype(BF16)
        w3s[...] = w3_ref[...].astype(BF16)
        w2s[...] = w2_ref[...].astype(BF16)

    @pl.when(i < nu_ref[0])
    def _run():
        xb = _from_row_tiles(xs_ref).astype(BF16)
        a = _dot(xb, w1s[...])
        b = _dot(xb, w3s[...])
        _to_row_tiles(y_ref, _dot((a * jax.nn.sigmoid(a) * b).astype(BF16), w2s[...]))

    @pl.when(i >= nu_ref[0])
    def _skip():
        y_ref[...] = jnp.zeros(y_ref.shape, F32)


def _experts(blk_expert, n_used, xs, w1, w3, w2, layer):
    nblk = xs.shape[0] // (TM_MOE * ROW_SUB)
    wmap = lambda i, be, nu: (layer, be[i], 0, 0)
    return pl.pallas_call(
        _expert_kernel,
        grid_spec=pltpu.PrefetchScalarGridSpec(
            num_scalar_prefetch=2,
            grid=(nblk,),
            in_specs=[pl.BlockSpec((TM_MOE * ROW_SUB, LANES),
                                   lambda i, be, nu: (jnp.minimum(i, nu[0] - 1), 0)),
                      pl.BlockSpec((None, None, D_MODEL, EXPERT_FF), wmap),
                      pl.BlockSpec((None, None, D_MODEL, EXPERT_FF), wmap),
                      pl.BlockSpec((None, None, EXPERT_FF, D_MODEL), wmap)],
            out_specs=pl.BlockSpec((TM_MOE * ROW_SUB, LANES), lambda i, be, nu: (i, 0)),
            scratch_shapes=[pltpu.VMEM((D_MODEL, EXPERT_FF), BF16),
                            pltpu.VMEM((D_MODEL, EXPERT_FF), BF16),
                            pltpu.VMEM((EXPERT_FF, D_MODEL), BF16)]),
        out_shape=jax.ShapeDtypeStruct(xs.shape, F32),
        compiler_params=_cparams(1),
        name="moe_experts",
    )(blk_expert, n_used, xs, w1, w3, w2)


def _combine_kernel(d0_ref, d1_ref, d0n_ref, d1n_ref, x2_ref, ri_ref, y_hbm, o_ref, buf, sem):
    i = pl.program_id(0)
    slot = i % 2

    def gather(d0, d1, s):
        def issue(r, carry):
            _row_copy(y_hbm, buf.at[s, 0], sem.at[s], d0[r], r).start(priority=0)
            _row_copy(y_hbm, buf.at[s, 1], sem.at[s], d1[r], r).start(priority=1)
            return carry
        lax.fori_loop(0, TC, issue, 0, unroll=8)

    @pl.when(i == 0)
    def _first():
        gather(d0_ref, d1_ref, 0)

    @pl.when(i + 1 < pl.num_programs(0))
    def _next():
        gather(d0n_ref, d1n_ref, 1 - slot)

    def drain(r, carry):
        _row_copy(y_hbm, buf.at[slot, 0], sem.at[slot], 0, 0).wait()
        _row_copy(y_hbm, buf.at[slot, 1], sem.at[slot], 0, 0).wait()
        return carry

    lax.fori_loop(0, TC, drain, 0, unroll=8)
    info = ri_ref[...]
    g0 = info[:, 2:3]
    g1 = info[:, 3:4]
    for c in range(D_MODEL // LANES):
        cols = slice(LANES * c, LANES * (c + 1))
        o_ref[:, cols] = (x2_ref[:, cols] + g0 * buf[slot, 0, _chunk(TC, c), :]
                          + g1 * buf[slot, 1, _chunk(TC, c), :])


def _combine(dest0, dest1, x2, rinfo, ybuf):
    T = x2.shape[0]
    row = lambda i: (i, 0)
    nsteps = T // TC
    idx = pl.BlockSpec((TC,), lambda i: (i,), memory_space=pltpu.SMEM)
    nxt = pl.BlockSpec((TC,), lambda i: (jnp.minimum(i + 1, nsteps - 1),), memory_space=pltpu.SMEM)
    return pl.pallas_call(
        _combine_kernel,
        grid=(nsteps,),
        in_specs=[idx, idx, nxt, nxt,
                  pl.BlockSpec((TC, D_MODEL), row),
                  pl.BlockSpec((TC, LANES), row),
                  pl.BlockSpec(memory_space=pl.ANY)],
        out_specs=pl.BlockSpec((TC, D_MODEL), row),
        out_shape=jax.ShapeDtypeStruct((T, D_MODEL), F32),
        scratch_shapes=[pltpu.VMEM((2, 2, TC * ROW_SUB, LANES), F32),
                        pltpu.SemaphoreType.DMA((2,))],
        compiler_params=_cparams(1),
        name="moe_combine",
    )(dest0, dest1, dest0, dest1, x2, rinfo, ybuf)


def _tri(n, strict):
    return jnp.asarray(np.tril(np.ones((n, n), np.float32), -1 if strict else 0), BF16)


def _rope_tables(S):
    half = ROPE_DIM // 2
    inv = jnp.power(ROPE_THETA, -2.0 * jnp.arange(half, dtype=F32) / ROPE_DIM)
    ang = jnp.arange(S).astype(F32)[:, None] * inv[None, :]
    cos, sin = jnp.cos(ang), jnp.sin(ang)
    d = np.arange(ATT_W) % HEAD_DIM
    idx = d % half
    c = jnp.where(d < ROPE_DIM, cos[:, idx], 1.0)
    s1 = jnp.where(d < half, -sin[:, idx], 0.0)
    s2 = jnp.where((d >= half) & (d < ROPE_DIM), sin[:, idx], 0.0)
    return jnp.stack([c, s1, s2]).astype(F32)


def _fox_consts():
    pq = np.zeros((ATT_HEADS, 3 * LANES, LANES), np.float32)
    pk = np.zeros_like(pq)
    oq = np.zeros((ATT_HEADS, 1, LANES), np.float32)
    ok = np.zeros_like(oq)
    for h in range(ATT_HEADS):
        off = HEAD_DIM if h % 2 == 0 else 0
        for k in range(3):
            pq[h, k * LANES + h, off + k] = 1.0
            oq[h, 0, off + 3 + k] = 1.0
            pk[h, k * LANES + h, off + 3 + k] = -1.0
            ok[h, 0, off + k] = 1.0
    return (_tri(256, False), jnp.asarray(pq, BF16), jnp.asarray(pk, BF16),
            jnp.asarray(oq), jnp.asarray(ok))


def _ssd_consts():
    pexp = np.zeros((3 * LANES, SSD_INNER), np.float32)
    pq = np.zeros((3 * LANES, SSD_HEADS * LANES), np.float32)
    pk = np.zeros_like(pq)
    oq = np.zeros((1, SSD_HEADS * LANES), np.float32)
    ok = np.zeros_like(oq)
    for h in range(SSD_HEADS):
        for k in range(3):
            src = k * LANES + DT_LANE0 + h
            pexp[src, HEAD_DIM * h:HEAD_DIM * (h + 1)] = 1.0
            pq[src, LANES * h + k] = 1.0
            oq[0, LANES * h + 3 + k] = 1.0
            pk[src, LANES * h + 3 + k] = -1.0
            ok[0, LANES * h + k] = 1.0
    return (_tri(SSD_CHUNK, False), jnp.asarray(pexp, BF16), jnp.asarray(pq, BF16),
            jnp.asarray(pk, BF16), jnp.asarray(oq), jnp.asarray(ok))


def _score_tables(S):
    nd = S // TK
    i = np.arange(TQ)[:, None]
    j = np.arange(TK)[None, :]
    causal = np.where(i >= j, 0.0, NEG).astype(np.float32)[None]
    dil = np.zeros((nd, TQ, TK), np.float32)
    for d in range(nd):
        delta = d * TK + i - j
        mult = np.zeros((TQ, TK), np.float64)
        for window, step in DIL_CONFIGS:
            mult += (delta >= 0) & (delta <= window) & (delta % step == 0)
        dil[d] = np.where(mult > 0, np.log(np.maximum(mult, 1.0)), NEG)
    return jnp.asarray(causal), jnp.asarray(dil)


def _group_matrix():
    g = np.arange(ATT_W) // HEAD_DIM
    return jnp.asarray((g[:, None] == g[None, :]).astype(np.float32), BF16)


def _pad_lanes(v, lane0):
    return jnp.zeros((1, LANES), F32).at[0, lane0:lane0 + v.shape[0]].set(v)


def _layer_params(l, w_in, wprep_consts, fox_fgate_b, fox_qn_g, fox_kn_g, dil_qn_g, dil_kn_g,
                  ssd_dt_bias, ssd_A_log, ssd_D, router_wg, router_bg, router_we, router_be):
    w_r = _wprep(w_in, l, *wprep_consts)
    tile4 = lambda g: jnp.tile(g, ATT_HEADS)[None, :]
    qkg = jnp.stack([tile4(fox_qn_g[l]), tile4(fox_kn_g[l]), tile4(dil_qn_g[l]), tile4(dil_kn_g[l])])
    sb = _pad_lanes(fox_fgate_b[l], 0) + _pad_lanes(ssd_dt_bias[l], DT_LANE0)
    arow = _pad_lanes(-jnp.exp(ssd_A_log[l]), DT_LANE0)
    dx = jnp.repeat(ssd_D[l], HEAD_DIM)[None, :]
    wr = jnp.concatenate([router_wg[l], router_we[l],
                          jnp.zeros((D_MODEL, LANES - N_GROUPS - N_EXPERTS), F32)], axis=1)
    wr3 = _split3(wr)
    rb = _pad_lanes(router_bg[l], 0) + _pad_lanes(router_be[l], N_GROUPS)
    return w_r, qkg, sb, arow, dx, wr3[:, 0:2 * LANES], wr3[:, 0:LANES], rb


def kernel(x, mem, norm1_g, w_in, fox_fgate_b, fox_qn_g, fox_kn_g, dil_qn_g, dil_kn_g, ssd_conv_w,
           ssd_conv_b, ssd_dt_bias, ssd_A_log, ssd_D, ssd_norm_g, w_out, norm2_g, mem_norm_g, xa_wq,
           xa_wkv, xa_qn_g, xa_kn_g, xa_wo, norm3_g, router_wg, router_bg, router_we, router_be,
           exp_w1, exp_w3, exp_w2):
    B, S, _ = x.shape
    T = B * S
    depth = w_in.shape[0]
    assert S % TM_IN == 0 and S % TQ == 0 and T % TD == 0 and S >= DIL_CONFIGS[-1][0]

    rope = _rope_tables(S)
    fox_consts = _fox_consts()
    ssd_consts = _ssd_consts()
    causal, dil_tab = _score_tables(S)
    gm = _group_matrix()
    wprep_consts = _wprep_consts()
    tri_mid = _tri(TM_MID, True)
    nblk = (2 * T) // TM_MOE + N_EXPERTS

    x2d = x.reshape(T, D_MODEL)
    xs = jnp.zeros((nblk * TM_MOE * ROW_SUB, LANES), F32)
    for l in range(depth):
        w_r, qkg, sb, arow, dx, wra, wrb, rb = _layer_params(
            l, w_in, wprep_consts, fox_fgate_b, fox_qn_g, fox_kn_g, dil_qn_g, dil_kn_g,
            ssd_dt_bias, ssd_A_log, ssd_D, router_wg, router_bg, router_we, router_be)

        fq, fkt, fv, dq, dkt, dv, z, xbc, small = _inproj(
            x2d, norm1_g[l][None, :], *w_r, gm, qkg, rope, sb, B, S)
        aug = _fox_scan(small, fox_consts, B, S)
        o_fox = _attention(fq, fkt, fv, causal, aug, B, S)
        o_dil = _attention(dq, dkt, dv, dil_tab, None, B, S)
        o_ssd = _ssd(xbc, z, small, ssd_conv_w[l], ssd_conv_b[l][None, :], arow, dx,
                     ssd_norm_g[l][None, :], ssd_consts, B, S)

        kt, v = _kv(mem, mem_norm_g[l][None, :], xa_wkv[l].astype(BF16), xa_kn_g[l][None, :])
        x2, h3t, rinfo, cnt = _mid(x2d, o_fox, o_dil, o_ssd, w_out[l].astype(BF16),
                                   norm2_g[l][None, :], xa_wq[l].astype(BF16), xa_qn_g[l][None, :],
                                   kt, v, xa_wo[l].astype(BF16), norm3_g[l][None, :], wra, wrb, rb,
                                   tri_mid, S)

        counts = cnt[0, :N_EXPERTS].astype(jnp.int32)
        padded = (counts + TM_MOE - 1) // TM_MOE * TM_MOE
        ends = jnp.cumsum(padded)
        dest = _dest(rinfo, (ends - padded).astype(F32)[:, None])
        dest0, dest1 = dest[0], dest[1]
        n_used = (ends[-1:] // TM_MOE).astype(jnp.int32)
        blk_start = jnp.arange(nblk, dtype=jnp.int32) * TM_MOE
        blk_expert = jnp.minimum(jnp.sum(ends[None, :] <= blk_start[:, None], axis=1),
                                 N_EXPERTS - 1).astype(jnp.int32)

        xs = _dispatch(dest0, dest1, h3t, xs)
        ybuf = _experts(blk_expert, n_used, xs, exp_w1, exp_w3, exp_w2, l)
        x2d = _combine(dest0, dest1, x2, rinfo, ybuf)
    return x2d.reshape(B, S, D_MODEL)
```

```python
import functools

import jax
import jax.numpy as jnp
import numpy as np
from jax import lax
from jax.experimental import pallas as pl
from jax.experimental.pallas import tpu as pltpu

F32 = jnp.float32
BF16 = jnp.bfloat16

D_MODEL = 1024
HEAD_DIM = 64
ATT_HEADS = 4
ATT_W = ATT_HEADS * HEAD_DIM
SSD_HEADS = 8
SSD_INNER = 512
SSD_STATE = 128
SSD_CONV = 4
SSD_CHUNK = 128
SSD_CONV_CH = 1024
XA_HEADS = 4
XA_HEAD_DIM = 256
MEM_LEN = 256
N_GROUPS = 4
EXPERTS_PER_GROUP = 8
N_EXPERTS = 32
EXPERT_FF = 512
DIL_CONFIGS = ((128, 1), (512, 4), (2048, 16))
ROPE_THETA = 500000.0
ROPE_DIM = 16
EPS = 1e-6
NEG = -1e30

LANES = 128
N_FGATE = 4
DT_LANE0 = 4
MAIN_W = 3 * ATT_W + 3 * ATT_W + SSD_INNER + SSD_CONV_CH
IN_W = MAIN_W + LANES

TM_IN = 512
TQ = 256
TK = 256
TM_MID = 512
TM_MOE = 256
SSD_ROWS = 256
TD = 1024
TC = 1024
VMEM_LIMIT = 48 * 1024 * 1024


def _cparams(n_axes):
    return pltpu.CompilerParams(dimension_semantics=("arbitrary",) * n_axes,
                                vmem_limit_bytes=VMEM_LIMIT)


def _rms(x, g):
    return x * lax.rsqrt(jnp.mean(x * x, axis=-1, keepdims=True) + EPS) * g


def _split3(x):
    hi = x.astype(BF16)
    r = x - hi.astype(F32)
    mid = r.astype(BF16)
    lo = (r - mid.astype(F32)).astype(BF16)
    return jnp.concatenate([hi, mid, lo], axis=1)


def _dot(a, b):
    return jnp.dot(a, b, preferred_element_type=F32)


def _dot_nt(a, b):
    return lax.dot_general(a, b, (((1,), (1,)), ((), ())), preferred_element_type=F32)


def _sum3(c):
    w = c.shape[1] // 3
    return c[:, 0:w] + c[:, w:2 * w] + c[:, 2 * w:3 * w]


FF_COL = 3 * ATT_W
DQ_COL = FF_COL + N_FGATE
DT_COL = DQ_COL + MAIN_W - 3 * ATT_W
IN_SRC_W = DT_COL + 8
WPREP_ROWS = 256


def _wprep_kernel(w_ref, tail_ref, sh_ref, shs_ref, wa_ref, wb_ref, ws_ref):
    wa_ref[...] = w_ref[:, 0:FF_COL].astype(BF16)
    lane = lax.broadcasted_iota(jnp.int32, tail_ref.shape, 1)
    tail = jnp.where(lane < IN_SRC_W - MAIN_W, tail_ref[...], 0.0).astype(BF16)
    nb = (MAIN_W - FF_COL) // LANES
    for n in range(nb):
        lo = FF_COL + LANES * n
        if n + 1 < nb:
            pair = w_ref[:, lo:lo + 2 * LANES].astype(BF16)
        else:
            pair = jnp.concatenate([w_ref[:, lo:lo + LANES].astype(BF16), tail], axis=1)
        wb_ref[:, LANES * n:LANES * (n + 1)] = _dot(pair, sh_ref[...]).astype(BF16)
    small = jnp.concatenate([w_ref[:, FF_COL:FF_COL + LANES].astype(BF16), tail], axis=1)
    ws_ref[...] = _dot(small, shs_ref[...]).astype(BF16)


def _wprep(w_in, layer, sh, shs):
    row = lambda i: (i, 0)
    c2 = lambda i: (0, 0)
    wb_w = MAIN_W - FF_COL
    return pl.pallas_call(
        _wprep_kernel,
        grid=(D_MODEL // WPREP_ROWS,),
        in_specs=[pl.BlockSpec((None, WPREP_ROWS, MAIN_W), lambda i: (layer, i, 0)),
                  pl.BlockSpec((None, WPREP_ROWS, LANES), lambda i: (layer, i, MAIN_W // LANES)),
                  pl.BlockSpec(sh.shape, c2), pl.BlockSpec(shs.shape, c2)],
        out_specs=[pl.BlockSpec((WPREP_ROWS, FF_COL), row),
                   pl.BlockSpec((WPREP_ROWS, wb_w), row),
                   pl.BlockSpec((WPREP_ROWS, LANES), row)],
        out_shape=[jax.ShapeDtypeStruct((D_MODEL, FF_COL), BF16),
                   jax.ShapeDtypeStruct((D_MODEL, wb_w), BF16),
                   jax.ShapeDtypeStruct((D_MODEL, LANES), BF16)],
        compiler_params=_cparams(1),
        name="wprep",
    )(w_in, w_in, sh, shs)


def _wprep_consts():
    sh = np.zeros((2 * LANES, LANES), np.float32)
    shs = np.zeros((2 * LANES, LANES), np.float32)
    off = DQ_COL - FF_COL
    for j in range(LANES):
        sh[j + off, j] = 1.0
    for k in range(N_FGATE):
        shs[k, k] = 1.0
    for k in range(SSD_HEADS):
        shs[LANES + DT_COL - MAIN_W + k, DT_LANE0 + k] = 1.0
    return jnp.asarray(sh, BF16), jnp.asarray(shs, BF16)


def _inproj_kernel(x_ref, g_ref, wa_ref, wb_ref, ws_ref, gm_ref, qkg_ref, rope_ref, sb_ref,
                   fq_ref, fkt_ref, fv_ref, dq_ref, dkt_ref, dv_ref, z_ref, xbc_ref, sm_ref):
    h = _rms(x_ref[...], g_ref[...]).astype(BF16)
    na = wa_ref.shape[1]

    def proj(a, b):
        if b <= na:
            return _dot(h, wa_ref[:, a:b])
        return _dot(h, wb_ref[:, a - na:b - na])

    def head_norm(a, idx):
        ssq = _dot((a * a).astype(BF16), gm_ref[...])
        return a * lax.rsqrt(ssq * (1.0 / HEAD_DIM) + EPS) * qkg_ref[idx]

    def rope(a):
        return (a * rope_ref[0] + pltpu.roll(a, ATT_W - ROPE_DIM // 2, 1) * rope_ref[1]
                + pltpu.roll(a, ROPE_DIM // 2, 1) * rope_ref[2])

    scale = HEAD_DIM ** -0.5
    fq_ref[...] = (head_norm(proj(0, 256), 0) * scale).astype(BF16)
    fkt_ref[0] = head_norm(proj(256, 512), 1).T.astype(BF16)
    fv_ref[...] = proj(512, 768).astype(BF16)
    dq_ref[...] = (rope(head_norm(proj(768, 1024), 2)) * scale).astype(BF16)
    dkt_ref[0] = rope(head_norm(proj(1024, 1280), 3)).T.astype(BF16)
    dv_ref[...] = proj(1280, 1536).astype(BF16)
    z_ref[...] = proj(1536, 2048)
    xbc_ref[...] = proj(2048, MAIN_W)
    v = _dot(h, ws_ref[...]) + sb_ref[...]
    e = jnp.log1p(jnp.exp(-jnp.abs(v)))
    lane = lax.broadcasted_iota(jnp.int32, v.shape, 1)
    sm_ref[...] = jnp.where(lane < N_FGATE, jnp.minimum(v, 0.0) - e, jnp.maximum(v, 0.0) + e)


def _inproj(x2d, g, wa, wb, ws, gm, qkg, rope, sb, B, S):
    T = x2d.shape[0]
    nst = S // TM_IN
    row = lambda i: (i, 0)
    const2 = lambda i: (0, 0)
    tr = lambda i: (i // nst, 0, i % nst)
    out_shape = [
        jax.ShapeDtypeStruct((T, ATT_W), BF16),
        jax.ShapeDtypeStruct((B, ATT_W, S), BF16),
        jax.ShapeDtypeStruct((T, ATT_W), BF16),
        jax.ShapeDtypeStruct((T, ATT_W), BF16),
        jax.ShapeDtypeStruct((B, ATT_W, S), BF16),
        jax.ShapeDtypeStruct((T, ATT_W), BF16),
        jax.ShapeDtypeStruct((T, SSD_INNER), F32),
        jax.ShapeDtypeStruct((T, SSD_CONV_CH), F32),
        jax.ShapeDtypeStruct((T, LANES), F32),
    ]
    att = pl.BlockSpec((TM_IN, ATT_W), row)
    att_t = pl.BlockSpec((1, ATT_W, TM_IN), tr)
    return pl.pallas_call(
        _inproj_kernel,
        grid=(T // TM_IN,),
        in_specs=[
            pl.BlockSpec((TM_IN, D_MODEL), row),
            pl.BlockSpec((1, D_MODEL), const2),
            pl.BlockSpec(wa.shape, const2),
            pl.BlockSpec(wb.shape, const2),
            pl.BlockSpec(ws.shape, const2),
            pl.BlockSpec((ATT_W, ATT_W), const2),
            pl.BlockSpec((4, 1, ATT_W), lambda i: (0, 0, 0)),
            pl.BlockSpec((3, TM_IN, ATT_W), lambda i: (0, i % nst, 0)),
            pl.BlockSpec((1, LANES), const2),
        ],
        out_specs=[att, att_t, att, att, att_t, att,
                   pl.BlockSpec((TM_IN, SSD_INNER), row),
                   pl.BlockSpec((TM_IN, SSD_CONV_CH), row),
                   pl.BlockSpec((TM_IN, LANES), row)],
        out_shape=out_shape,
        compiler_params=_cparams(1),
        name="inproj",
    )(x2d, g, wa, wb, ws, gm, qkg, rope, sb)


def _fox_scan_kernel(sm_ref, tri_ref, pq_ref, pk_ref, oq_ref, ok_ref, augq_ref, augkt_ref):
    S = sm_ref.shape[1]
    blk = tri_ref.shape[0]
    carry = jnp.zeros((1, LANES), F32)
    for b in range(S // blk):
        rows = slice(b * blk, (b + 1) * blk)
        c = _sum3(_dot(tri_ref[...], _split3(sm_ref[0, rows, :]))) + carry
        carry = c[blk - 1:blk, :]
        c3 = _split3(c)
        for h in range(ATT_HEADS):
            cols = slice(LANES * h, LANES * (h + 1))
            augq_ref[0, rows, cols] = (_dot(c3, pq_ref[h]) + oq_ref[h]).astype(BF16)
            augkt_ref[0, cols, rows] = (_dot(c3, pk_ref[h]) + ok_ref[h]).T.astype(BF16)


def _fox_scan(small, consts, B, S):
    tri, pq, pk, oq, ok = consts
    c2 = lambda b: (0, 0)
    c3 = lambda b: (0, 0, 0)
    return pl.pallas_call(
        _fox_scan_kernel,
        grid=(B,),
        in_specs=[
            pl.BlockSpec((1, S, LANES), lambda b: (b, 0, 0)),
            pl.BlockSpec(tri.shape, c2),
            pl.BlockSpec(pq.shape, c3),
            pl.BlockSpec(pk.shape, c3),
            pl.BlockSpec(oq.shape, c3),
            pl.BlockSpec(ok.shape, c3),
        ],
        out_specs=[pl.BlockSpec((1, S, ATT_HEADS * LANES), lambda b: (b, 0, 0)),
                   pl.BlockSpec((1, ATT_HEADS * LANES, S), lambda b: (b, 0, 0))],
        out_shape=[jax.ShapeDtypeStruct((B, S, ATT_HEADS * LANES), BF16),
                   jax.ShapeDtypeStruct((B, ATT_HEADS * LANES, S), BF16)],
        compiler_params=_cparams(1),
        name="fox_scan",
    )(small.reshape(B, S, LANES), tri, pq, pk, oq, ok)


def _attn_kernel(*refs, fox, nk):
    if fox:
        q_ref, kt_ref, v_ref, lm_ref, augq_ref, augkt_ref, o_ref, kt_scr, v_scr = refs
    else:
        q_ref, kt_ref, v_ref, lm_ref, o_ref, kt_scr, v_scr = refs
    qi = pl.program_id(1)

    @pl.when(qi == 0)
    def _prep():
        row = lax.broadcasted_iota(jnp.int32, (LANES, TK), 0)
        lane = lax.broadcasted_iota(jnp.int32, (TK, LANES), 1)
        for h in range(ATT_HEADS):
            p, mem = divmod(h, 2)
            pair = slice(LANES * p, LANES * (p + 1))
            for j in range(nk):
                keys = slice(j * TK, (j + 1) * TK)
                kd = kt_ref[0, pair, keys]
                if fox:
                    other = augkt_ref[0, LANES * h:LANES * (h + 1), keys]
                else:
                    other = jnp.zeros_like(kd)
                kt_scr[h, j] = jnp.where((row >> 6) == mem, kd, other)
                vd = v_ref[keys, pair]
                v_scr[h, j] = jnp.where((lane >> 6) == mem, vd, jnp.ones_like(vd))

    qlane = lax.broadcasted_iota(jnp.int32, (TQ, LANES), 1)
    qas = []
    for h in range(ATT_HEADS):
        p, mem = divmod(h, 2)
        qd = q_ref[:, LANES * p:LANES * (p + 1)]
        if fox:
            other = augq_ref[:, LANES * h:LANES * (h + 1)]
        else:
            other = jnp.zeros_like(qd)
        qas.append(jnp.where((qlane >> 6) == mem, qd, other))

    def step(j, carry, table):
        new = []
        for h in range(ATT_HEADS):
            m, acc = carry[h]
            s = _dot(qas[h], kt_scr[h, j])
            if table is not None:
                s = s + table
            m_new = jnp.maximum(m, jnp.max(s, axis=1, keepdims=True))
            alpha = jnp.exp(m - m_new)
            pr = jnp.exp(s - m_new).astype(BF16)
            new.append((m_new, alpha * acc + _dot(pr, v_scr[h, j])))
        return tuple(new)

    init = tuple((jnp.full((TQ, 1), NEG, F32), jnp.zeros((TQ, LANES), F32))
                 for _ in range(ATT_HEADS))

    def run(n, table):
        def two(p, c):
            return step(2 * p + 1, step(2 * p, c, table(2 * p)), table(2 * p + 1))
        c = lax.fori_loop(0, n // 2, two, init)
        return lax.cond(n % 2 == 1, lambda c: step(n - 1, c, table(n - 1)), lambda c: c, c)

    if fox:
        carry = step(qi, run(qi, lambda j: None), lm_ref[0])
    else:
        carry = run(qi + 1, lambda j: lm_ref[qi - j])
    outs = [acc / pltpu.roll(acc, HEAD_DIM, 1) for _, acc in carry]
    for p in range(ATT_HEADS // 2):
        o_ref[:, LANES * p:LANES * (p + 1)] = jnp.where(
            (qlane >> 6) == 0, outs[2 * p], outs[2 * p + 1]).astype(BF16)


def _attention(q, kt, v, lm, aug, B, S):
    fox = aug is not None
    nq, nk = S // TQ, S // TK
    in_specs = [
        pl.BlockSpec((TQ, ATT_W), lambda b, i: (b * nq + i, 0)),
        pl.BlockSpec((1, ATT_W, S), lambda b, i: (b, 0, 0)),
        pl.BlockSpec((S, ATT_W), lambda b, i: (b, 0)),
        pl.BlockSpec(lm.shape, lambda b, i: (0, 0, 0)),
    ]
    args = [q, kt, v, lm]
    if fox:
        in_specs += [pl.BlockSpec((TQ, ATT_HEADS * LANES), lambda b, i: (b * nq + i, 0)),
                     pl.BlockSpec((1, ATT_HEADS * LANES, S), lambda b, i: (b, 0, 0))]
        args += [aug[0].reshape(B * S, ATT_HEADS * LANES), aug[1]]
    return pl.pallas_call(
        functools.partial(_attn_kernel, fox=fox, nk=nk),
        grid=(B, nq),
        in_specs=in_specs,
        out_specs=pl.BlockSpec((TQ, ATT_W), lambda b, i: (b * nq + i, 0)),
        out_shape=jax.ShapeDtypeStruct((B * S, ATT_W), BF16),
        scratch_shapes=[pltpu.VMEM((ATT_HEADS, nk, LANES, TK), BF16),
                        pltpu.VMEM((ATT_HEADS, nk, TK, LANES), BF16)],
        compiler_params=_cparams(2),
        name="fox_attn" if fox else "dil_attn",
    )(*args)


def _ssd_kernel(xbc_ref, z_ref, sm_ref, cw_ref, cb_ref, arow_ref, dx_ref, ng_ref,
                tri_ref, pexp_ref, pq_ref, pk_ref, oq_ref, ok_ref, o_ref, buf, state):
    rows = xbc_ref.shape[0]

    @pl.when(pl.program_id(1) == 0)
    def _reset():
        buf[0:8, :] = jnp.zeros((8, SSD_CONV_CH), F32)
        state[...] = jnp.zeros(state.shape, F32)

    xb = xbc_ref[...]
    buf[8:8 + rows, :] = xb
    conv = cb_ref[...]
    for k in range(SSD_CONV):
        off = 8 - (SSD_CONV - 1) + k
        conv = conv + cw_ref[k:k + 1, :] * buf[off:off + rows, :]
    buf[0:8, :] = xb[rows - 8:rows, :]
    act = conv * jax.nn.sigmoid(conv)

    for c in range(rows // SSD_CHUNK):
        rs = slice(SSD_CHUNK * c, SSD_CHUNK * (c + 1))
        _ssd_chunk(act[rs, :], z_ref[rs, :], sm_ref[rs, :], arow_ref, dx_ref, ng_ref, tri_ref, pexp_ref,
                   pq_ref, pk_ref, oq_ref, ok_ref, o_ref.at[rs], state)


def _ssd_chunk(act, zz, dt, arow_ref, dx_ref, ng_ref, tri_ref, pexp_ref, pq_ref, pk_ref, oq_ref, ok_ref,
               o_ref, state):
    Q = SSD_CHUNK
    xs = act[:, 0:SSD_INNER]
    bm = act[:, SSD_INNER:SSD_INNER + 2 * SSD_STATE]
    cm = act[:, SSD_INNER + 2 * SSD_STATE:]

    acs = _sum3(_dot(tri_ref[...], _split3(dt * arow_ref[...])))
    acs3 = _split3(acs)
    ax = _dot(acs3, pexp_ref[...])
    dtx = _dot(_split3(dt), pexp_ref[...])
    last = ax[Q - 1:Q, :]
    ea = jnp.exp(ax)
    cdec = jnp.exp(last)
    xc = xs * dtx
    xcb = xc.astype(BF16)
    xcd = (xc * jnp.exp(last - ax)).astype(BF16)
    uq = (_dot(acs3, pq_ref[...]) + oq_ref[...]).astype(BF16)
    uk = (_dot(acs3, pk_ref[...]) + ok_ref[...]).astype(BF16)

    tril = (lax.broadcasted_iota(jnp.int32, (Q, Q), 0) >= lax.broadcasted_iota(jnp.int32, (Q, Q), 1))
    first = lax.broadcasted_iota(jnp.int32, (Q, LANES), 1) < HEAD_DIM
    ys = []
    for g in range(2):
        gs = slice(SSD_STATE * g, SSD_STATE * (g + 1))
        bg = bm[:, gs]
        cg = cm[:, gs].astype(BF16)
        cbm = _dot_nt(cg, bg.astype(BF16))
        bgt = bg.T.astype(BF16)
        for pp in range(2):
            p = 2 * g + pp
            ps = slice(LANES * p, LANES * (p + 1))
            ms = []
            for mem in range(2):
                hs = slice(LANES * (2 * p + mem), LANES * (2 * p + mem + 1))
                dm = _dot_nt(uq[:, hs], uk[:, hs])
                ms.append((cbm * jnp.exp(jnp.where(tril, dm, NEG))).astype(BF16))
            xp = xcb[:, ps]
            zero = jnp.zeros_like(xp)
            xcat = jnp.concatenate([jnp.where(first, xp, zero), jnp.where(first, zero, xp)], axis=0)
            y_diag = _dot(jnp.concatenate(ms, axis=1), xcat)
            st = state[p]
            y_off = _dot(cg, st.astype(BF16)) * ea[:, ps]
            state[p] = cdec[:, ps] * st + _dot(bgt, xcd[:, ps])
            ys.append(y_diag + y_off + xs[:, ps] * dx_ref[:, ps])
    y = jnp.concatenate(ys, axis=1)
    o_ref[...] = _rms(y * (zz * jax.nn.sigmoid(zz)), ng_ref[...]).astype(BF16)


def _ssd(xbc, z, small, cw, cb, arow, dx, ng, consts, B, S):
    nc = S // SSD_ROWS
    row = lambda b, c: (b * nc + c, 0)
    c2 = lambda b, c: (0, 0)
    full = lambda a: pl.BlockSpec(a.shape, c2)
    return pl.pallas_call(
        _ssd_kernel,
        grid=(B, nc),
        in_specs=[pl.BlockSpec((SSD_ROWS, SSD_CONV_CH), row),
                  pl.BlockSpec((SSD_ROWS, SSD_INNER), row),
                  pl.BlockSpec((SSD_ROWS, LANES), row),
                  full(cw), full(cb), full(arow), full(dx), full(ng)] + [full(a) for a in consts],
        out_specs=pl.BlockSpec((SSD_ROWS, SSD_INNER), row),
        out_shape=jax.ShapeDtypeStruct((B * S, SSD_INNER), BF16),
        scratch_shapes=[pltpu.VMEM((8 + SSD_ROWS, SSD_CONV_CH), F32),
                        pltpu.VMEM((SSD_HEADS // 2, SSD_STATE, LANES), F32)],
        compiler_params=_cparams(2),
        name="ssd",
    )(xbc, z, small, cw, cb, arow, dx, ng, *consts)


def _kv_kernel(mem_ref, g_ref, w_ref, kg_ref, kt_ref, v_ref):
    m = _rms(mem_ref[0], g_ref[...]).astype(BF16)
    kv = _dot(m, w_ref[...])
    for h in range(XA_HEADS):
        hs = slice(XA_HEAD_DIM * h, XA_HEAD_DIM * (h + 1))
        kt_ref[0, hs, :] = _rms(kv[:, hs], kg_ref[...]).T.astype(BF16)
    v_ref[0] = kv[:, D_MODEL:].astype(BF16)


def _kv(mem, g, w, kg):
    B = mem.shape[0]
    c2 = lambda b: (0, 0)
    return pl.pallas_call(
        _kv_kernel,
        grid=(B,),
        in_specs=[pl.BlockSpec((1, MEM_LEN, D_MODEL), lambda b: (b, 0, 0)),
                  pl.BlockSpec((1, D_MODEL), c2),
                  pl.BlockSpec((D_MODEL, 2 * D_MODEL), c2),
                  pl.BlockSpec((1, XA_HEAD_DIM), c2)],
        out_specs=[pl.BlockSpec((1, D_MODEL, MEM_LEN), lambda b: (b, 0, 0)),
                   pl.BlockSpec((1, MEM_LEN, D_MODEL), lambda b: (b, 0, 0))],
        out_shape=[jax.ShapeDtypeStruct((B, D_MODEL, MEM_LEN), BF16),
                   jax.ShapeDtypeStruct((B, MEM_LEN, D_MODEL), BF16)],
        compiler_params=_cparams(1),
        name="mem_kv",
    )(mem, g, w, kg)


ROW_SUB = D_MODEL // LANES


def _chunk(n, c):
    return pl.ds(c, n, stride=ROW_SUB)


def _to_row_tiles(ref, x):
    n = x.shape[0]
    for c in range(ROW_SUB):
        ref[_chunk(n, c), :] = x[:, LANES * c:LANES * (c + 1)]


def _from_row_tiles(ref):
    n = ref.shape[0] // ROW_SUB
    return jnp.concatenate([ref[_chunk(n, c), :] for c in range(ROW_SUB)], axis=1)


def _row_copy(src, dst, sem, src_row, dst_row):
    return pltpu.make_async_copy(
        src.at[pl.ds(pl.multiple_of(src_row * ROW_SUB, ROW_SUB), ROW_SUB)],
        dst.at[pl.ds(pl.multiple_of(dst_row * ROW_SUB, ROW_SUB), ROW_SUB)], sem)


def _mid_kernel(x_ref, of_ref, od_ref, os_ref, wo_ref, g2_ref, wq_ref, qg_ref, kt_ref, v_ref,
                wxo_ref, g3_ref, wra_ref, wrb_ref, rb_ref, tri_ref,
                x2_ref, h3_ref, ri_ref, cnt_ref, run):
    @pl.when(pl.program_id(0) == 0)
    def _reset():
        run[...] = jnp.zeros(run.shape, F32)

    x1 = (x_ref[...] + _dot(of_ref[...], wo_ref[0:ATT_W, :])
          + _dot(od_ref[...], wo_ref[ATT_W:2 * ATT_W, :])
          + _dot(os_ref[...], wo_ref[2 * ATT_W:, :]))

    q = _dot(_rms(x1, g2_ref[...]).astype(BF16), wq_ref[...])
    heads = []
    for h in range(XA_HEADS):
        hs = slice(XA_HEAD_DIM * h, XA_HEAD_DIM * (h + 1))
        qn = (_rms(q[:, hs], qg_ref[...]) * XA_HEAD_DIM ** -0.5).astype(BF16)
        s = _dot(qn, kt_ref[0, hs, :])
        e = jnp.exp(s - jnp.max(s, axis=1, keepdims=True))
        o = _dot(e.astype(BF16), v_ref[0, :, hs]) / jnp.sum(e, axis=1, keepdims=True)
        heads.append(o.astype(BF16))
    x2 = x1 + _dot(jnp.concatenate(heads, axis=1), wxo_ref[...])
    x2_ref[...] = x2

    h3 = _rms(x2, g3_ref[...])
    _to_row_tiles(h3_ref, h3)

    hi = h3.astype(BF16)
    mid = (h3 - hi.astype(F32)).astype(BF16)
    both = _dot(hi, wra_ref[...])
    logits = _dot(mid, wrb_ref[...]) + both[:, LANES:] + both[:, 0:LANES] + rb_ref[...]

    lane = lax.broadcasted_iota(jnp.int32, logits.shape, 1)
    lanef = lane.astype(F32)
    big = float(LANES)

    def first_max(vals):
        top = jnp.max(vals, axis=1, keepdims=True)
        return top, jnp.min(jnp.where(vals == top, lanef, big), axis=1, keepdims=True)

    gl = jnp.where(lane < N_GROUPS, logits, NEG)
    gmax, gsel = first_max(gl)
    ggate = 1.0 / jnp.sum(jnp.exp(gl - gmax), axis=1, keepdims=True)
    grp = ((lane - N_GROUPS) >> 3).astype(F32)
    el = jnp.where(grp == gsel, logits, NEG)
    v1, i1 = first_max(el)
    v2, i2 = first_max(jnp.where(lanef == i1, NEG, el))
    t = jnp.exp(v2 - v1)
    p1 = 1.0 / (1.0 + t)
    e1 = i1 - N_GROUPS
    e2 = i2 - N_GROUPS

    hit1 = lanef == e1
    hit2 = lanef == e2
    onehot = jnp.where(hit1 | hit2, 1.0, 0.0)
    before = _dot(tri_ref[...], onehot.astype(BF16)) + run[...]
    r1 = jnp.sum(jnp.where(hit1, before, 0.0), axis=1, keepdims=True)
    r2 = jnp.sum(jnp.where(hit2, before, 0.0), axis=1, keepdims=True)
    run[...] = run[...] + jnp.sum(onehot, axis=0, keepdims=True)

    cols = (e1, e2, p1 * ggate, t * p1 * ggate, r1, r2)
    info = jnp.zeros(logits.shape, F32)
    for k, col in enumerate(cols):
        info = jnp.where(lane == k, col, info)
    ri_ref[...] = info
    cnt_ref[...] = jnp.broadcast_to(run[...], cnt_ref.shape)


def _mid(x2d, o_fox, o_dil, o_ssd, wo, g2, wq, qg, kt, v, wxo, g3, wra, wrb, rb, tri, S):
    T = x2d.shape[0]
    npb = S // TM_MID
    row = lambda i: (i, 0)
    c2 = lambda i: (0, 0)
    return pl.pallas_call(
        _mid_kernel,
        grid=(T // TM_MID,),
        in_specs=[pl.BlockSpec((TM_MID, D_MODEL), row),
                  pl.BlockSpec((TM_MID, ATT_W), row),
                  pl.BlockSpec((TM_MID, ATT_W), row),
                  pl.BlockSpec((TM_MID, SSD_INNER), row),
                  pl.BlockSpec((D_MODEL, D_MODEL), c2),
                  pl.BlockSpec((1, D_MODEL), c2),
                  pl.BlockSpec((D_MODEL, D_MODEL), c2),
                  pl.BlockSpec((1, XA_HEAD_DIM), c2),
                  pl.BlockSpec((1, D_MODEL, MEM_LEN), lambda i: (i // npb, 0, 0)),
                  pl.BlockSpec((1, MEM_LEN, D_MODEL), lambda i: (i // npb, 0, 0)),
                  pl.BlockSpec((D_MODEL, D_MODEL), c2),
                  pl.BlockSpec((1, D_MODEL), c2),
                  pl.BlockSpec((D_MODEL, 2 * LANES), c2),
                  pl.BlockSpec((D_MODEL, LANES), c2),
                  pl.BlockSpec((1, LANES), c2),
                  pl.BlockSpec((TM_MID, TM_MID), c2)],
        out_specs=[pl.BlockSpec((TM_MID, D_MODEL), row),
                   pl.BlockSpec((TM_MID * ROW_SUB, LANES), row),
                   pl.BlockSpec((TM_MID, LANES), row),
                   pl.BlockSpec((8, LANES), c2)],
        out_shape=[jax.ShapeDtypeStruct((T, D_MODEL), F32),
                   jax.ShapeDtypeStruct((T * ROW_SUB, LANES), F32),
                   jax.ShapeDtypeStruct((T, LANES), F32),
                   jax.ShapeDtypeStruct((8, LANES), F32)],
        scratch_shapes=[pltpu.VMEM((1, LANES), F32)],
        compiler_params=_cparams(1),
        name="mid",
    )(x2d, o_fox, o_dil, o_ssd, wo, g2, wq, qg, kt, v, wxo, g3, wra, wrb, rb, tri)


def _dest_kernel(ri_ref, st_ref, o_ref):
    info = ri_ref[...].T
    n = info.shape[1]
    expert = lax.broadcasted_iota(jnp.int32, (N_EXPERTS, n), 0).astype(F32)
    row = lax.broadcasted_iota(jnp.int32, (8, n), 0)
    out = jnp.zeros((8, n), F32)
    for k in range(2):
        start = jnp.sum(jnp.where(expert == info[k:k + 1, :], st_ref[...], 0.0), axis=0, keepdims=True)
        out = jnp.where(row == k, start + info[4 + k:5 + k, :], out)
    o_ref[...] = out.astype(jnp.int32)


def _dest(rinfo, starts):
    T = rinfo.shape[0]
    return pl.pallas_call(
        _dest_kernel,
        grid=(T // TD,),
        in_specs=[pl.BlockSpec((TD, LANES), lambda i: (i, 0)),
                  pl.BlockSpec((N_EXPERTS, 1), lambda i: (0, 0))],
        out_specs=pl.BlockSpec((8, TD), lambda i: (0, i)),
        out_shape=jax.ShapeDtypeStruct((8, T), jnp.int32),
        compiler_params=_cparams(1),
        name="moe_dest",
    )(rinfo, starts)


def _dispatch_kernel(d0_ref, d1_ref, h_ref, xs_in, xs_out, sem):
    del xs_in

    def issue(r, carry):
        _row_copy(h_ref, xs_out, sem, r, d0_ref[r]).start(priority=0)
        _row_copy(h_ref, xs_out, sem, r, d1_ref[r]).start(priority=1)
        return carry

    def drain(r, carry):
        _row_copy(h_ref, xs_out, sem, 0, 0).wait()
        _row_copy(h_ref, xs_out, sem, 0, 0).wait()
        return carry

    lax.fori_loop(0, TD, issue, 0, unroll=8)
    lax.fori_loop(0, TD, drain, 0, unroll=8)


def _dispatch(dest0, dest1, h3t, xs_init):
    T = h3t.shape[0] // ROW_SUB
    idx = pl.BlockSpec((TD,), lambda i: (i,), memory_space=pltpu.SMEM)
    return pl.pallas_call(
        _dispatch_kernel,
        grid=(T // TD,),
        in_specs=[idx, idx,
                  pl.BlockSpec((TD * ROW_SUB, LANES), lambda i: (i, 0)),
                  pl.BlockSpec(memory_space=pl.ANY)],
        out_specs=pl.BlockSpec(memory_space=pl.ANY),
        out_shape=jax.ShapeDtypeStruct(xs_init.shape, F32),
        scratch_shapes=[pltpu.SemaphoreType.DMA],
        input_output_aliases={3: 0},
        compiler_params=_cparams(1),
        name="moe_dispatch",
    )(dest0, dest1, h3t, xs_init)


def _expert_kernel(fb_ref, xs_hbm, w1_ref, w3_ref, w2_ref, y_hbm, w1s, w3s, w2s, xin, yout, isem, osem):
    e = pl.program_id(0)
    n_used = fb_ref[N_EXPERTS]
    blk = TM_MOE * ROW_SUB

    def rows(g):
        return pl.ds(pl.multiple_of(g * blk, blk), blk)

    def in_copy(g, slot):
        return pltpu.make_async_copy(xs_hbm.at[rows(g)], xin.at[slot], isem.at[slot])

    def out_copy(g, slot):
        return pltpu.make_async_copy(yout.at[slot], y_hbm.at[rows(g)], osem.at[slot])

    @pl.when((e == 0) & (n_used > 0))
    def _prime():
        in_copy(0, 0).start()

    w1s[...] = w1_ref[...].astype(BF16)
    w3s[...] = w3_ref[...].astype(BF16)
    w2s[...] = w2_ref[...].astype(BF16)

    def block(g, carry):
        slot = g % 2

        @pl.when(g + 1 < n_used)
        def _prefetch():
            in_copy(g + 1, 1 - slot).start()

        in_copy(g, slot).wait()

        @pl.when(g >= 2)
        def _free_out_slot():
            out_copy(g - 2, slot).wait()

        xb = _from_row_tiles(xin.at[slot]).astype(BF16)
        a = _dot(xb, w1s[...])
        b = _dot(xb, w3s[...])
        _to_row_tiles(yout.at[slot], _dot((a * jax.nn.sigmoid(a) * b).astype(BF16), w2s[...]))
        out_copy(g, slot).start()
        return carry

    lax.fori_loop(fb_ref[e], fb_ref[e + 1], block, 0)

    @pl.when(e == pl.num_programs(0) - 1)
    def _drain():
        for back in (2, 1):
            @pl.when(n_used >= back)
            def _wait():
                out_copy(n_used - back, (n_used - back) % 2).wait()


def _experts(first_blk, xs, w1, w3, w2, layer):
    wmap = lambda e, fb: (layer, e, 0, 0)
    blk = TM_MOE * ROW_SUB
    return pl.pallas_call(
        _expert_kernel,
        grid_spec=pltpu.PrefetchScalarGridSpec(
            num_scalar_prefetch=1,
            grid=(N_EXPERTS,),
            in_specs=[pl.BlockSpec(memory_space=pl.ANY),
                      pl.BlockSpec((None, None, D_MODEL, EXPERT_FF), wmap),
                      pl.BlockSpec((None, None, D_MODEL, EXPERT_FF), wmap),
                      pl.BlockSpec((None, None, EXPERT_FF, D_MODEL), wmap)],
            out_specs=pl.BlockSpec(memory_space=pl.ANY),
            scratch_shapes=[pltpu.VMEM((D_MODEL, EXPERT_FF), BF16),
                            pltpu.VMEM((D_MODEL, EXPERT_FF), BF16),
                            pltpu.VMEM((EXPERT_FF, D_MODEL), BF16),
                            pltpu.VMEM((2, blk, LANES), F32),
                            pltpu.VMEM((2, blk, LANES), F32),
                            pltpu.SemaphoreType.DMA((2,)),
                            pltpu.SemaphoreType.DMA((2,))]),
        out_shape=jax.ShapeDtypeStruct(xs.shape, F32),
        input_output_aliases={1: 0},
        compiler_params=_cparams(1),
        name="moe_experts",
    )(first_blk, xs, w1, w3, w2)


def _combine_kernel(d0_ref, d1_ref, d0n_ref, d1n_ref, x2_ref, ri_ref, y_hbm, o_ref, buf, sem):
    i = pl.program_id(0)
    slot = i % 2

    def gather(d0, d1, s):
        def issue(r, carry):
            _row_copy(y_hbm, buf.at[s, 0], sem.at[s], d0[r], r).start(priority=0)
            _row_copy(y_hbm, buf.at[s, 1], sem.at[s], d1[r], r).start(priority=1)
            return carry
        lax.fori_loop(0, TC, issue, 0, unroll=8)

    @pl.when(i == 0)
    def _first():
        gather(d0_ref, d1_ref, 0)

    @pl.when(i + 1 < pl.num_programs(0))
    def _next():
        gather(d0n_ref, d1n_ref, 1 - slot)

    def drain(r, carry):
        _row_copy(y_hbm, buf.at[slot, 0], sem.at[slot], 0, 0).wait()
        _row_copy(y_hbm, buf.at[slot, 1], sem.at[slot], 0, 0).wait()
        return carry

    lax.fori_loop(0, TC, drain, 0, unroll=8)
    info = ri_ref[...]
    g0 = info[:, 2:3]
    g1 = info[:, 3:4]
    for c in range(D_MODEL // LANES):
        cols = slice(LANES * c, LANES * (c + 1))
        o_ref[:, cols] = (x2_ref[:, cols] + g0 * buf[slot, 0, _chunk(TC, c), :]
                          + g1 * buf[slot, 1, _chunk(TC, c), :])


def _combine(dest0, dest1, x2, rinfo, ybuf):
    T = x2.shape[0]
    row = lambda i: (i, 0)
    nsteps = T // TC
    idx = pl.BlockSpec((TC,), lambda i: (i,), memory_space=pltpu.SMEM)
    nxt = pl.BlockSpec((TC,), lambda i: (jnp.minimum(i + 1, nsteps - 1),), memory_space=pltpu.SMEM)
    return pl.pallas_call(
        _combine_kernel,
        grid=(nsteps,),
        in_specs=[idx, idx, nxt, nxt,
                  pl.BlockSpec((TC, D_MODEL), row),
                  pl.BlockSpec((TC, LANES), row),
                  pl.BlockSpec(memory_space=pl.ANY)],
        out_specs=pl.BlockSpec((TC, D_MODEL), row),
        out_shape=jax.ShapeDtypeStruct((T, D_MODEL), F32),
        scratch_shapes=[pltpu.VMEM((2, 2, TC * ROW_SUB, LANES), F32),
                        pltpu.SemaphoreType.DMA((2,))],
        compiler_params=_cparams(1),
        name="moe_combine",
    )(dest0, dest1, dest0, dest1, x2, rinfo, ybuf)


def _tri(n, strict):
    return jnp.asarray(np.tril(np.ones((n, n), np.float32), -1 if strict else 0), BF16)


def _rope_tables(S):
    half = ROPE_DIM // 2
    inv = jnp.power(ROPE_THETA, -2.0 * jnp.arange(half, dtype=F32) / ROPE_DIM)
    ang = jnp.arange(S).astype(F32)[:, None] * inv[None, :]
    cos, sin = jnp.cos(ang), jnp.sin(ang)
    d = np.arange(ATT_W) % HEAD_DIM
    idx = d % half
    c = jnp.where(d < ROPE_DIM, cos[:, idx], 1.0)
    s1 = jnp.where(d < half, -sin[:, idx], 0.0)
    s2 = jnp.where((d >= half) & (d < ROPE_DIM), sin[:, idx], 0.0)
    return jnp.stack([c, s1, s2]).astype(F32)


def _fox_consts():
    pq = np.zeros((ATT_HEADS, 3 * LANES, LANES), np.float32)
    pk = np.zeros_like(pq)
    oq = np.zeros((ATT_HEADS, 1, LANES), np.float32)
    ok = np.zeros_like(oq)
    for h in range(ATT_HEADS):
        off = HEAD_DIM if h % 2 == 0 else 0
        for k in range(3):
            pq[h, k * LANES + h, off + k] = 1.0
            oq[h, 0, off + 3 + k] = 1.0
            pk[h, k * LANES + h, off + 3 + k] = -1.0
            ok[h, 0, off + k] = 1.0
    return (_tri(256, False), jnp.asarray(pq, BF16), jnp.asarray(pk, BF16),
            jnp.asarray(oq), jnp.asarray(ok))


def _ssd_consts():
    pexp = np.zeros((3 * LANES, SSD_INNER), np.float32)
    pq = np.zeros((3 * LANES, SSD_HEADS * LANES), np.float32)
    pk = np.zeros_like(pq)
    oq = np.zeros((1, SSD_HEADS * LANES), np.float32)
    ok = np.zeros_like(oq)
    for h in range(SSD_HEADS):
        for k in range(3):
            src = k * LANES + DT_LANE0 + h
            pexp[src, HEAD_DIM * h:HEAD_DIM * (h + 1)] = 1.0
            pq[src, LANES * h + k] = 1.0
            oq[0, LANES * h + 3 + k] = 1.0
            pk[src, LANES * h + 3 + k] = -1.0
            ok[0, LANES * h + k] = 1.0
    return (_tri(SSD_CHUNK, False), jnp.asarray(pexp, BF16), jnp.asarray(pq, BF16),
            jnp.asarray(pk, BF16), jnp.asarray(oq), jnp.asarray(ok))


def _score_tables(S):
    nd = S // TK
    i = np.arange(TQ)[:, None]
    j = np.arange(TK)[None, :]
    causal = np.where(i >= j, 0.0, NEG).astype(np.float32)[None]
    dil = np.zeros((nd, TQ, TK), np.float32)
    for d in range(nd):
        delta = d * TK + i - j
        mult = np.zeros((TQ, TK), np.float64)
        for window, step in DIL_CONFIGS:
            mult += (delta >= 0) & (delta <= window) & (delta % step == 0)
        dil[d] = np.where(mult > 0, np.log(np.maximum(mult, 1.0)), NEG)
    return jnp.asarray(causal), jnp.asarray(dil)


def _group_matrix():
    g = np.arange(ATT_W) // HEAD_DIM
    return jnp.asarray((g[:, None] == g[None, :]).astype(np.float32), BF16)


def _pad_lanes(v, lane0):
    return jnp.zeros((1, LANES), F32).at[0, lane0:lane0 + v.shape[0]].set(v)


def _layer_params(l, w_in, wprep_consts, fox_fgate_b, fox_qn_g, fox_kn_g, dil_qn_g, dil_kn_g,
                  ssd_dt_bias, ssd_A_log, ssd_D, router_wg, router_bg, router_we, router_be):
    w_r = _wprep(w_in, l, *wprep_consts)
    tile4 = lambda g: jnp.tile(g, ATT_HEADS)[None, :]
    qkg = jnp.stack([tile4(fox_qn_g[l]), tile4(fox_kn_g[l]), tile4(dil_qn_g[l]), tile4(dil_kn_g[l])])
    sb = _pad_lanes(fox_fgate_b[l], 0) + _pad_lanes(ssd_dt_bias[l], DT_LANE0)
    arow = _pad_lanes(-jnp.exp(ssd_A_log[l]), DT_LANE0)
    dx = jnp.repeat(ssd_D[l], HEAD_DIM)[None, :]
    wr = jnp.concatenate([router_wg[l], router_we[l],
                          jnp.zeros((D_MODEL, LANES - N_GROUPS - N_EXPERTS), F32)], axis=1)
    wr3 = _split3(wr)
    rb = _pad_lanes(router_bg[l], 0) + _pad_lanes(router_be[l], N_GROUPS)
    return w_r, qkg, sb, arow, dx, wr3[:, 0:2 * LANES], wr3[:, 0:LANES], rb


def kernel(x, mem, norm1_g, w_in, fox_fgate_b, fox_qn_g, fox_kn_g, dil_qn_g, dil_kn_g, ssd_conv_w,
           ssd_conv_b, ssd_dt_bias, ssd_A_log, ssd_D, ssd_norm_g, w_out, norm2_g, mem_norm_g, xa_wq,
           xa_wkv, xa_qn_g, xa_kn_g, xa_wo, norm3_g, router_wg, router_bg, router_we, router_be,
           exp_w1, exp_w3, exp_w2):
    B, S, _ = x.shape
    T = B * S
    depth = w_in.shape[0]
    assert S % TM_IN == 0 and S % TQ == 0 and T % TD == 0 and S >= DIL_CONFIGS[-1][0]

    rope = _rope_tables(S)
    fox_consts = _fox_consts()
    ssd_consts = _ssd_consts()
    causal, dil_tab = _score_tables(S)
    gm = _group_matrix()
    wprep_consts = _wprep_consts()
    tri_mid = _tri(TM_MID, True)
    nblk = (2 * T) // TM_MOE + N_EXPERTS

    x2d = x.reshape(T, D_MODEL)
    xs = jnp.zeros((nblk * TM_MOE * ROW_SUB, LANES), F32)
    for l in range(depth):
        w_r, qkg, sb, arow, dx, wra, wrb, rb = _layer_params(
            l, w_in, wprep_consts, fox_fgate_b, fox_qn_g, fox_kn_g, dil_qn_g, dil_kn_g,
            ssd_dt_bias, ssd_A_log, ssd_D, router_wg, router_bg, router_we, router_be)

        fq, fkt, fv, dq, dkt, dv, z, xbc, small = _inproj(
            x2d, norm1_g[l][None, :], *w_r, gm, qkg, rope, sb, B, S)
        aug = _fox_scan(small, fox_consts, B, S)
        o_fox = _attention(fq, fkt, fv, causal, aug, B, S)
        o_dil = _attention(dq, dkt, dv, dil_tab, None, B, S)
        o_ssd = _ssd(xbc, z, small, ssd_conv_w[l], ssd_conv_b[l][None, :], arow, dx,
                     ssd_norm_g[l][None, :], ssd_consts, B, S)

        kt, v = _kv(mem, mem_norm_g[l][None, :], xa_wkv[l].astype(BF16), xa_kn_g[l][None, :])
        x2, h3t, rinfo, cnt = _mid(x2d, o_fox, o_dil, o_ssd, w_out[l].astype(BF16),
                                   norm2_g[l][None, :], xa_wq[l].astype(BF16), xa_qn_g[l][None, :],
                                   kt, v, xa_wo[l].astype(BF16), norm3_g[l][None, :], wra, wrb, rb,
                                   tri_mid, S)

        counts = cnt[0, :N_EXPERTS].astype(jnp.int32)
        padded = (counts + TM_MOE - 1) // TM_MOE * TM_MOE
        ends = jnp.cumsum(padded)
        dest = _dest(rinfo, (ends - padded).astype(F32)[:, None])
        dest0, dest1 = dest[0], dest[1]
        first_blk = (jnp.concatenate([jnp.zeros((1,), jnp.int32), ends]) // TM_MOE).astype(jnp.int32)

        xs = _dispatch(dest0, dest1, h3t, xs)
        xs = _experts(first_blk, xs, exp_w1, exp_w3, exp_w2, l)
        x2d = _combine(dest0, dest1, x2, rinfo, xs)
    return x2d.reshape(B, S, D_MODEL)
```

```python
import functools

import jax
import jax.numpy as jnp
import numpy as np
from jax import lax
from jax.experimental import pallas as pl
from jax.experimental.pallas import tpu as pltpu

F32 = jnp.float32
BF16 = jnp.bfloat16

D_MODEL = 1024
HEAD_DIM = 64
ATT_HEADS = 4
ATT_W = ATT_HEADS * HEAD_DIM
SSD_HEADS = 8
SSD_INNER = 512
SSD_STATE = 128
SSD_CONV = 4
SSD_CHUNK = 128
SSD_CONV_CH = 1024
XA_HEADS = 4
XA_HEAD_DIM = 256
MEM_LEN = 256
N_GROUPS = 4
EXPERTS_PER_GROUP = 8
N_EXPERTS = 32
EXPERT_FF = 512
DIL_CONFIGS = ((128, 1), (512, 4), (2048, 16))
ROPE_THETA = 500000.0
ROPE_DIM = 16
EPS = 1e-6
NEG = -1e30
LOG2E = 1.4426950408889634

LANES = 128
N_FGATE = 4
DT_LANE0 = 4
MAIN_W = 3 * ATT_W + 3 * ATT_W + SSD_INNER + SSD_CONV_CH
IN_W = MAIN_W + LANES

TM_IN = 512
TQ = 256
TK = 256
TM_MID = 512
TM_MOE = 256
SSD_ROWS = 256
TD = 1024
TC = 1024
VMEM_LIMIT = 48 * 1024 * 1024


def _cparams(n_axes):
    return pltpu.CompilerParams(dimension_semantics=("arbitrary",) * n_axes,
                                vmem_limit_bytes=VMEM_LIMIT)


def _rms(x, g):
    return x * lax.rsqrt(jnp.mean(x * x, axis=-1, keepdims=True) + EPS) * g


def _split3(x):
    hi = x.astype(BF16)
    r = x - hi.astype(F32)
    mid = r.astype(BF16)
    lo = (r - mid.astype(F32)).astype(BF16)
    return jnp.concatenate([hi, mid, lo], axis=1)


def _dot(a, b):
    return jnp.dot(a, b, preferred_element_type=F32)


def _dot_nt(a, b):
    return lax.dot_general(a, b, (((1,), (1,)), ((), ())), preferred_element_type=F32)


def _sum3(c):
    w = c.shape[1] // 3
    return c[:, 0:w] + c[:, w:2 * w] + c[:, 2 * w:3 * w]


FF_COL = 3 * ATT_W
DQ_COL = FF_COL + N_FGATE
DT_COL = DQ_COL + MAIN_W - 3 * ATT_W
IN_SRC_W = DT_COL + 8
WPREP_ROWS = 256


def _wprep_kernel(w_ref, tail_ref, sh_ref, shs_ref, wa_ref, wb_ref, ws_ref):
    wa_ref[...] = w_ref[:, 0:FF_COL].astype(BF16)
    lane = lax.broadcasted_iota(jnp.int32, tail_ref.shape, 1)
    tail = jnp.where(lane < IN_SRC_W - MAIN_W, tail_ref[...], 0.0).astype(BF16)
    nb = (MAIN_W - FF_COL) // LANES
    for n in range(nb):
        lo = FF_COL + LANES * n
        if n + 1 < nb:
            pair = w_ref[:, lo:lo + 2 * LANES].astype(BF16)
        else:
            pair = jnp.concatenate([w_ref[:, lo:lo + LANES].astype(BF16), tail], axis=1)
        wb_ref[:, LANES * n:LANES * (n + 1)] = _dot(pair, sh_ref[...]).astype(BF16)
    small = jnp.concatenate([w_ref[:, FF_COL:FF_COL + LANES].astype(BF16), tail], axis=1)
    ws_ref[...] = _dot(small, shs_ref[...]).astype(BF16)


def _wprep(w_in, layer, sh, shs):
    row = lambda i: (i, 0)
    c2 = lambda i: (0, 0)
    wb_w = MAIN_W - FF_COL
    return pl.pallas_call(
        _wprep_kernel,
        grid=(D_MODEL // WPREP_ROWS,),
        in_specs=[pl.BlockSpec((None, WPREP_ROWS, MAIN_W), lambda i: (layer, i, 0)),
                  pl.BlockSpec((None, WPREP_ROWS, LANES), lambda i: (layer, i, MAIN_W // LANES)),
                  pl.BlockSpec(sh.shape, c2), pl.BlockSpec(shs.shape, c2)],
        out_specs=[pl.BlockSpec((WPREP_ROWS, FF_COL), row),
                   pl.BlockSpec((WPREP_ROWS, wb_w), row),
                   pl.BlockSpec((WPREP_ROWS, LANES), row)],
        out_shape=[jax.ShapeDtypeStruct((D_MODEL, FF_COL), BF16),
                   jax.ShapeDtypeStruct((D_MODEL, wb_w), BF16),
                   jax.ShapeDtypeStruct((D_MODEL, LANES), BF16)],
        compiler_params=_cparams(1),
        name="wprep",
    )(w_in, w_in, sh, shs)


def _wprep_consts():
    sh = np.zeros((2 * LANES, LANES), np.float32)
    shs = np.zeros((2 * LANES, LANES), np.float32)
    off = DQ_COL - FF_COL
    for j in range(LANES):
        sh[j + off, j] = 1.0
    for k in range(N_FGATE):
        shs[k, k] = 1.0
    for k in range(SSD_HEADS):
        shs[LANES + DT_COL - MAIN_W + k, DT_LANE0 + k] = 1.0
    return jnp.asarray(sh, BF16), jnp.asarray(shs, BF16)


def _inproj_kernel(x_ref, g_ref, wa_ref, wb_ref, ws_ref, gm_ref, qkg_ref, rope_ref, sb_ref,
                   fq_ref, fkt_ref, fv_ref, dq_ref, dkt_ref, dv_ref, z_ref, xbc_ref, sm_ref):
    h = _rms(x_ref[...], g_ref[...]).astype(BF16)
    na = wa_ref.shape[1]

    def proj(a, b):
        if b <= na:
            return _dot(h, wa_ref[:, a:b])
        return _dot(h, wb_ref[:, a - na:b - na])

    def head_norm(a, idx):
        ssq = _dot((a * a).astype(BF16), gm_ref[...])
        return a * lax.rsqrt(ssq * (1.0 / HEAD_DIM) + EPS) * qkg_ref[idx]

    def rope(a):
        return (a * rope_ref[0] + pltpu.roll(a, ATT_W - ROPE_DIM // 2, 1) * rope_ref[1]
                + pltpu.roll(a, ROPE_DIM // 2, 1) * rope_ref[2])

    scale = HEAD_DIM ** -0.5 * LOG2E
    fq_ref[...] = (head_norm(proj(0, 256), 0) * scale).astype(BF16)
    fkt_ref[0] = head_norm(proj(256, 512), 1).T.astype(BF16)
    fv_ref[...] = proj(512, 768).astype(BF16)
    dq_ref[...] = (rope(head_norm(proj(768, 1024), 2)) * scale).astype(BF16)
    dkt_ref[0] = rope(head_norm(proj(1024, 1280), 3)).T.astype(BF16)
    dv_ref[...] = proj(1280, 1536).astype(BF16)
    z_ref[...] = proj(1536, 2048)
    xbc_ref[...] = proj(2048, MAIN_W)
    v = _dot(h, ws_ref[...]) + sb_ref[...]
    e = jnp.log1p(jnp.exp(-jnp.abs(v)))
    lane = lax.broadcasted_iota(jnp.int32, v.shape, 1)
    sm_ref[...] = jnp.where(lane < N_FGATE, jnp.minimum(v, 0.0) - e, jnp.maximum(v, 0.0) + e)


def _inproj(x2d, g, wa, wb, ws, gm, qkg, rope, sb, B, S):
    T = x2d.shape[0]
    nst = S // TM_IN
    row = lambda i: (i, 0)
    const2 = lambda i: (0, 0)
    tr = lambda i: (i // nst, 0, i % nst)
    out_shape = [
        jax.ShapeDtypeStruct((T, ATT_W), BF16),
        jax.ShapeDtypeStruct((B, ATT_W, S), BF16),
        jax.ShapeDtypeStruct((T, ATT_W), BF16),
        jax.ShapeDtypeStruct((T, ATT_W), BF16),
        jax.ShapeDtypeStruct((B, ATT_W, S), BF16),
        jax.ShapeDtypeStruct((T, ATT_W), BF16),
        jax.ShapeDtypeStruct((T, SSD_INNER), F32),
        jax.ShapeDtypeStruct((T, SSD_CONV_CH), F32),
        jax.ShapeDtypeStruct((T, LANES), F32),
    ]
    att = pl.BlockSpec((TM_IN, ATT_W), row)
    att_t = pl.BlockSpec((1, ATT_W, TM_IN), tr)
    return pl.pallas_call(
        _inproj_kernel,
        grid=(T // TM_IN,),
        in_specs=[
            pl.BlockSpec((TM_IN, D_MODEL), row),
            pl.BlockSpec((1, D_MODEL), const2),
            pl.BlockSpec(wa.shape, const2),
            pl.BlockSpec(wb.shape, const2),
            pl.BlockSpec(ws.shape, const2),
            pl.BlockSpec((ATT_W, ATT_W), const2),
            pl.BlockSpec((4, 1, ATT_W), lambda i: (0, 0, 0)),
            pl.BlockSpec((3, TM_IN, ATT_W), lambda i: (0, i % nst, 0)),
            pl.BlockSpec((1, LANES), const2),
        ],
        out_specs=[att, att_t, att, att, att_t, att,
                   pl.BlockSpec((TM_IN, SSD_INNER), row),
                   pl.BlockSpec((TM_IN, SSD_CONV_CH), row),
                   pl.BlockSpec((TM_IN, LANES), row)],
        out_shape=out_shape,
        compiler_params=_cparams(1),
        name="inproj",
    )(x2d, g, wa, wb, ws, gm, qkg, rope, sb)


def _fox_scan_kernel(sm_ref, tri_ref, pq_ref, pk_ref, oq_ref, ok_ref, augq_ref, augkt_ref):
    S = sm_ref.shape[1]
    blk = tri_ref.shape[0]
    carry = jnp.zeros((1, LANES), F32)
    for b in range(S // blk):
        rows = slice(b * blk, (b + 1) * blk)
        c = _sum3(_dot(tri_ref[...], _split3(sm_ref[0, rows, :]))) + carry
        carry = c[blk - 1:blk, :]
        c3 = _split3(c * LOG2E)
        augq_ref[0, rows, :] = (_dot(c3, pq_ref[...]) + oq_ref[...]).astype(BF16)
        ak = _dot(c3, pk_ref[...]) + ok_ref[...]
        for h in range(ATT_HEADS):
            cols = slice(LANES * h, LANES * (h + 1))
            augkt_ref[0, cols, rows] = ak[:, cols].T.astype(BF16)


def _fox_scan(small, consts, B, S):
    tri, pq, pk, oq, ok = consts
    c2 = lambda b: (0, 0)
    return pl.pallas_call(
        _fox_scan_kernel,
        grid=(B,),
        in_specs=[
            pl.BlockSpec((1, S, LANES), lambda b: (b, 0, 0)),
            pl.BlockSpec(tri.shape, c2),
            pl.BlockSpec(pq.shape, c2),
            pl.BlockSpec(pk.shape, c2),
            pl.BlockSpec(oq.shape, c2),
            pl.BlockSpec(ok.shape, c2),
        ],
        out_specs=[pl.BlockSpec((1, S, ATT_HEADS * LANES), lambda b: (b, 0, 0)),
                   pl.BlockSpec((1, ATT_HEADS * LANES, S), lambda b: (b, 0, 0))],
        out_shape=[jax.ShapeDtypeStruct((B, S, ATT_HEADS * LANES), BF16),
                   jax.ShapeDtypeStruct((B, ATT_HEADS * LANES, S), BF16)],
        compiler_params=_cparams(1),
        name="fox_scan",
    )(small.reshape(B, S, LANES), tri, pq, pk, oq, ok)


def _attn_kernel(*refs, fox, nk):
    if fox:
        q_ref, kt_ref, v_ref, lm_ref, augq_ref, augkt_ref, o_ref, kt_scr, v_scr = refs
    else:
        q_ref, kt_ref, v_ref, lm_ref, o_ref, kt_scr, v_scr = refs
    qi = pl.program_id(1)

    @pl.when(qi == 0)
    def _prep():
        row = lax.broadcasted_iota(jnp.int32, (LANES, TK), 0)
        lane = lax.broadcasted_iota(jnp.int32, (TK, LANES), 1)
        for h in range(ATT_HEADS):
            p, mem = divmod(h, 2)
            pair = slice(LANES * p, LANES * (p + 1))
            for j in range(nk):
                keys = slice(j * TK, (j + 1) * TK)
                kd = kt_ref[0, pair, keys]
                if fox:
                    other = augkt_ref[0, LANES * h:LANES * (h + 1), keys]
                else:
                    other = jnp.zeros_like(kd)
                kt_scr[h, j] = jnp.where((row >> 6) == mem, kd, other)
                vd = v_ref[keys, pair]
                v_scr[h, j] = jnp.where((lane >> 6) == mem, vd, jnp.ones_like(vd))

    qlane = lax.broadcasted_iota(jnp.int32, (TQ, LANES), 1)
    qas = []
    for h in range(ATT_HEADS):
        p, mem = divmod(h, 2)
        qd = q_ref[:, LANES * p:LANES * (p + 1)]
        if fox:
            other = augq_ref[:, LANES * h:LANES * (h + 1)]
        else:
            other = jnp.zeros_like(qd)
        qas.append(jnp.where((qlane >> 6) == mem, qd, other))

    def step(j, carry, table):
        new = []
        for h in range(ATT_HEADS):
            m, acc = carry[h]
            s = _dot(qas[h], kt_scr[h, j])
            if table is not None:
                s = s + table
            m_new = jnp.maximum(m, jnp.max(s, axis=1, keepdims=True))
            alpha = jnp.exp2(m - m_new)
            pr = jnp.exp2(s - m_new).astype(BF16)
            new.append((m_new, alpha * acc + _dot(pr, v_scr[h, j])))
        return tuple(new)

    init = tuple((jnp.full((TQ, 1), NEG, F32), jnp.zeros((TQ, LANES), F32))
                 for _ in range(ATT_HEADS))

    if fox:
        table = lambda j: lm_ref[jnp.minimum(qi - j, 1)]
    else:
        table = lambda j: lm_ref[qi - j]

    def steps(j0, count, c):
        for u in range(count):
            c = step(j0 + u, c, table(j0 + u))
        return c

    n = qi + 1
    carry = lax.fori_loop(0, n // 4, lambda p, c: steps(4 * p, 4, c), init)
    carry = lax.cond(n % 4 >= 2, lambda c: steps(n - n % 4, 2, c), lambda c: c, carry)
    carry = lax.cond(n % 2 == 1, lambda c: steps(n - 1, 1, c), lambda c: c, carry)
    outs = [acc / pltpu.roll(acc, HEAD_DIM, 1) for _, acc in carry]
    for p in range(ATT_HEADS // 2):
        o_ref[:, LANES * p:LANES * (p + 1)] = jnp.where(
            (qlane >> 6) == 0, outs[2 * p], outs[2 * p + 1]).astype(BF16)


def _attention(q, kt, v, lm, aug, B, S):
    fox = aug is not None
    nq, nk = S // TQ, S // TK
    in_specs = [
        pl.BlockSpec((TQ, ATT_W), lambda b, i: (b * nq + i, 0)),
        pl.BlockSpec((1, ATT_W, S), lambda b, i: (b, 0, 0)),
        pl.BlockSpec((S, ATT_W), lambda b, i: (b, 0)),
        pl.BlockSpec(lm.shape, lambda b, i: (0, 0, 0)),
    ]
    args = [q, kt, v, lm]
    if fox:
        in_specs += [pl.BlockSpec((TQ, ATT_HEADS * LANES), lambda b, i: (b * nq + i, 0)),
                     pl.BlockSpec((1, ATT_HEADS * LANES, S), lambda b, i: (b, 0, 0))]
        args += [aug[0].reshape(B * S, ATT_HEADS * LANES), aug[1]]
    return pl.pallas_call(
        functools.partial(_attn_kernel, fox=fox, nk=nk),
        grid=(B, nq),
        in_specs=in_specs,
        out_specs=pl.BlockSpec((TQ, ATT_W), lambda b, i: (b * nq + i, 0)),
        out_shape=jax.ShapeDtypeStruct((B * S, ATT_W), BF16),
        scratch_shapes=[pltpu.VMEM((ATT_HEADS, nk, LANES, TK), BF16),
                        pltpu.VMEM((ATT_HEADS, nk, TK, LANES), BF16)],
        compiler_params=_cparams(2),
        name="fox_attn" if fox else "dil_attn",
    )(*args)


def _ssd_kernel(xbc_ref, z_ref, sm_ref, cw_ref, cb_ref, arow_ref, dx_ref, ng_ref,
                tri_ref, pexp_ref, pq_ref, pk_ref, oq_ref, ok_ref, o_ref, buf, state):
    rows = xbc_ref.shape[0]

    @pl.when(pl.program_id(1) == 0)
    def _reset():
        buf[0:8, :] = jnp.zeros((8, SSD_CONV_CH), F32)
        state[...] = jnp.zeros(state.shape, F32)

    xb = xbc_ref[...]
    buf[8:8 + rows, :] = xb
    conv = cb_ref[...]
    for k in range(SSD_CONV):
        off = 8 - (SSD_CONV - 1) + k
        conv = conv + cw_ref[k:k + 1, :] * buf[off:off + rows, :]
    buf[0:8, :] = xb[rows - 8:rows, :]
    act = conv * jax.nn.sigmoid(conv)

    for c in range(rows // SSD_CHUNK):
        rs = slice(SSD_CHUNK * c, SSD_CHUNK * (c + 1))
        _ssd_chunk(act[rs, :], z_ref[rs, :], sm_ref[rs, :], arow_ref, dx_ref, ng_ref, tri_ref, pexp_ref,
                   pq_ref, pk_ref, oq_ref, ok_ref, o_ref.at[rs], state)


def _ssd_chunk(act, zz, dt, arow_ref, dx_ref, ng_ref, tri_ref, pexp_ref, pq_ref, pk_ref, oq_ref, ok_ref,
               o_ref, state):
    Q = SSD_CHUNK
    xs = act[:, 0:SSD_INNER]
    bm = act[:, SSD_INNER:SSD_INNER + 2 * SSD_STATE]
    cm = act[:, SSD_INNER + 2 * SSD_STATE:]

    acs = _sum3(_dot(tri_ref[...], _split3(dt * arow_ref[...])))
    acs3 = _split3(acs)
    ax = _dot(acs3, pexp_ref[...])
    dtx = _dot(_split3(dt), pexp_ref[...])
    last = ax[Q - 1:Q, :]
    ea = jnp.exp(ax)
    cdec = jnp.exp(last)
    xc = xs * dtx
    xcb = xc.astype(BF16)
    xcd = (xc * jnp.exp(last - ax)).astype(BF16)
    uq = (_dot(acs3, pq_ref[...]) + oq_ref[...]).astype(BF16)
    uk = (_dot(acs3, pk_ref[...]) + ok_ref[...]).astype(BF16)

    tril = (lax.broadcasted_iota(jnp.int32, (Q, Q), 0) >= lax.broadcasted_iota(jnp.int32, (Q, Q), 1))
    first = lax.broadcasted_iota(jnp.int32, (Q, LANES), 1) < HEAD_DIM
    ys = []
    for g in range(2):
        gs = slice(SSD_STATE * g, SSD_STATE * (g + 1))
        bg = bm[:, gs]
        cg = cm[:, gs].astype(BF16)
        cbm = _dot_nt(cg, bg.astype(BF16))
        bgt = bg.T.astype(BF16)
        for pp in range(2):
            p = 2 * g + pp
            ps = slice(LANES * p, LANES * (p + 1))
            ms = []
            for mem in range(2):
                hs = slice(LANES * (2 * p + mem), LANES * (2 * p + mem + 1))
                dm = _dot_nt(uq[:, hs], uk[:, hs])
                ms.append((cbm * jnp.exp(jnp.where(tril, dm, NEG))).astype(BF16))
            xp = xcb[:, ps]
            zero = jnp.zeros_like(xp)
            xcat = jnp.concatenate([jnp.where(first, xp, zero), jnp.where(first, zero, xp)], axis=0)
            y_diag = _dot(jnp.concatenate(ms, axis=1), xcat)
            st = state[p]
            y_off = _dot(cg, st.astype(BF16)) * ea[:, ps]
            state[p] = cdec[:, ps] * st + _dot(bgt, xcd[:, ps])
            ys.append(y_diag + y_off + xs[:, ps] * dx_ref[:, ps])
    y = jnp.concatenate(ys, axis=1)
    o_ref[...] = _rms(y * (zz * jax.nn.sigmoid(zz)), ng_ref[...]).astype(BF16)


def _ssd(xbc, z, small, cw, cb, arow, dx, ng, consts, B, S):
    nc = S // SSD_ROWS
    row = lambda b, c: (b * nc + c, 0)
    c2 = lambda b, c: (0, 0)
    full = lambda a: pl.BlockSpec(a.shape, c2)
    return pl.pallas_call(
        _ssd_kernel,
        grid=(B, nc),
        in_specs=[pl.BlockSpec((SSD_ROWS, SSD_CONV_CH), row),
                  pl.BlockSpec((SSD_ROWS, SSD_INNER), row),
                  pl.BlockSpec((SSD_ROWS, LANES), row),
                  full(cw), full(cb), full(arow), full(dx), full(ng)] + [full(a) for a in consts],
        out_specs=pl.BlockSpec((SSD_ROWS, SSD_INNER), row),
        out_shape=jax.ShapeDtypeStruct((B * S, SSD_INNER), BF16),
        scratch_shapes=[pltpu.VMEM((8 + SSD_ROWS, SSD_CONV_CH), F32),
                        pltpu.VMEM((SSD_HEADS // 2, SSD_STATE, LANES), F32)],
        compiler_params=_cparams(2),
        name="ssd",
    )(xbc, z, small, cw, cb, arow, dx, ng, *consts)


def _kv_kernel(mem_ref, g_ref, w_ref, kg_ref, kt_ref, v_ref):
    m = _rms(mem_ref[0], g_ref[...]).astype(BF16)
    kv = _dot(m, w_ref[...])
    for h in range(XA_HEADS):
        hs = slice(XA_HEAD_DIM * h, XA_HEAD_DIM * (h + 1))
        kt_ref[0, hs, :] = _rms(kv[:, hs], kg_ref[...]).T.astype(BF16)
    v_ref[0] = kv[:, D_MODEL:].astype(BF16)


def _kv(mem, g, w, kg):
    B = mem.shape[0]
    c2 = lambda b: (0, 0)
    return pl.pallas_call(
        _kv_kernel,
        grid=(B,),
        in_specs=[pl.BlockSpec((1, MEM_LEN, D_MODEL), lambda b: (b, 0, 0)),
                  pl.BlockSpec((1, D_MODEL), c2),
                  pl.BlockSpec((D_MODEL, 2 * D_MODEL), c2),
                  pl.BlockSpec((1, XA_HEAD_DIM), c2)],
        out_specs=[pl.BlockSpec((1, D_MODEL, MEM_LEN), lambda b: (b, 0, 0)),
                   pl.BlockSpec((1, MEM_LEN, D_MODEL), lambda b: (b, 0, 0))],
        out_shape=[jax.ShapeDtypeStruct((B, D_MODEL, MEM_LEN), BF16),
                   jax.ShapeDtypeStruct((B, MEM_LEN, D_MODEL), BF16)],
        compiler_params=_cparams(1),
        name="mem_kv",
    )(mem, g, w, kg)


ROW_SUB = D_MODEL // LANES


def _chunk(n, c):
    return pl.ds(c, n, stride=ROW_SUB)


def _to_row_tiles(ref, x):
    n = x.shape[0]
    for c in range(ROW_SUB):
        ref[_chunk(n, c), :] = x[:, LANES * c:LANES * (c + 1)]


def _from_row_tiles(ref):
    n = ref.shape[0] // ROW_SUB
    return jnp.concatenate([ref[_chunk(n, c), :] for c in range(ROW_SUB)], axis=1)


def _row_copy(src, dst, sem, src_row, dst_row):
    return pltpu.make_async_copy(
        src.at[pl.ds(pl.multiple_of(src_row * ROW_SUB, ROW_SUB), ROW_SUB)],
        dst.at[pl.ds(pl.multiple_of(dst_row * ROW_SUB, ROW_SUB), ROW_SUB)], sem)


def _mid_kernel(x_ref, of_ref, od_ref, os_ref, wo_ref, g2_ref, wq_ref, qg_ref, kt_ref, v_ref,
                wxo_ref, g3_ref, wra_ref, wrb_ref, rb_ref, tri_ref,
                x2_ref, h3_ref, ri_ref, cnt_ref, run):
    @pl.when(pl.program_id(0) == 0)
    def _reset():
        run[...] = jnp.zeros(run.shape, F32)

    x1 = (x_ref[...] + _dot(of_ref[...], wo_ref[0:ATT_W, :])
          + _dot(od_ref[...], wo_ref[ATT_W:2 * ATT_W, :])
          + _dot(os_ref[...], wo_ref[2 * ATT_W:, :]))

    q = _dot(_rms(x1, g2_ref[...]).astype(BF16), wq_ref[...])
    heads = []
    for h in range(XA_HEADS):
        hs = slice(XA_HEAD_DIM * h, XA_HEAD_DIM * (h + 1))
        qn = (_rms(q[:, hs], qg_ref[...]) * XA_HEAD_DIM ** -0.5).astype(BF16)
        s = _dot(qn, kt_ref[0, hs, :])
        e = jnp.exp(s - jnp.max(s, axis=1, keepdims=True))
        o = _dot(e.astype(BF16), v_ref[0, :, hs]) / jnp.sum(e, axis=1, keepdims=True)
        heads.append(o.astype(BF16))
    x2 = x1 + _dot(jnp.concatenate(heads, axis=1), wxo_ref[...])
    x2_ref[...] = x2

    h3 = _rms(x2, g3_ref[...])
    _to_row_tiles(h3_ref, h3)

    hi = h3.astype(BF16)
    mid = (h3 - hi.astype(F32)).astype(BF16)
    both = _dot(hi, wra_ref[...])
    logits = _dot(mid, wrb_ref[...]) + both[:, LANES:] + both[:, 0:LANES] + rb_ref[...]

    lane = lax.broadcasted_iota(jnp.int32, logits.shape, 1)
    lanef = lane.astype(F32)
    big = float(LANES)

    def first_max(vals):
        top = jnp.max(vals, axis=1, keepdims=True)
        return top, jnp.min(jnp.where(vals == top, lanef, big), axis=1, keepdims=True)

    gl = jnp.where(lane < N_GROUPS, logits, NEG)
    gmax, gsel = first_max(gl)
    ggate = 1.0 / jnp.sum(jnp.exp(gl - gmax), axis=1, keepdims=True)
    grp = ((lane - N_GROUPS) >> 3).astype(F32)
    el = jnp.where(grp == gsel, logits, NEG)
    v1, i1 = first_max(el)
    v2, i2 = first_max(jnp.where(lanef == i1, NEG, el))
    t = jnp.exp(v2 - v1)
    p1 = 1.0 / (1.0 + t)
    e1 = i1 - N_GROUPS
    e2 = i2 - N_GROUPS

    hit1 = lanef == e1
    hit2 = lanef == e2
    onehot = jnp.where(hit1 | hit2, 1.0, 0.0)
    before = _dot(tri_ref[...], onehot.astype(BF16)) + run[...]
    r1 = jnp.sum(jnp.where(hit1, before, 0.0), axis=1, keepdims=True)
    r2 = jnp.sum(jnp.where(hit2, before, 0.0), axis=1, keepdims=True)
    run[...] = run[...] + jnp.sum(onehot, axis=0, keepdims=True)

    cols = (e1, e2, p1 * ggate, t * p1 * ggate, r1, r2)
    info = jnp.zeros(logits.shape, F32)
    for k, col in enumerate(cols):
        info = jnp.where(lane == k, col, info)
    ri_ref[...] = info
    cnt_ref[...] = jnp.broadcast_to(run[...], cnt_ref.shape)


def _mid(x2d, o_fox, o_dil, o_ssd, wo, g2, wq, qg, kt, v, wxo, g3, wra, wrb, rb, tri, S):
    T = x2d.shape[0]
    npb = S // TM_MID
    row = lambda i: (i, 0)
    c2 = lambda i: (0, 0)
    return pl.pallas_call(
        _mid_kernel,
        grid=(T // TM_MID,),
        in_specs=[pl.BlockSpec((TM_MID, D_MODEL), row),
                  pl.BlockSpec((TM_MID, ATT_W), row),
                  pl.BlockSpec((TM_MID, ATT_W), row),
                  pl.BlockSpec((TM_MID, SSD_INNER), row),
                  pl.BlockSpec((D_MODEL, D_MODEL), c2),
                  pl.BlockSpec((1, D_MODEL), c2),
                  pl.BlockSpec((D_MODEL, D_MODEL), c2),
                  pl.BlockSpec((1, XA_HEAD_DIM), c2),
                  pl.BlockSpec((1, D_MODEL, MEM_LEN), lambda i: (i // npb, 0, 0)),
                  pl.BlockSpec((1, MEM_LEN, D_MODEL), lambda i: (i // npb, 0, 0)),
                  pl.BlockSpec((D_MODEL, D_MODEL), c2),
                  pl.BlockSpec((1, D_MODEL), c2),
                  pl.BlockSpec((D_MODEL, 2 * LANES), c2),
                  pl.BlockSpec((D_MODEL, LANES), c2),
                  pl.BlockSpec((1, LANES), c2),
                  pl.BlockSpec((TM_MID, TM_MID), c2)],
        out_specs=[pl.BlockSpec((TM_MID, D_MODEL), row),
                   pl.BlockSpec((TM_MID * ROW_SUB, LANES), row),
                   pl.BlockSpec((TM_MID, LANES), row),
                   pl.BlockSpec((8, LANES), c2)],
        out_shape=[jax.ShapeDtypeStruct((T, D_MODEL), F32),
                   jax.ShapeDtypeStruct((T * ROW_SUB, LANES), F32),
                   jax.ShapeDtypeStruct((T, LANES), F32),
                   jax.ShapeDtypeStruct((8, LANES), F32)],
        scratch_shapes=[pltpu.VMEM((1, LANES), F32)],
        compiler_params=_cparams(1),
        name="mid",
    )(x2d, o_fox, o_dil, o_ssd, wo, g2, wq, qg, kt, v, wxo, g3, wra, wrb, rb, tri)


def _dest_kernel(ri_ref, st_ref, o_ref):
    info = ri_ref[...].T
    n = info.shape[1]
    expert = lax.broadcasted_iota(jnp.int32, (N_EXPERTS, n), 0).astype(F32)
    row = lax.broadcasted_iota(jnp.int32, (8, n), 0)
    out = jnp.zeros((8, n), F32)
    for k in range(2):
        start = jnp.sum(jnp.where(expert == info[k:k + 1, :], st_ref[...], 0.0), axis=0, keepdims=True)
        out = jnp.where(row == k, start + info[4 + k:5 + k, :], out)
    o_ref[...] = out.astype(jnp.int32)


def _dest(rinfo, starts):
    T = rinfo.shape[0]
    return pl.pallas_call(
        _dest_kernel,
        grid=(T // TD,),
        in_specs=[pl.BlockSpec((TD, LANES), lambda i: (i, 0)),
                  pl.BlockSpec((N_EXPERTS, 1), lambda i: (0, 0))],
        out_specs=pl.BlockSpec((8, TD), lambda i: (0, i)),
        out_shape=jax.ShapeDtypeStruct((8, T), jnp.int32),
        compiler_params=_cparams(1),
        name="moe_dest",
    )(rinfo, starts)


def _dispatch_kernel(d0_ref, d1_ref, h_ref, xs_in, xs_out, sem):
    del xs_in

    def issue(r, carry):
        _row_copy(h_ref, xs_out, sem, r, d0_ref[r]).start(priority=0)
        _row_copy(h_ref, xs_out, sem, r, d1_ref[r]).start(priority=1)
        return carry

    def drain(r, carry):
        _row_copy(h_ref, xs_out, sem, 0, 0).wait()
        _row_copy(h_ref, xs_out, sem, 0, 0).wait()
        return carry

    lax.fori_loop(0, TD, issue, 0, unroll=8)
    lax.fori_loop(0, TD, drain, 0, unroll=8)


def _dispatch(dest0, dest1, h3t, xs_init):
    T = h3t.shape[0] // ROW_SUB
    idx = pl.BlockSpec((TD,), lambda i: (i,), memory_space=pltpu.SMEM)
    return pl.pallas_call(
        _dispatch_kernel,
        grid=(T // TD,),
        in_specs=[idx, idx,
                  pl.BlockSpec((TD * ROW_SUB, LANES), lambda i: (i, 0)),
                  pl.BlockSpec(memory_space=pl.ANY)],
        out_specs=pl.BlockSpec(memory_space=pl.ANY),
        out_shape=jax.ShapeDtypeStruct(xs_init.shape, F32),
        scratch_shapes=[pltpu.SemaphoreType.DMA],
        input_output_aliases={3: 0},
        compiler_params=_cparams(1),
        name="moe_dispatch",
    )(dest0, dest1, h3t, xs_init)


def _expert_kernel(fb_ref, xs_hbm, w1_ref, w3_ref, w2_ref, y_hbm, w1s, w3s, w2s, xin, yout, isem, osem):
    e = pl.program_id(0)
    n_used = fb_ref[N_EXPERTS]
    blk = TM_MOE * ROW_SUB

    def rows(g):
        return pl.ds(pl.multiple_of(g * blk, blk), blk)

    def in_copy(g, slot):
        return pltpu.make_async_copy(xs_hbm.at[rows(g)], xin.at[slot], isem.at[slot])

    def out_copy(g, slot):
        return pltpu.make_async_copy(yout.at[slot], y_hbm.at[rows(g)], osem.at[slot])

    @pl.when((e == 0) & (n_used > 0))
    def _prime():
        in_copy(0, 0).start()

    w1s[...] = w1_ref[...].astype(BF16)
    w3s[...] = w3_ref[...].astype(BF16)
    w2s[...] = w2_ref[...].astype(BF16)

    def block(g, carry):
        slot = g % 2

        @pl.when(g + 1 < n_used)
        def _prefetch():
            in_copy(g + 1, 1 - slot).start()

        in_copy(g, slot).wait()

        @pl.when(g >= 2)
        def _free_out_slot():
            out_copy(g - 2, slot).wait()

        xb = _from_row_tiles(xin.at[slot]).astype(BF16)
        a = _dot(xb, w1s[...])
        b = _dot(xb, w3s[...])
        _to_row_tiles(yout.at[slot], _dot((a * jax.nn.sigmoid(a) * b).astype(BF16), w2s[...]))
        out_copy(g, slot).start()
        return carry

    lax.fori_loop(fb_ref[e], fb_ref[e + 1], block, 0)

    @pl.when(e == pl.num_programs(0) - 1)
    def _drain():
        for back in (2, 1):
            @pl.when(n_used >= back)
            def _wait():
                out_copy(n_used - back, (n_used - back) % 2).wait()


def _experts(first_blk, xs, w1, w3, w2, layer):
    wmap = lambda e, fb: (layer, e, 0, 0)
    blk = TM_MOE * ROW_SUB
    return pl.pallas_call(
        _expert_kernel,
        grid_spec=pltpu.PrefetchScalarGridSpec(
            num_scalar_prefetch=1,
            grid=(N_EXPERTS,),
            in_specs=[pl.BlockSpec(memory_space=pl.ANY),
                      pl.BlockSpec((None, None, D_MODEL, EXPERT_FF), wmap),
                      pl.BlockSpec((None, None, D_MODEL, EXPERT_FF), wmap),
                      pl.BlockSpec((None, None, EXPERT_FF, D_MODEL), wmap)],
            out_specs=pl.BlockSpec(memory_space=pl.ANY),
            scratch_shapes=[pltpu.VMEM((D_MODEL, EXPERT_FF), BF16),
                            pltpu.VMEM((D_MODEL, EXPERT_FF), BF16),
                            pltpu.VMEM((EXPERT_FF, D_MODEL), BF16),
                            pltpu.VMEM((2, blk, LANES), F32),
                            pltpu.VMEM((2, blk, LANES), F32),
                            pltpu.SemaphoreType.DMA((2,)),
                            pltpu.SemaphoreType.DMA((2,))]),
        out_shape=jax.ShapeDtypeStruct(xs.shape, F32),
        input_output_aliases={1: 0},
        compiler_params=_cparams(1),
        name="moe_experts",
    )(first_blk, xs, w1, w3, w2)


def _combine_kernel(d0_ref, d1_ref, d0n_ref, d1n_ref, x2_ref, ri_ref, y_hbm, o_ref, buf, sem):
    i = pl.program_id(0)
    slot = i % 2

    def gather(d0, d1, s):
        def issue(r, carry):
            _row_copy(y_hbm, buf.at[s, 0], sem.at[s], d0[r], r).start(priority=0)
            _row_copy(y_hbm, buf.at[s, 1], sem.at[s], d1[r], r).start(priority=1)
            return carry
        lax.fori_loop(0, TC, issue, 0, unroll=8)

    @pl.when(i == 0)
    def _first():
        gather(d0_ref, d1_ref, 0)

    @pl.when(i + 1 < pl.num_programs(0))
    def _next():
        gather(d0n_ref, d1n_ref, 1 - slot)

    def drain(r, carry):
        _row_copy(y_hbm, buf.at[slot, 0], sem.at[slot], 0, 0).wait()
        _row_copy(y_hbm, buf.at[slot, 1], sem.at[slot], 0, 0).wait()
        return carry

    lax.fori_loop(0, TC, drain, 0, unroll=8)
    info = ri_ref[...]
    g0 = info[:, 2:3]
    g1 = info[:, 3:4]
    for c in range(D_MODEL // LANES):
        cols = slice(LANES * c, LANES * (c + 1))
        o_ref[:, cols] = (x2_ref[:, cols] + g0 * buf[slot, 0, _chunk(TC, c), :]
                          + g1 * buf[slot, 1, _chunk(TC, c), :])


def _combine(dest0, dest1, x2, rinfo, ybuf):
    T = x2.shape[0]
    row = lambda i: (i, 0)
    nsteps = T // TC
    idx = pl.BlockSpec((TC,), lambda i: (i,), memory_space=pltpu.SMEM)
    nxt = pl.BlockSpec((TC,), lambda i: (jnp.minimum(i + 1, nsteps - 1),), memory_space=pltpu.SMEM)
    return pl.pallas_call(
        _combine_kernel,
        grid=(nsteps,),
        in_specs=[idx, idx, nxt, nxt,
                  pl.BlockSpec((TC, D_MODEL), row),
                  pl.BlockSpec((TC, LANES), row),
                  pl.BlockSpec(memory_space=pl.ANY)],
        out_specs=pl.BlockSpec((TC, D_MODEL), row),
        out_shape=jax.ShapeDtypeStruct((T, D_MODEL), F32),
        scratch_shapes=[pltpu.VMEM((2, 2, TC * ROW_SUB, LANES), F32),
                        pltpu.SemaphoreType.DMA((2,))],
        compiler_params=_cparams(1),
        name="moe_combine",
    )(dest0, dest1, dest0, dest1, x2, rinfo, ybuf)


def _tri(n, strict):
    return jnp.asarray(np.tril(np.ones((n, n), np.float32), -1 if strict else 0), BF16)


def _rope_tables(S):
    half = ROPE_DIM // 2
    inv = jnp.power(ROPE_THETA, -2.0 * jnp.arange(half, dtype=F32) / ROPE_DIM)
    ang = jnp.arange(S).astype(F32)[:, None] * inv[None, :]
    cos, sin = jnp.cos(ang), jnp.sin(ang)
    d = np.arange(ATT_W) % HEAD_DIM
    idx = d % half
    c = jnp.where(d < ROPE_DIM, cos[:, idx], 1.0)
    s1 = jnp.where(d < half, -sin[:, idx], 0.0)
    s2 = jnp.where((d >= half) & (d < ROPE_DIM), sin[:, idx], 0.0)
    return jnp.stack([c, s1, s2]).astype(F32)


def _fox_consts():
    pq = np.zeros((3 * LANES, ATT_HEADS * LANES), np.float32)
    pk = np.zeros_like(pq)
    oq = np.zeros((1, ATT_HEADS * LANES), np.float32)
    ok = np.zeros_like(oq)
    for h in range(ATT_HEADS):
        off = LANES * h + (HEAD_DIM if h % 2 == 0 else 0)
        for k in range(3):
            pq[k * LANES + h, off + k] = 1.0
            oq[0, off + 3 + k] = 1.0
            pk[k * LANES + h, off + 3 + k] = -1.0
            ok[0, off + k] = 1.0
    return (_tri(256, False), jnp.asarray(pq, BF16), jnp.asarray(pk, BF16),
            jnp.asarray(oq), jnp.asarray(ok))


def _ssd_consts():
    pexp = np.zeros((3 * LANES, SSD_INNER), np.float32)
    pq = np.zeros((3 * LANES, SSD_HEADS * LANES), np.float32)
    pk = np.zeros_like(pq)
    oq = np.zeros((1, SSD_HEADS * LANES), np.float32)
    ok = np.zeros_like(oq)
    for h in range(SSD_HEADS):
        for k in range(3):
            src = k * LANES + DT_LANE0 + h
            pexp[src, HEAD_DIM * h:HEAD_DIM * (h + 1)] = 1.0
            pq[src, LANES * h + k] = 1.0
            oq[0, LANES * h + 3 + k] = 1.0
            pk[src, LANES * h + 3 + k] = -1.0
            ok[0, LANES * h + k] = 1.0
    return (_tri(SSD_CHUNK, False), jnp.asarray(pexp, BF16), jnp.asarray(pq, BF16),
            jnp.asarray(pk, BF16), jnp.asarray(oq), jnp.asarray(ok))


def _score_tables(S):
    nd = S // TK
    i = np.arange(TQ)[:, None]
    j = np.arange(TK)[None, :]
    causal = np.stack([np.where(i >= j, 0.0, NEG), np.zeros((TQ, TK))]).astype(np.float32)
    dil = np.zeros((nd, TQ, TK), np.float32)
    for d in range(nd):
        delta = d * TK + i - j
        mult = np.zeros((TQ, TK), np.float64)
        for window, step in DIL_CONFIGS:
            mult += (delta >= 0) & (delta <= window) & (delta % step == 0)
        dil[d] = np.where(mult > 0, np.log2(np.maximum(mult, 1.0)), NEG)
    return jnp.asarray(causal), jnp.asarray(dil)


def _group_matrix():
    g = np.arange(ATT_W) // HEAD_DIM
    return jnp.asarray((g[:, None] == g[None, :]).astype(np.float32), BF16)


def _pad_lanes(v, lane0):
    return jnp.zeros((1, LANES), F32).at[0, lane0:lane0 + v.shape[0]].set(v)


def _layer_params(l, w_in, wprep_consts, fox_fgate_b, fox_qn_g, fox_kn_g, dil_qn_g, dil_kn_g,
                  ssd_dt_bias, ssd_A_log, ssd_D, router_wg, router_bg, router_we, router_be):
    w_r = _wprep(w_in, l, *wprep_consts)
    tile4 = lambda g: jnp.tile(g, ATT_HEADS)[None, :]
    qkg = jnp.stack([tile4(fox_qn_g[l]), tile4(fox_kn_g[l]), tile4(dil_qn_g[l]), tile4(dil_kn_g[l])])
    sb = _pad_lanes(fox_fgate_b[l], 0) + _pad_lanes(ssd_dt_bias[l], DT_LANE0)
    arow = _pad_lanes(-jnp.exp(ssd_A_log[l]), DT_LANE0)
    dx = jnp.repeat(ssd_D[l], HEAD_DIM)[None, :]
    wr = jnp.concatenate([router_wg[l], router_we[l],
                          jnp.zeros((D_MODEL, LANES - N_GROUPS - N_EXPERTS), F32)], axis=1)
    wr3 = _split3(wr)
    rb = _pad_lanes(router_bg[l], 0) + _pad_lanes(router_be[l], N_GROUPS)
    return w_r, qkg, sb, arow, dx, wr3[:, 0:2 * LANES], wr3[:, 0:LANES], rb


def kernel(x, mem, norm1_g, w_in, fox_fgate_b, fox_qn_g, fox_kn_g, dil_qn_g, dil_kn_g, ssd_conv_w,
           ssd_conv_b, ssd_dt_bias, ssd_A_log, ssd_D, ssd_norm_g, w_out, norm2_g, mem_norm_g, xa_wq,
           xa_wkv, xa_qn_g, xa_kn_g, xa_wo, norm3_g, router_wg, router_bg, router_we, router_be,
           exp_w1, exp_w3, exp_w2):
    B, S, _ = x.shape
    T = B * S
    depth = w_in.shape[0]
    assert S % TM_IN == 0 and S % TQ == 0 and T % TD == 0 and S >= DIL_CONFIGS[-1][0]

    rope = _rope_tables(S)
    fox_consts = _fox_consts()
    ssd_consts = _ssd_consts()
    causal, dil_tab = _score_tables(S)
    gm = _group_matrix()
    wprep_consts = _wprep_consts()
    tri_mid = _tri(TM_MID, True)
    nblk = (2 * T) // TM_MOE + N_EXPERTS

    x2d = x.reshape(T, D_MODEL)
    xs = jnp.zeros((nblk * TM_MOE * ROW_SUB, LANES), F32)
    for l in range(depth):
        w_r, qkg, sb, arow, dx, wra, wrb, rb = _layer_params(
            l, w_in, wprep_consts, fox_fgate_b, fox_qn_g, fox_kn_g, dil_qn_g, dil_kn_g,
            ssd_dt_bias, ssd_A_log, ssd_D, router_wg, router_bg, router_we, router_be)

        fq, fkt, fv, dq, dkt, dv, z, xbc, small = _inproj(
            x2d, norm1_g[l][None, :], *w_r, gm, qkg, rope, sb, B, S)
        aug = _fox_scan(small, fox_consts, B, S)
        o_fox = _attention(fq, fkt, fv, causal, aug, B, S)
        o_dil = _attention(dq, dkt, dv, dil_tab, None, B, S)
        o_ssd = _ssd(xbc, z, small, ssd_conv_w[l], ssd_conv_b[l][None, :], arow, dx,
                     ssd_norm_g[l][None, :], ssd_consts, B, S)

        kt, v = _kv(mem, mem_norm_g[l][None, :], xa_wkv[l].astype(BF16), xa_kn_g[l][None, :])
        x2, h3t, rinfo, cnt = _mid(x2d, o_fox, o_dil, o_ssd, w_out[l].astype(BF16),
                                   norm2_g[l][None, :], xa_wq[l].astype(BF16), xa_qn_g[l][None, :],
                                   kt, v, xa_wo[l].astype(BF16), norm3_g[l][None, :], wra, wrb, rb,
                                   tri_mid, S)

        counts = cnt[0, :N_EXPERTS].astype(jnp.int32)
        padded = (counts + TM_MOE - 1) // TM_MOE * TM_MOE
        ends = jnp.cumsum(padded)
        dest = _dest(rinfo, (ends - padded).astype(F32)[:, None])
        dest0, dest1 = dest[0], dest[1]
        first_blk = (jnp.concatenate([jnp.zeros((1,), jnp.int32), ends]) // TM_MOE).astype(jnp.int32)

        xs = _dispatch(dest0, dest1, h3t, xs)
        xs = _experts(first_blk, xs, exp_w1, exp_w3, exp_w2, l)
        x2d = _combine(dest0, dest1, x2, rinfo, xs)
    return x2d.reshape(B, S, D_MODEL)
```

```python
import functools

import jax
import jax.numpy as jnp
import numpy as np
from jax import lax
from jax.experimental import pallas as pl
from jax.experimental.pallas import tpu as pltpu

F32 = jnp.float32
BF16 = jnp.bfloat16

D_MODEL = 1024
HEAD_DIM = 64
ATT_HEADS = 4
ATT_W = ATT_HEADS * HEAD_DIM
SSD_HEADS = 8
SSD_INNER = 512
SSD_STATE = 128
SSD_CONV = 4
SSD_CHUNK = 128
SSD_CONV_CH = 1024
XA_HEADS = 4
XA_HEAD_DIM = 256
MEM_LEN = 256
N_GROUPS = 4
EXPERTS_PER_GROUP = 8
N_EXPERTS = 32
EXPERT_FF = 512
DIL_CONFIGS = ((128, 1), (512, 4), (2048, 16))
ROPE_THETA = 500000.0
ROPE_DIM = 16
EPS = 1e-6
NEG = -1e30
LOG2E = 1.4426950408889634

LANES = 128
N_FGATE = 4
DT_LANE0 = 4
MAIN_W = 3 * ATT_W + 3 * ATT_W + SSD_INNER + SSD_CONV_CH
IN_W = MAIN_W + LANES

TM_IN = 512
TQ = 256
TK = 256
TM_MID = 512
TM_MOE = 256
SSD_ROWS = 512
TD = 1024
TC = 1024
VMEM_LIMIT = 48 * 1024 * 1024


def _cparams(n_axes):
    return pltpu.CompilerParams(dimension_semantics=("arbitrary",) * n_axes,
                                vmem_limit_bytes=VMEM_LIMIT)


def _rms(x, g):
    return x * lax.rsqrt(jnp.mean(x * x, axis=-1, keepdims=True) + EPS) * g


def _split3(x):
    hi = x.astype(BF16)
    r = x - hi.astype(F32)
    mid = r.astype(BF16)
    lo = (r - mid.astype(F32)).astype(BF16)
    return jnp.concatenate([hi, mid, lo], axis=1)


def _dot(a, b):
    return jnp.dot(a, b, preferred_element_type=F32)


def _dot_nt(a, b):
    return lax.dot_general(a, b, (((1,), (1,)), ((), ())), preferred_element_type=F32)


def _sum3(c):
    w = c.shape[1] // 3
    return c[:, 0:w] + c[:, w:2 * w] + c[:, 2 * w:3 * w]


FF_COL = 3 * ATT_W
DQ_COL = FF_COL + N_FGATE
DT_COL = DQ_COL + MAIN_W - 3 * ATT_W
IN_SRC_W = DT_COL + 8
WPREP_ROWS = 256


def _wprep_kernel(w_ref, tail_ref, sh_ref, shs_ref, wa_ref, wb_ref, ws_ref):
    wa_ref[...] = w_ref[:, 0:FF_COL].astype(BF16)
    lane = lax.broadcasted_iota(jnp.int32, tail_ref.shape, 1)
    tail = jnp.where(lane < IN_SRC_W - MAIN_W, tail_ref[...], 0.0).astype(BF16)
    nb = (MAIN_W - FF_COL) // LANES
    for n in range(nb):
        lo = FF_COL + LANES * n
        if n + 1 < nb:
            pair = w_ref[:, lo:lo + 2 * LANES].astype(BF16)
        else:
            pair = jnp.concatenate([w_ref[:, lo:lo + LANES].astype(BF16), tail], axis=1)
        wb_ref[:, LANES * n:LANES * (n + 1)] = _dot(pair, sh_ref[...]).astype(BF16)
    small = jnp.concatenate([w_ref[:, FF_COL:FF_COL + LANES].astype(BF16), tail], axis=1)
    ws_ref[...] = _dot(small, shs_ref[...]).astype(BF16)


def _wprep(w_in, layer, sh, shs):
    row = lambda i: (i, 0)
    c2 = lambda i: (0, 0)
    wb_w = MAIN_W - FF_COL
    return pl.pallas_call(
        _wprep_kernel,
        grid=(D_MODEL // WPREP_ROWS,),
        in_specs=[pl.BlockSpec((None, WPREP_ROWS, MAIN_W), lambda i: (layer, i, 0)),
                  pl.BlockSpec((None, WPREP_ROWS, LANES), lambda i: (layer, i, MAIN_W // LANES)),
                  pl.BlockSpec(sh.shape, c2), pl.BlockSpec(shs.shape, c2)],
        out_specs=[pl.BlockSpec((WPREP_ROWS, FF_COL), row),
                   pl.BlockSpec((WPREP_ROWS, wb_w), row),
                   pl.BlockSpec((WPREP_ROWS, LANES), row)],
        out_shape=[jax.ShapeDtypeStruct((D_MODEL, FF_COL), BF16),
                   jax.ShapeDtypeStruct((D_MODEL, wb_w), BF16),
                   jax.ShapeDtypeStruct((D_MODEL, LANES), BF16)],
        compiler_params=_cparams(1),
        name="wprep",
    )(w_in, w_in, sh, shs)


def _wprep_consts():
    sh = np.zeros((2 * LANES, LANES), np.float32)
    shs = np.zeros((2 * LANES, LANES), np.float32)
    off = DQ_COL - FF_COL
    for j in range(LANES):
        sh[j + off, j] = 1.0
    for k in range(N_FGATE):
        shs[k, k] = 1.0
    for k in range(SSD_HEADS):
        shs[LANES + DT_COL - MAIN_W + k, DT_LANE0 + k] = 1.0
    return jnp.asarray(sh, BF16), jnp.asarray(shs, BF16)


def _inproj_kernel(x_ref, g_ref, wa_ref, wb_ref, ws_ref, gm_ref, qkg_ref, rope_ref, sb_ref,
                   fq_ref, fkt_ref, fv_ref, dq_ref, dkt_ref, dv_ref, z_ref, xbc_ref, sm_ref):
    h = _rms(x_ref[...], g_ref[...]).astype(BF16)
    na = wa_ref.shape[1]

    def proj(a, b):
        if b <= na:
            return _dot(h, wa_ref[:, a:b])
        return _dot(h, wb_ref[:, a - na:b - na])

    def head_norm(a, idx):
        ssq = _dot((a * a).astype(BF16), gm_ref[...])
        return a * lax.rsqrt(ssq * (1.0 / HEAD_DIM) + EPS) * qkg_ref[idx]

    def rope(a):
        return (a * rope_ref[0] + pltpu.roll(a, ATT_W - ROPE_DIM // 2, 1) * rope_ref[1]
                + pltpu.roll(a, ROPE_DIM // 2, 1) * rope_ref[2])

    scale = HEAD_DIM ** -0.5 * LOG2E
    fq_ref[...] = (head_norm(proj(0, 256), 0) * scale).astype(BF16)
    fkt_ref[0] = head_norm(proj(256, 512), 1).T.astype(BF16)
    fv_ref[...] = proj(512, 768).astype(BF16)
    dq_ref[...] = (rope(head_norm(proj(768, 1024), 2)) * scale).astype(BF16)
    dkt_ref[0] = rope(head_norm(proj(1024, 1280), 3)).T.astype(BF16)
    dv_ref[...] = proj(1280, 1536).astype(BF16)
    z_ref[...] = proj(1536, 2048)
    xbc_ref[...] = proj(2048, MAIN_W)
    v = _dot(h, ws_ref[...]) + sb_ref[...]
    e = jnp.log1p(jnp.exp(-jnp.abs(v)))
    lane = lax.broadcasted_iota(jnp.int32, v.shape, 1)
    sm_ref[...] = jnp.where(lane < N_FGATE, jnp.minimum(v, 0.0) - e, jnp.maximum(v, 0.0) + e)


def _inproj(x2d, g, wa, wb, ws, gm, qkg, rope, sb, B, S):
    T = x2d.shape[0]
    nst = S // TM_IN
    row = lambda i: (i, 0)
    const2 = lambda i: (0, 0)
    tr = lambda i: (i // nst, 0, i % nst)
    out_shape = [
        jax.ShapeDtypeStruct((T, ATT_W), BF16),
        jax.ShapeDtypeStruct((B, ATT_W, S), BF16),
        jax.ShapeDtypeStruct((T, ATT_W), BF16),
        jax.ShapeDtypeStruct((T, ATT_W), BF16),
        jax.ShapeDtypeStruct((B, ATT_W, S), BF16),
        jax.ShapeDtypeStruct((T, ATT_W), BF16),
        jax.ShapeDtypeStruct((T, SSD_INNER), F32),
        jax.ShapeDtypeStruct((T, SSD_CONV_CH), F32),
        jax.ShapeDtypeStruct((T, LANES), F32),
    ]
    att = pl.BlockSpec((TM_IN, ATT_W), row)
    att_t = pl.BlockSpec((1, ATT_W, TM_IN), tr)
    return pl.pallas_call(
        _inproj_kernel,
        grid=(T // TM_IN,),
        in_specs=[
            pl.BlockSpec((TM_IN, D_MODEL), row),
            pl.BlockSpec((1, D_MODEL), const2),
            pl.BlockSpec(wa.shape, const2),
            pl.BlockSpec(wb.shape, const2),
            pl.BlockSpec(ws.shape, const2),
            pl.BlockSpec((ATT_W, ATT_W), const2),
            pl.BlockSpec((4, 1, ATT_W), lambda i: (0, 0, 0)),
            pl.BlockSpec((3, TM_IN, ATT_W), lambda i: (0, i % nst, 0)),
            pl.BlockSpec((1, LANES), const2),
        ],
        out_specs=[att, att_t, att, att, att_t, att,
                   pl.BlockSpec((TM_IN, SSD_INNER), row),
                   pl.BlockSpec((TM_IN, SSD_CONV_CH), row),
                   pl.BlockSpec((TM_IN, LANES), row)],
        out_shape=out_shape,
        compiler_params=_cparams(1),
        name="inproj",
    )(x2d, g, wa, wb, ws, gm, qkg, rope, sb)


def _fox_scan_kernel(sm_ref, tri_ref, pq_ref, pk_ref, oq_ref, ok_ref, augq_ref, augkt_ref):
    S = sm_ref.shape[1]
    blk = tri_ref.shape[0]
    carry = jnp.zeros((1, LANES), F32)
    for b in range(S // blk):
        rows = slice(b * blk, (b + 1) * blk)
        c = _sum3(_dot(tri_ref[...], _split3(sm_ref[0, rows, :]))) + carry
        carry = c[blk - 1:blk, :]
        c3 = _split3(c * LOG2E)
        augq_ref[0, rows, :] = (_dot(c3, pq_ref[...]) + oq_ref[...]).astype(BF16)
        ak = _dot(c3, pk_ref[...]) + ok_ref[...]
        for h in range(ATT_HEADS):
            cols = slice(LANES * h, LANES * (h + 1))
            augkt_ref[0, cols, rows] = ak[:, cols].T.astype(BF16)


def _fox_scan(small, consts, B, S):
    tri, pq, pk, oq, ok = consts
    c2 = lambda b: (0, 0)
    return pl.pallas_call(
        _fox_scan_kernel,
        grid=(B,),
        in_specs=[
            pl.BlockSpec((1, S, LANES), lambda b: (b, 0, 0)),
            pl.BlockSpec(tri.shape, c2),
            pl.BlockSpec(pq.shape, c2),
            pl.BlockSpec(pk.shape, c2),
            pl.BlockSpec(oq.shape, c2),
            pl.BlockSpec(ok.shape, c2),
        ],
        out_specs=[pl.BlockSpec((1, S, ATT_HEADS * LANES), lambda b: (b, 0, 0)),
                   pl.BlockSpec((1, ATT_HEADS * LANES, S), lambda b: (b, 0, 0))],
        out_shape=[jax.ShapeDtypeStruct((B, S, ATT_HEADS * LANES), BF16),
                   jax.ShapeDtypeStruct((B, ATT_HEADS * LANES, S), BF16)],
        compiler_params=_cparams(1),
        name="fox_scan",
    )(small.reshape(B, S, LANES), tri, pq, pk, oq, ok)


def _attn_kernel(*refs, fox, nk):
    if fox:
        q_ref, kt_ref, v_ref, lm_ref, augq_ref, augkt_ref, o_ref, kt_scr, v_scr = refs
    else:
        q_ref, kt_ref, v_ref, lm_ref, o_ref, kt_scr, v_scr = refs
    qi = pl.program_id(1)

    @pl.when(qi == 0)
    def _prep():
        row = lax.broadcasted_iota(jnp.int32, (LANES, TK), 0)
        lane = lax.broadcasted_iota(jnp.int32, (TK, LANES), 1)
        for h in range(ATT_HEADS):
            p, mem = divmod(h, 2)
            pair = slice(LANES * p, LANES * (p + 1))
            for j in range(nk):
                keys = slice(j * TK, (j + 1) * TK)
                kd = kt_ref[0, pair, keys]
                if fox:
                    other = augkt_ref[0, LANES * h:LANES * (h + 1), keys]
                else:
                    other = jnp.zeros_like(kd)
                kt_scr[h, j] = jnp.where((row >> 6) == mem, kd, other)
                vd = v_ref[keys, pair]
                v_scr[h, j] = jnp.where((lane >> 6) == mem, vd, jnp.ones_like(vd))

    qlane = lax.broadcasted_iota(jnp.int32, (TQ, LANES), 1)
    qas = []
    for h in range(ATT_HEADS):
        p, mem = divmod(h, 2)
        qd = q_ref[:, LANES * p:LANES * (p + 1)]
        if fox:
            other = augq_ref[:, LANES * h:LANES * (h + 1)]
        else:
            other = jnp.zeros_like(qd)
        qas.append(jnp.where((qlane >> 6) == mem, qd, other))

    def step(j, carry, table):
        new = []
        for h in range(ATT_HEADS):
            m, acc = carry[h]
            s = _dot(qas[h], kt_scr[h, j])
            if table is not None:
                s = s + table
            m_new = jnp.maximum(m, jnp.max(s, axis=1, keepdims=True))
            alpha = jnp.exp2(m - m_new)
            pr = jnp.exp2(s - m_new).astype(BF16)
            new.append((m_new, alpha * acc + _dot(pr, v_scr[h, j])))
        return tuple(new)

    init = tuple((jnp.full((TQ, 1), NEG, F32), jnp.zeros((TQ, LANES), F32))
                 for _ in range(ATT_HEADS))

    if fox:
        table = lambda j: lm_ref[jnp.minimum(qi - j, 1)]
    else:
        table = lambda j: lm_ref[qi - j]

    def steps(j0, count, c):
        for u in range(count):
            c = step(j0 + u, c, table(j0 + u))
        return c

    n = qi + 1
    carry = lax.fori_loop(0, n // 4, lambda p, c: steps(4 * p, 4, c), init)
    carry = lax.cond(n % 4 >= 2, lambda c: steps(n - n % 4, 2, c), lambda c: c, carry)
    carry = lax.cond(n % 2 == 1, lambda c: steps(n - 1, 1, c), lambda c: c, carry)
    outs = [acc / pltpu.roll(acc, HEAD_DIM, 1) for _, acc in carry]
    for p in range(ATT_HEADS // 2):
        o_ref[:, LANES * p:LANES * (p + 1)] = jnp.where(
            (qlane >> 6) == 0, outs[2 * p], outs[2 * p + 1]).astype(BF16)


def _attention(q, kt, v, lm, aug, B, S):
    fox = aug is not None
    nq, nk = S // TQ, S // TK
    in_specs = [
        pl.BlockSpec((TQ, ATT_W), lambda b, i: (b * nq + i, 0)),
        pl.BlockSpec((1, ATT_W, S), lambda b, i: (b, 0, 0)),
        pl.BlockSpec((S, ATT_W), lambda b, i: (b, 0)),
        pl.BlockSpec(lm.shape, lambda b, i: (0, 0, 0)),
    ]
    args = [q, kt, v, lm]
    if fox:
        in_specs += [pl.BlockSpec((TQ, ATT_HEADS * LANES), lambda b, i: (b * nq + i, 0)),
                     pl.BlockSpec((1, ATT_HEADS * LANES, S), lambda b, i: (b, 0, 0))]
        args += [aug[0].reshape(B * S, ATT_HEADS * LANES), aug[1]]
    return pl.pallas_call(
        functools.partial(_attn_kernel, fox=fox, nk=nk),
        grid=(B, nq),
        in_specs=in_specs,
        out_specs=pl.BlockSpec((TQ, ATT_W), lambda b, i: (b * nq + i, 0)),
        out_shape=jax.ShapeDtypeStruct((B * S, ATT_W), BF16),
        scratch_shapes=[pltpu.VMEM((ATT_HEADS, nk, LANES, TK), BF16),
                        pltpu.VMEM((ATT_HEADS, nk, TK, LANES), BF16)],
        compiler_params=_cparams(2),
        name="fox_attn" if fox else "dil_attn",
    )(*args)


def _ssd_kernel(xbc_ref, z_ref, sm_ref, cw_ref, cb_ref, arow_ref, dx_ref, ng_ref,
                tri_ref, pexp_ref, pq_ref, pk_ref, oq_ref, ok_ref, o_ref, buf, state):
    rows = xbc_ref.shape[0]

    @pl.when(pl.program_id(1) == 0)
    def _reset():
        buf[0:8, :] = jnp.zeros((8, SSD_CONV_CH), F32)
        state[...] = jnp.zeros(state.shape, F32)

    xb = xbc_ref[...]
    buf[8:8 + rows, :] = xb
    conv = cb_ref[...]
    for k in range(SSD_CONV):
        off = 8 - (SSD_CONV - 1) + k
        conv = conv + cw_ref[k:k + 1, :] * buf[off:off + rows, :]
    buf[0:8, :] = xb[rows - 8:rows, :]
    act = conv * jax.nn.sigmoid(conv)

    for c in range(rows // SSD_CHUNK):
        rs = slice(SSD_CHUNK * c, SSD_CHUNK * (c + 1))
        _ssd_chunk(act[rs, :], z_ref[rs, :], sm_ref[rs, :], arow_ref, dx_ref, ng_ref, tri_ref, pexp_ref,
                   pq_ref, pk_ref, oq_ref, ok_ref, o_ref.at[rs], state)


def _ssd_chunk(act, zz, dt, arow_ref, dx_ref, ng_ref, tri_ref, pexp_ref, pq_ref, pk_ref, oq_ref, ok_ref,
               o_ref, state):
    Q = SSD_CHUNK
    xs = act[:, 0:SSD_INNER]
    bm = act[:, SSD_INNER:SSD_INNER + 2 * SSD_STATE]
    cm = act[:, SSD_INNER + 2 * SSD_STATE:]

    acs = _sum3(_dot(tri_ref[...], _split3(dt * arow_ref[...])))
    acs3 = _split3(acs)
    ax = _dot(acs3, pexp_ref[...])
    dtx = _dot(_split3(dt), pexp_ref[...])
    last = ax[Q - 1:Q, :]
    ea = jnp.exp(ax)
    cdec = jnp.exp(last)
    xc = xs * dtx
    xcb = xc.astype(BF16)
    xcd = (xc * jnp.exp(last - ax)).astype(BF16)
    uq = (_dot(acs3, pq_ref[...]) + oq_ref[...]).astype(BF16)
    uk = (_dot(acs3, pk_ref[...]) + ok_ref[...]).astype(BF16)

    tril = (lax.broadcasted_iota(jnp.int32, (Q, Q), 0) >= lax.broadcasted_iota(jnp.int32, (Q, Q), 1))
    first = lax.broadcasted_iota(jnp.int32, (Q, LANES), 1) < HEAD_DIM
    ys = []
    for g in range(2):
        gs = slice(SSD_STATE * g, SSD_STATE * (g + 1))
        bg = bm[:, gs]
        cg = cm[:, gs].astype(BF16)
        cbm = _dot_nt(cg, bg.astype(BF16))
        bgt = bg.T.astype(BF16)
        for pp in range(2):
            p = 2 * g + pp
            ps = slice(LANES * p, LANES * (p + 1))
            ms = []
            for mem in range(2):
                hs = slice(LANES * (2 * p + mem), LANES * (2 * p + mem + 1))
                dm = _dot_nt(uq[:, hs], uk[:, hs])
                ms.append((cbm * jnp.exp(jnp.where(tril, dm, NEG))).astype(BF16))
            xp = xcb[:, ps]
            zero = jnp.zeros_like(xp)
            xcat = jnp.concatenate([jnp.where(first, xp, zero), jnp.where(first, zero, xp)], axis=0)
            y_diag = _dot(jnp.concatenate(ms, axis=1), xcat)
            st = state[p]
            y_off = _dot(cg, st.astype(BF16)) * ea[:, ps]
            state[p] = cdec[:, ps] * st + _dot(bgt, xcd[:, ps])
            ys.append(y_diag + y_off + xs[:, ps] * dx_ref[:, ps])
    y = jnp.concatenate(ys, axis=1)
    o_ref[...] = _rms(y * (zz * jax.nn.sigmoid(zz)), ng_ref[...]).astype(BF16)


def _ssd(xbc, z, small, cw, cb, arow, dx, ng, consts, B, S):
    nc = S // SSD_ROWS
    row = lambda b, c: (b * nc + c, 0)
    c2 = lambda b, c: (0, 0)
    full = lambda a: pl.BlockSpec(a.shape, c2)
    return pl.pallas_call(
        _ssd_kernel,
        grid=(B, nc),
        in_specs=[pl.BlockSpec((SSD_ROWS, SSD_CONV_CH), row),
                  pl.BlockSpec((SSD_ROWS, SSD_INNER), row),
                  pl.BlockSpec((SSD_ROWS, LANES), row),
                  full(cw), full(cb), full(arow), full(dx), full(ng)] + [full(a) for a in consts],
        out_specs=pl.BlockSpec((SSD_ROWS, SSD_INNER), row),
        out_shape=jax.ShapeDtypeStruct((B * S, SSD_INNER), BF16),
        scratch_shapes=[pltpu.VMEM((8 + SSD_ROWS, SSD_CONV_CH), F32),
                        pltpu.VMEM((SSD_HEADS // 2, SSD_STATE, LANES), F32)],
        compiler_params=_cparams(2),
        name="ssd",
    )(xbc, z, small, cw, cb, arow, dx, ng, *consts)


def _kv_kernel(mem_ref, g_ref, w_ref, kg_ref, kt_ref, v_ref):
    m = _rms(mem_ref[0], g_ref[...]).astype(BF16)
    kv = _dot(m, w_ref[...])
    for h in range(XA_HEADS):
        hs = slice(XA_HEAD_DIM * h, XA_HEAD_DIM * (h + 1))
        kt_ref[0, hs, :] = _rms(kv[:, hs], kg_ref[...]).T.astype(BF16)
    v_ref[0] = kv[:, D_MODEL:].astype(BF16)


def _kv(mem, g, w, kg):
    B = mem.shape[0]
    c2 = lambda b: (0, 0)
    return pl.pallas_call(
        _kv_kernel,
        grid=(B,),
        in_specs=[pl.BlockSpec((1, MEM_LEN, D_MODEL), lambda b: (b, 0, 0)),
                  pl.BlockSpec((1, D_MODEL), c2),
                  pl.BlockSpec((D_MODEL, 2 * D_MODEL), c2),
                  pl.BlockSpec((1, XA_HEAD_DIM), c2)],
        out_specs=[pl.BlockSpec((1, D_MODEL, MEM_LEN), lambda b: (b, 0, 0)),
                   pl.BlockSpec((1, MEM_LEN, D_MODEL), lambda b: (b, 0, 0))],
        out_shape=[jax.ShapeDtypeStruct((B, D_MODEL, MEM_LEN), BF16),
                   jax.ShapeDtypeStruct((B, MEM_LEN, D_MODEL), BF16)],
        compiler_params=_cparams(1),
        name="mem_kv",
    )(mem, g, w, kg)


ROW_SUB = D_MODEL // LANES


def _chunk(n, c):
    return pl.ds(c, n, stride=ROW_SUB)


def _to_row_tiles(ref, x):
    n = x.shape[0]
    for c in range(ROW_SUB):
        ref[_chunk(n, c), :] = x[:, LANES * c:LANES * (c + 1)]


def _from_row_tiles(ref):
    n = ref.shape[0] // ROW_SUB
    return jnp.concatenate([ref[_chunk(n, c), :] for c in range(ROW_SUB)], axis=1)


def _row_copy(src, dst, sem, src_row, dst_row):
    return pltpu.make_async_copy(
        src.at[pl.ds(pl.multiple_of(src_row * ROW_SUB, ROW_SUB), ROW_SUB)],
        dst.at[pl.ds(pl.multiple_of(dst_row * ROW_SUB, ROW_SUB), ROW_SUB)], sem)


def _mid_kernel(x_ref, of_ref, od_ref, os_ref, wo_ref, g2_ref, wq_ref, qg_ref, kt_ref, v_ref,
                wxo_ref, g3_ref, wra_ref, wrb_ref, rb_ref, tri_ref,
                x2_ref, h3_ref, ri_ref, cnt_ref, run):
    @pl.when(pl.program_id(0) == 0)
    def _reset():
        run[...] = jnp.zeros(run.shape, F32)

    x1 = (x_ref[...] + _dot(of_ref[...], wo_ref[0:ATT_W, :])
          + _dot(od_ref[...], wo_ref[ATT_W:2 * ATT_W, :])
          + _dot(os_ref[...], wo_ref[2 * ATT_W:, :]))

    q = _dot(_rms(x1, g2_ref[...]).astype(BF16), wq_ref[...])
    heads = []
    for h in range(XA_HEADS):
        hs = slice(XA_HEAD_DIM * h, XA_HEAD_DIM * (h + 1))
        qn = (_rms(q[:, hs], qg_ref[...]) * XA_HEAD_DIM ** -0.5).astype(BF16)
        s = _dot(qn, kt_ref[0, hs, :])
        e = jnp.exp(s - jnp.max(s, axis=1, keepdims=True))
        o = _dot(e.astype(BF16), v_ref[0, :, hs]) / jnp.sum(e, axis=1, keepdims=True)
        heads.append(o.astype(BF16))
    x2 = x1 + _dot(jnp.concatenate(heads, axis=1), wxo_ref[...])
    x2_ref[...] = x2

    h3 = _rms(x2, g3_ref[...])
    _to_row_tiles(h3_ref, h3)

    hi = h3.astype(BF16)
    mid = (h3 - hi.astype(F32)).astype(BF16)
    both = _dot(hi, wra_ref[...])
    logits = _dot(mid, wrb_ref[...]) + both[:, LANES:] + both[:, 0:LANES] + rb_ref[...]

    lane = lax.broadcasted_iota(jnp.int32, logits.shape, 1)
    lanef = lane.astype(F32)
    big = float(LANES)

    def first_max(vals):
        top = jnp.max(vals, axis=1, keepdims=True)
        return top, jnp.min(jnp.where(vals == top, lanef, big), axis=1, keepdims=True)

    gl = jnp.where(lane < N_GROUPS, logits, NEG)
    gmax, gsel = first_max(gl)
    ggate = 1.0 / jnp.sum(jnp.exp(gl - gmax), axis=1, keepdims=True)
    grp = ((lane - N_GROUPS) >> 3).astype(F32)
    el = jnp.where(grp == gsel, logits, NEG)
    v1, i1 = first_max(el)
    v2, i2 = first_max(jnp.where(lanef == i1, NEG, el))
    t = jnp.exp(v2 - v1)
    p1 = 1.0 / (1.0 + t)
    e1 = i1 - N_GROUPS
    e2 = i2 - N_GROUPS

    hit1 = lanef == e1
    hit2 = lanef == e2
    onehot = jnp.where(hit1 | hit2, 1.0, 0.0)
    before = _dot(tri_ref[...], onehot.astype(BF16)) + run[...]
    r1 = jnp.sum(jnp.where(hit1, before, 0.0), axis=1, keepdims=True)
    r2 = jnp.sum(jnp.where(hit2, before, 0.0), axis=1, keepdims=True)
    run[...] = run[...] + jnp.sum(onehot, axis=0, keepdims=True)

    cols = (e1, e2, p1 * ggate, t * p1 * ggate, r1, r2)
    info = jnp.zeros(logits.shape, F32)
    for k, col in enumerate(cols):
        info = jnp.where(lane == k, col, info)
    ri_ref[...] = info
    cnt_ref[...] = jnp.broadcast_to(run[...], cnt_ref.shape)


def _mid(x2d, o_fox, o_dil, o_ssd, wo, g2, wq, qg, kt, v, wxo, g3, wra, wrb, rb, tri, S):
    T = x2d.shape[0]
    npb = S // TM_MID
    row = lambda i: (i, 0)
    c2 = lambda i: (0, 0)
    return pl.pallas_call(
        _mid_kernel,
        grid=(T // TM_MID,),
        in_specs=[pl.BlockSpec((TM_MID, D_MODEL), row),
                  pl.BlockSpec((TM_MID, ATT_W), row),
                  pl.BlockSpec((TM_MID, ATT_W), row),
                  pl.BlockSpec((TM_MID, SSD_INNER), row),
                  pl.BlockSpec((D_MODEL, D_MODEL), c2),
                  pl.BlockSpec((1, D_MODEL), c2),
                  pl.BlockSpec((D_MODEL, D_MODEL), c2),
                  pl.BlockSpec((1, XA_HEAD_DIM), c2),
                  pl.BlockSpec((1, D_MODEL, MEM_LEN), lambda i: (i // npb, 0, 0)),
                  pl.BlockSpec((1, MEM_LEN, D_MODEL), lambda i: (i // npb, 0, 0)),
                  pl.BlockSpec((D_MODEL, D_MODEL), c2),
                  pl.BlockSpec((1, D_MODEL), c2),
                  pl.BlockSpec((D_MODEL, 2 * LANES), c2),
                  pl.BlockSpec((D_MODEL, LANES), c2),
                  pl.BlockSpec((1, LANES), c2),
                  pl.BlockSpec((TM_MID, TM_MID), c2)],
        out_specs=[pl.BlockSpec((TM_MID, D_MODEL), row),
                   pl.BlockSpec((TM_MID * ROW_SUB, LANES), row),
                   pl.BlockSpec((TM_MID, LANES), row),
                   pl.BlockSpec((8, LANES), c2)],
        out_shape=[jax.ShapeDtypeStruct((T, D_MODEL), F32),
                   jax.ShapeDtypeStruct((T * ROW_SUB, LANES), F32),
                   jax.ShapeDtypeStruct((T, LANES), F32),
                   jax.ShapeDtypeStruct((8, LANES), F32)],
        scratch_shapes=[pltpu.VMEM((1, LANES), F32)],
        compiler_params=_cparams(1),
        name="mid",
    )(x2d, o_fox, o_dil, o_ssd, wo, g2, wq, qg, kt, v, wxo, g3, wra, wrb, rb, tri)


def _dest_kernel(ri_ref, st_ref, o_ref):
    info = ri_ref[...].T
    n = info.shape[1]
    expert = lax.broadcasted_iota(jnp.int32, (N_EXPERTS, n), 0).astype(F32)
    row = lax.broadcasted_iota(jnp.int32, (8, n), 0)
    out = jnp.zeros((8, n), F32)
    for k in range(2):
        start = jnp.sum(jnp.where(expert == info[k:k + 1, :], st_ref[...], 0.0), axis=0, keepdims=True)
        out = jnp.where(row == k, start + info[4 + k:5 + k, :], out)
    o_ref[...] = out.astype(jnp.int32)


def _dest(rinfo, starts):
    T = rinfo.shape[0]
    return pl.pallas_call(
        _dest_kernel,
        grid=(T // TD,),
        in_specs=[pl.BlockSpec((TD, LANES), lambda i: (i, 0)),
                  pl.BlockSpec((N_EXPERTS, 1), lambda i: (0, 0))],
        out_specs=pl.BlockSpec((8, TD), lambda i: (0, i)),
        out_shape=jax.ShapeDtypeStruct((8, T), jnp.int32),
        compiler_params=_cparams(1),
        name="moe_dest",
    )(rinfo, starts)


def _dispatch_kernel(d0_ref, d1_ref, h_ref, xs_in, xs_out, sem):
    del xs_in

    def issue(r, carry):
        _row_copy(h_ref, xs_out, sem, r, d0_ref[r]).start(priority=0)
        _row_copy(h_ref, xs_out, sem, r, d1_ref[r]).start(priority=1)
        return carry

    def drain(r, carry):
        _row_copy(h_ref, xs_out, sem, 0, 0).wait()
        _row_copy(h_ref, xs_out, sem, 0, 0).wait()
        return carry

    lax.fori_loop(0, TD, issue, 0, unroll=8)
    lax.fori_loop(0, TD, drain, 0, unroll=8)


def _dispatch(dest0, dest1, h3t, xs_init):
    T = h3t.shape[0] // ROW_SUB
    idx = pl.BlockSpec((TD,), lambda i: (i,), memory_space=pltpu.SMEM)
    return pl.pallas_call(
        _dispatch_kernel,
        grid=(T // TD,),
        in_specs=[idx, idx,
                  pl.BlockSpec((TD * ROW_SUB, LANES), lambda i: (i, 0)),
                  pl.BlockSpec(memory_space=pl.ANY)],
        out_specs=pl.BlockSpec(memory_space=pl.ANY),
        out_shape=jax.ShapeDtypeStruct(xs_init.shape, F32),
        scratch_shapes=[pltpu.SemaphoreType.DMA],
        input_output_aliases={3: 0},
        compiler_params=_cparams(1),
        name="moe_dispatch",
    )(dest0, dest1, h3t, xs_init)


def _expert_kernel(fb_ref, xs_hbm, w1_ref, w3_ref, w2_ref, y_hbm, w1s, w3s, w2s, xin, yout, isem, osem):
    e = pl.program_id(0)
    n_used = fb_ref[N_EXPERTS]
    blk = TM_MOE * ROW_SUB

    def rows(g):
        return pl.ds(pl.multiple_of(g * blk, blk), blk)

    def in_copy(g, slot):
        return pltpu.make_async_copy(xs_hbm.at[rows(g)], xin.at[slot], isem.at[slot])

    def out_copy(g, slot):
        return pltpu.make_async_copy(yout.at[slot], y_hbm.at[rows(g)], osem.at[slot])

    @pl.when((e == 0) & (n_used > 0))
    def _prime():
        in_copy(0, 0).start()

    w1s[...] = w1_ref[...].astype(BF16)
    w3s[...] = w3_ref[...].astype(BF16)
    w2s[...] = w2_ref[...].astype(BF16)

    def block(g, carry):
        slot = g % 2

        @pl.when(g + 1 < n_used)
        def _prefetch():
            in_copy(g + 1, 1 - slot).start()

        in_copy(g, slot).wait()

        @pl.when(g >= 2)
        def _free_out_slot():
            out_copy(g - 2, slot).wait()

        xb = _from_row_tiles(xin.at[slot]).astype(BF16)
        a = _dot(xb, w1s[...])
        b = _dot(xb, w3s[...])
        _to_row_tiles(yout.at[slot], _dot((a * jax.nn.sigmoid(a) * b).astype(BF16), w2s[...]))
        out_copy(g, slot).start()
        return carry

    lax.fori_loop(fb_ref[e], fb_ref[e + 1], block, 0)

    @pl.when(e == pl.num_programs(0) - 1)
    def _drain():
        for back in (2, 1):
            @pl.when(n_used >= back)
            def _wait():
                out_copy(n_used - back, (n_used - back) % 2).wait()


def _experts(first_blk, xs, w1, w3, w2, layer):
    wmap = lambda e, fb: (layer, e, 0, 0)
    blk = TM_MOE * ROW_SUB
    return pl.pallas_call(
        _expert_kernel,
        grid_spec=pltpu.PrefetchScalarGridSpec(
            num_scalar_prefetch=1,
            grid=(N_EXPERTS,),
            in_specs=[pl.BlockSpec(memory_space=pl.ANY),
                      pl.BlockSpec((None, None, D_MODEL, EXPERT_FF), wmap),
                      pl.BlockSpec((None, None, D_MODEL, EXPERT_FF), wmap),
                      pl.BlockSpec((None, None, EXPERT_FF, D_MODEL), wmap)],
            out_specs=pl.BlockSpec(memory_space=pl.ANY),
            scratch_shapes=[pltpu.VMEM((D_MODEL, EXPERT_FF), BF16),
                            pltpu.VMEM((D_MODEL, EXPERT_FF), BF16),
                            pltpu.VMEM((EXPERT_FF, D_MODEL), BF16),
                            pltpu.VMEM((2, blk, LANES), F32),
                            pltpu.VMEM((2, blk, LANES), F32),
                            pltpu.SemaphoreType.DMA((2,)),
                            pltpu.SemaphoreType.DMA((2,))]),
        out_shape=jax.ShapeDtypeStruct(xs.shape, F32),
        input_output_aliases={1: 0},
        compiler_params=_cparams(1),
        name="moe_experts",
    )(first_blk, xs, w1, w3, w2)


def _combine_kernel(d0_ref, d1_ref, d0n_ref, d1n_ref, x2_ref, ri_ref, y_hbm, o_ref, buf, sem):
    i = pl.program_id(0)
    slot = i % 2

    def gather(d0, d1, s):
        def issue(r, carry):
            _row_copy(y_hbm, buf.at[s, 0], sem.at[s], d0[r], r).start(priority=0)
            _row_copy(y_hbm, buf.at[s, 1], sem.at[s], d1[r], r).start(priority=1)
            return carry
        lax.fori_loop(0, TC, issue, 0, unroll=8)

    @pl.when(i == 0)
    def _first():
        gather(d0_ref, d1_ref, 0)

    @pl.when(i + 1 < pl.num_programs(0))
    def _next():
        gather(d0n_ref, d1n_ref, 1 - slot)

    def drain(r, carry):
        _row_copy(y_hbm, buf.at[slot, 0], sem.at[slot], 0, 0).wait()
        _row_copy(y_hbm, buf.at[slot, 1], sem.at[slot], 0, 0).wait()
        return carry

    lax.fori_loop(0, TC, drain, 0, unroll=8)
    info = ri_ref[...]
    g0 = info[:, 2:3]
    g1 = info[:, 3:4]
    for c in range(D_MODEL // LANES):
        cols = slice(LANES * c, LANES * (c + 1))
        o_ref[:, cols] = (x2_ref[:, cols] + g0 * buf[slot, 0, _chunk(TC, c), :]
                          + g1 * buf[slot, 1, _chunk(TC, c), :])


def _combine(dest0, dest1, x2, rinfo, ybuf):
    T = x2.shape[0]
    row = lambda i: (i, 0)
    nsteps = T // TC
    idx = pl.BlockSpec((TC,), lambda i: (i,), memory_space=pltpu.SMEM)
    nxt = pl.BlockSpec((TC,), lambda i: (jnp.minimum(i + 1, nsteps - 1),), memory_space=pltpu.SMEM)
    return pl.pallas_call(
        _combine_kernel,
        grid=(nsteps,),
        in_specs=[idx, idx, nxt, nxt,
                  pl.BlockSpec((TC, D_MODEL), row),
                  pl.BlockSpec((TC, LANES), row),
                  pl.BlockSpec(memory_space=pl.ANY)],
        out_specs=pl.BlockSpec((TC, D_MODEL), row),
        out_shape=jax.ShapeDtypeStruct((T, D_MODEL), F32),
        scratch_shapes=[pltpu.VMEM((2, 2, TC * ROW_SUB, LANES), F32),
                        pltpu.SemaphoreType.DMA((2,))],
        compiler_params=_cparams(1),
        name="moe_combine",
    )(dest0, dest1, dest0, dest1, x2, rinfo, ybuf)


def _tri(n, strict):
    return jnp.asarray(np.tril(np.ones((n, n), np.float32), -1 if strict else 0), BF16)


def _rope_tables(S):
    half = ROPE_DIM // 2
    inv = jnp.power(ROPE_THETA, -2.0 * jnp.arange(half, dtype=F32) / ROPE_DIM)
    ang = jnp.arange(S).astype(F32)[:, None] * inv[None, :]
    cos, sin = jnp.cos(ang), jnp.sin(ang)
    d = np.arange(ATT_W) % HEAD_DIM
    idx = d % half
    c = jnp.where(d < ROPE_DIM, cos[:, idx], 1.0)
    s1 = jnp.where(d < half, -sin[:, idx], 0.0)
    s2 = jnp.where((d >= half) & (d < ROPE_DIM), sin[:, idx], 0.0)
    return jnp.stack([c, s1, s2]).astype(F32)


def _fox_consts():
    pq = np.zeros((3 * LANES, ATT_HEADS * LANES), np.float32)
    pk = np.zeros_like(pq)
    oq = np.zeros((1, ATT_HEADS * LANES), np.float32)
    ok = np.zeros_like(oq)
    for h in range(ATT_HEADS):
        off = LANES * h + (HEAD_DIM if h % 2 == 0 else 0)
        for k in range(3):
            pq[k * LANES + h, off + k] = 1.0
            oq[0, off + 3 + k] = 1.0
            pk[k * LANES + h, off + 3 + k] = -1.0
            ok[0, off + k] = 1.0
    return (_tri(256, False), jnp.asarray(pq, BF16), jnp.asarray(pk, BF16),
            jnp.asarray(oq), jnp.asarray(ok))


def _ssd_consts():
    pexp = np.zeros((3 * LANES, SSD_INNER), np.float32)
    pq = np.zeros((3 * LANES, SSD_HEADS * LANES), np.float32)
    pk = np.zeros_like(pq)
    oq = np.zeros((1, SSD_HEADS * LANES), np.float32)
    ok = np.zeros_like(oq)
    for h in range(SSD_HEADS):
        for k in range(3):
            src = k * LANES + DT_LANE0 + h
            pexp[src, HEAD_DIM * h:HEAD_DIM * (h + 1)] = 1.0
            pq[src, LANES * h + k] = 1.0
            oq[0, LANES * h + 3 + k] = 1.0
            pk[src, LANES * h + 3 + k] = -1.0
            ok[0, LANES * h + k] = 1.0
    return (_tri(SSD_CHUNK, False), jnp.asarray(pexp, BF16), jnp.asarray(pq, BF16),
            jnp.asarray(pk, BF16), jnp.asarray(oq), jnp.asarray(ok))


def _score_tables(S):
    nd = S // TK
    i = np.arange(TQ)[:, None]
    j = np.arange(TK)[None, :]
    causal = np.stack([np.where(i >= j, 0.0, NEG), np.zeros((TQ, TK))]).astype(np.float32)
    dil = np.zeros((nd, TQ, TK), np.float32)
    for d in range(nd):
        delta = d * TK + i - j
        mult = np.zeros((TQ, TK), np.float64)
        for window, step in DIL_CONFIGS:
            mult += (delta >= 0) & (delta <= window) & (delta % step == 0)
        dil[d] = np.where(mult > 0, np.log2(np.maximum(mult, 1.0)), NEG)
    return jnp.asarray(causal), jnp.asarray(dil)


def _group_matrix():
    g = np.arange(ATT_W) // HEAD_DIM
    return jnp.asarray((g[:, None] == g[None, :]).astype(np.float32), BF16)


def _pad_lanes(v, lane0):
    return jnp.zeros((1, LANES), F32).at[0, lane0:lane0 + v.shape[0]].set(v)


def _layer_params(l, w_in, wprep_consts, fox_fgate_b, fox_qn_g, fox_kn_g, dil_qn_g, dil_kn_g,
                  ssd_dt_bias, ssd_A_log, ssd_D, router_wg, router_bg, router_we, router_be):
    w_r = _wprep(w_in, l, *wprep_consts)
    tile4 = lambda g: jnp.tile(g, ATT_HEADS)[None, :]
    qkg = jnp.stack([tile4(fox_qn_g[l]), tile4(fox_kn_g[l]), tile4(dil_qn_g[l]), tile4(dil_kn_g[l])])
    sb = _pad_lanes(fox_fgate_b[l], 0) + _pad_lanes(ssd_dt_bias[l], DT_LANE0)
    arow = _pad_lanes(-jnp.exp(ssd_A_log[l]), DT_LANE0)
    dx = jnp.repeat(ssd_D[l], HEAD_DIM)[None, :]
    wr = jnp.concatenate([router_wg[l], router_we[l],
                          jnp.zeros((D_MODEL, LANES - N_GROUPS - N_EXPERTS), F32)], axis=1)
    wr3 = _split3(wr)
    rb = _pad_lanes(router_bg[l], 0) + _pad_lanes(router_be[l], N_GROUPS)
    return w_r, qkg, sb, arow, dx, wr3[:, 0:2 * LANES], wr3[:, 0:LANES], rb


def kernel(x, mem, norm1_g, w_in, fox_fgate_b, fox_qn_g, fox_kn_g, dil_qn_g, dil_kn_g, ssd_conv_w,
           ssd_conv_b, ssd_dt_bias, ssd_A_log, ssd_D, ssd_norm_g, w_out, norm2_g, mem_norm_g, xa_wq,
           xa_wkv, xa_qn_g, xa_kn_g, xa_wo, norm3_g, router_wg, router_bg, router_we, router_be,
           exp_w1, exp_w3, exp_w2):
    B, S, _ = x.shape
    T = B * S
    depth = w_in.shape[0]
    assert S % TM_IN == 0 and S % TQ == 0 and T % TD == 0 and S >= DIL_CONFIGS[-1][0]

    rope = _rope_tables(S)
    fox_consts = _fox_consts()
    ssd_consts = _ssd_consts()
    causal, dil_tab = _score_tables(S)
    gm = _group_matrix()
    wprep_consts = _wprep_consts()
    tri_mid = _tri(TM_MID, True)
    nblk = (2 * T) // TM_MOE + N_EXPERTS

    x2d = x.reshape(T, D_MODEL)
    xs = jnp.zeros((nblk * TM_MOE * ROW_SUB, LANES), F32)
    for l in range(depth):
        w_r, qkg, sb, arow, dx, wra, wrb, rb = _layer_params(
            l, w_in, wprep_consts, fox_fgate_b, fox_qn_g, fox_kn_g, dil_qn_g, dil_kn_g,
            ssd_dt_bias, ssd_A_log, ssd_D, router_wg, router_bg, router_we, router_be)

        fq, fkt, fv, dq, dkt, dv, z, xbc, small = _inproj(
            x2d, norm1_g[l][None, :], *w_r, gm, qkg, rope, sb, B, S)
        aug = _fox_scan(small, fox_consts, B, S)
        o_fox = _attention(fq, fkt, fv, causal, aug, B, S)
        o_dil = _attention(dq, dkt, dv, dil_tab, None, B, S)
        o_ssd = _ssd(xbc, z, small, ssd_conv_w[l], ssd_conv_b[l][None, :], arow, dx,
                     ssd_norm_g[l][None, :], ssd_consts, B, S)

        kt, v = _kv(mem, mem_norm_g[l][None, :], xa_wkv[l].astype(BF16), xa_kn_g[l][None, :])
        x2, h3t, rinfo, cnt = _mid(x2d, o_fox, o_dil, o_ssd, w_out[l].astype(BF16),
                                   norm2_g[l][None, :], xa_wq[l].astype(BF16), xa_qn_g[l][None, :],
                                   kt, v, xa_wo[l].astype(BF16), norm3_g[l][None, :], wra, wrb, rb,
                                   tri_mid, S)

        counts = cnt[0, :N_EXPERTS].astype(jnp.int32)
        padded = (counts + TM_MOE - 1) // TM_MOE * TM_MOE
        ends = jnp.cumsum(padded)
        dest = _dest(rinfo, (ends - padded).astype(F32)[:, None])
        dest0, dest1 = dest[0], dest[1]
        first_blk = (jnp.concatenate([jnp.zeros((1,), jnp.int32), ends]) // TM_MOE).astype(jnp.int32)

        xs = _dispatch(dest0, dest1, h3t, xs)
        xs = _experts(first_blk, xs, exp_w1, exp_w3, exp_w2, l)
        x2d = _combine(dest0, dest1, x2, rinfo, xs)
    return x2d.reshape(B, S, D_MODEL)
```

```python
import functools

import jax
import jax.numpy as jnp
import numpy as np
from jax import lax
from jax.experimental import pallas as pl
from jax.experimental.pallas import tpu as pltpu

F32 = jnp.float32
BF16 = jnp.bfloat16

D_MODEL = 1024
HEAD_DIM = 64
ATT_HEADS = 4
ATT_W = ATT_HEADS * HEAD_DIM
SSD_HEADS = 8
SSD_INNER = 512
SSD_STATE = 128
SSD_CONV = 4
SSD_CHUNK = 128
SSD_CONV_CH = 1024
XA_HEADS = 4
XA_HEAD_DIM = 256
MEM_LEN = 256
N_GROUPS = 4
EXPERTS_PER_GROUP = 8
N_EXPERTS = 32
EXPERT_FF = 512
DIL_CONFIGS = ((128, 1), (512, 4), (2048, 16))
ROPE_THETA = 500000.0
ROPE_DIM = 16
EPS = 1e-6
NEG = -1e30
LOG2E = 1.4426950408889634

LANES = 128
N_FGATE = 4
DT_LANE0 = 4
MAIN_W = 3 * ATT_W + 3 * ATT_W + SSD_INNER + SSD_CONV_CH
IN_W = MAIN_W + LANES

TM_IN = 512
TQ = 256
TK = 256
TM_MID = 512
TM_MOE = 256
SSD_ROWS = 512
TD = 1024
TC = 1024
VMEM_LIMIT = 48 * 1024 * 1024


def _cparams(n_axes):
    return pltpu.CompilerParams(dimension_semantics=("arbitrary",) * n_axes,
                                vmem_limit_bytes=VMEM_LIMIT)


def _rms(x, g):
    return x * lax.rsqrt(jnp.mean(x * x, axis=-1, keepdims=True) + EPS) * g


def _split3(x):
    hi = x.astype(BF16)
    r = x - hi.astype(F32)
    mid = r.astype(BF16)
    lo = (r - mid.astype(F32)).astype(BF16)
    return jnp.concatenate([hi, mid, lo], axis=1)


def _dot(a, b):
    return jnp.dot(a, b, preferred_element_type=F32)


def _dot_nt(a, b):
    return lax.dot_general(a, b, (((1,), (1,)), ((), ())), preferred_element_type=F32)


def _sum3(c):
    w = c.shape[1] // 3
    return c[:, 0:w] + c[:, w:2 * w] + c[:, 2 * w:3 * w]


FF_COL = 3 * ATT_W
DQ_COL = FF_COL + N_FGATE
DT_COL = DQ_COL + MAIN_W - 3 * ATT_W
IN_SRC_W = DT_COL + 8
WPREP_ROWS = 256


def _wprep_kernel(w_ref, tail_ref, sh_ref, shs_ref, wa_ref, wb_ref, ws_ref):
    wa_ref[...] = w_ref[:, 0:FF_COL].astype(BF16)
    lane = lax.broadcasted_iota(jnp.int32, tail_ref.shape, 1)
    tail = jnp.where(lane < IN_SRC_W - MAIN_W, tail_ref[...], 0.0).astype(BF16)
    nb = (MAIN_W - FF_COL) // LANES
    for n in range(nb):
        lo = FF_COL + LANES * n
        if n + 1 < nb:
            pair = w_ref[:, lo:lo + 2 * LANES].astype(BF16)
        else:
            pair = jnp.concatenate([w_ref[:, lo:lo + LANES].astype(BF16), tail], axis=1)
        wb_ref[:, LANES * n:LANES * (n + 1)] = _dot(pair, sh_ref[...]).astype(BF16)
    small = jnp.concatenate([w_ref[:, FF_COL:FF_COL + LANES].astype(BF16), tail], axis=1)
    ws_ref[...] = _dot(small, shs_ref[...]).astype(BF16)


def _wprep(w_in, layer, sh, shs):
    row = lambda i: (i, 0)
    c2 = lambda i: (0, 0)
    wb_w = MAIN_W - FF_COL
    return pl.pallas_call(
        _wprep_kernel,
        grid=(D_MODEL // WPREP_ROWS,),
        in_specs=[pl.BlockSpec((None, WPREP_ROWS, MAIN_W), lambda i: (layer, i, 0)),
                  pl.BlockSpec((None, WPREP_ROWS, LANES), lambda i: (layer, i, MAIN_W // LANES)),
                  pl.BlockSpec(sh.shape, c2), pl.BlockSpec(shs.shape, c2)],
        out_specs=[pl.BlockSpec((WPREP_ROWS, FF_COL), row),
                   pl.BlockSpec((WPREP_ROWS, wb_w), row),
                   pl.BlockSpec((WPREP_ROWS, LANES), row)],
        out_shape=[jax.ShapeDtypeStruct((D_MODEL, FF_COL), BF16),
                   jax.ShapeDtypeStruct((D_MODEL, wb_w), BF16),
                   jax.ShapeDtypeStruct((D_MODEL, LANES), BF16)],
        compiler_params=_cparams(1),
        name="wprep",
    )(w_in, w_in, sh, shs)


def _wprep_consts():
    sh = np.zeros((2 * LANES, LANES), np.float32)
    shs = np.zeros((2 * LANES, LANES), np.float32)
    off = DQ_COL - FF_COL
    for j in range(LANES):
        sh[j + off, j] = 1.0
    for k in range(N_FGATE):
        shs[k, k] = 1.0
    for k in range(SSD_HEADS):
        shs[LANES + DT_COL - MAIN_W + k, DT_LANE0 + k] = 1.0
    return jnp.asarray(sh, BF16), jnp.asarray(shs, BF16)


def _inproj_kernel(x_ref, g_ref, wa_ref, wb_ref, ws_ref, gm_ref, qkg_ref, rope_ref, sb_ref,
                   fq_ref, fkt_ref, fv_ref, dq_ref, dkt_ref, dv_ref, z_ref, xbc_ref, sm_ref):
    h = _rms(x_ref[...], g_ref[...]).astype(BF16)
    na = wa_ref.shape[1]

    def proj(a, b):
        if b <= na:
            return _dot(h, wa_ref[:, a:b])
        return _dot(h, wb_ref[:, a - na:b - na])

    def head_norm(a, idx):
        ssq = _dot((a * a).astype(BF16), gm_ref[...])
        return a * lax.rsqrt(ssq * (1.0 / HEAD_DIM) + EPS) * qkg_ref[idx]

    def rope(a):
        return (a * rope_ref[0] + pltpu.roll(a, ATT_W - ROPE_DIM // 2, 1) * rope_ref[1]
                + pltpu.roll(a, ROPE_DIM // 2, 1) * rope_ref[2])

    scale = HEAD_DIM ** -0.5 * LOG2E
    fq_ref[...] = (head_norm(proj(0, 256), 0) * scale).astype(BF16)
    fkt_ref[0] = head_norm(proj(256, 512), 1).T.astype(BF16)
    fv_ref[...] = proj(512, 768).astype(BF16)
    dq_ref[...] = (rope(head_norm(proj(768, 1024), 2)) * scale).astype(BF16)
    dkt_ref[0] = rope(head_norm(proj(1024, 1280), 3)).T.astype(BF16)
    dv_ref[...] = proj(1280, 1536).astype(BF16)
    z_ref[...] = proj(1536, 2048)
    xbc_ref[...] = proj(2048, MAIN_W)
    v = _dot(h, ws_ref[...]) + sb_ref[...]
    e = jnp.log1p(jnp.exp(-jnp.abs(v)))
    lane = lax.broadcasted_iota(jnp.int32, v.shape, 1)
    sm_ref[...] = jnp.where(lane < N_FGATE, jnp.minimum(v, 0.0) - e, jnp.maximum(v, 0.0) + e)


def _inproj(x2d, g, wa, wb, ws, gm, qkg, rope, sb, B, S):
    T = x2d.shape[0]
    nst = S // TM_IN
    row = lambda i: (i, 0)
    const2 = lambda i: (0, 0)
    tr = lambda i: (i // nst, 0, i % nst)
    out_shape = [
        jax.ShapeDtypeStruct((T, ATT_W), BF16),
        jax.ShapeDtypeStruct((B, ATT_W, S), BF16),
        jax.ShapeDtypeStruct((T, ATT_W), BF16),
        jax.ShapeDtypeStruct((T, ATT_W), BF16),
        jax.ShapeDtypeStruct((B, ATT_W, S), BF16),
        jax.ShapeDtypeStruct((T, ATT_W), BF16),
        jax.ShapeDtypeStruct((T, SSD_INNER), F32),
        jax.ShapeDtypeStruct((T, SSD_CONV_CH), F32),
        jax.ShapeDtypeStruct((T, LANES), F32),
    ]
    att = pl.BlockSpec((TM_IN, ATT_W), row)
    att_t = pl.BlockSpec((1, ATT_W, TM_IN), tr)
    return pl.pallas_call(
        _inproj_kernel,
        grid=(T // TM_IN,),
        in_specs=[
            pl.BlockSpec((TM_IN, D_MODEL), row),
            pl.BlockSpec((1, D_MODEL), const2),
            pl.BlockSpec(wa.shape, const2),
            pl.BlockSpec(wb.shape, const2),
            pl.BlockSpec(ws.shape, const2),
            pl.BlockSpec((ATT_W, ATT_W), const2),
            pl.BlockSpec((4, 1, ATT_W), lambda i: (0, 0, 0)),
            pl.BlockSpec((3, TM_IN, ATT_W), lambda i: (0, i % nst, 0)),
            pl.BlockSpec((1, LANES), const2),
        ],
        out_specs=[att, att_t, att, att, att_t, att,
                   pl.BlockSpec((TM_IN, SSD_INNER), row),
                   pl.BlockSpec((TM_IN, SSD_CONV_CH), row),
                   pl.BlockSpec((TM_IN, LANES), row)],
        out_shape=out_shape,
        compiler_params=_cparams(1),
        name="inproj",
    )(x2d, g, wa, wb, ws, gm, qkg, rope, sb)


def _fox_scan_kernel(sm_ref, tri_ref, pq_ref, pk_ref, oq_ref, ok_ref, augq_ref, augkt_ref):
    S = sm_ref.shape[1]
    blk = tri_ref.shape[0]
    carry = jnp.zeros((1, LANES), F32)
    for b in range(S // blk):
        rows = slice(b * blk, (b + 1) * blk)
        c = _sum3(_dot(tri_ref[...], _split3(sm_ref[0, rows, :]))) + carry
        carry = c[blk - 1:blk, :]
        c3 = _split3(c * LOG2E)
        augq_ref[0, rows, :] = (_dot(c3, pq_ref[...]) + oq_ref[...]).astype(BF16)
        ak = _dot(c3, pk_ref[...]) + ok_ref[...]
        for h in range(ATT_HEADS):
            cols = slice(LANES * h, LANES * (h + 1))
            augkt_ref[0, cols, rows] = ak[:, cols].T.astype(BF16)


def _fox_scan(small, consts, B, S):
    tri, pq, pk, oq, ok = consts
    c2 = lambda b: (0, 0)
    return pl.pallas_call(
        _fox_scan_kernel,
        grid=(B,),
        in_specs=[
            pl.BlockSpec((1, S, LANES), lambda b: (b, 0, 0)),
            pl.BlockSpec(tri.shape, c2),
            pl.BlockSpec(pq.shape, c2),
            pl.BlockSpec(pk.shape, c2),
            pl.BlockSpec(oq.shape, c2),
            pl.BlockSpec(ok.shape, c2),
        ],
        out_specs=[pl.BlockSpec((1, S, ATT_HEADS * LANES), lambda b: (b, 0, 0)),
                   pl.BlockSpec((1, ATT_HEADS * LANES, S), lambda b: (b, 0, 0))],
        out_shape=[jax.ShapeDtypeStruct((B, S, ATT_HEADS * LANES), BF16),
                   jax.ShapeDtypeStruct((B, ATT_HEADS * LANES, S), BF16)],
        compiler_params=_cparams(1),
        name="fox_scan",
    )(small.reshape(B, S, LANES), tri, pq, pk, oq, ok)


def _attn_kernel(*refs, fox, nk):
    if fox:
        q_ref, kt_ref, v_ref, lm_ref, augq_ref, augkt_ref, o_ref, kt_scr, v_scr = refs
    else:
        q_ref, kt_ref, v_ref, lm_ref, o_ref, kt_scr, v_scr = refs
    qi = pl.program_id(1)

    @pl.when(qi == 0)
    def _prep():
        row = lax.broadcasted_iota(jnp.int32, (LANES, TK), 0)
        lane = lax.broadcasted_iota(jnp.int32, (TK, LANES), 1)
        for h in range(ATT_HEADS):
            p, mem = divmod(h, 2)
            pair = slice(LANES * p, LANES * (p + 1))
            for j in range(nk):
                keys = slice(j * TK, (j + 1) * TK)
                kd = kt_ref[0, pair, keys]
                if fox:
                    other = augkt_ref[0, LANES * h:LANES * (h + 1), keys]
                else:
                    other = jnp.zeros_like(kd)
                kt_scr[h, j] = jnp.where((row >> 6) == mem, kd, other)
                vd = v_ref[keys, pair]
                v_scr[h, j] = jnp.where((lane >> 6) == mem, vd, jnp.ones_like(vd))

    qlane = lax.broadcasted_iota(jnp.int32, (TQ, LANES), 1)
    qas = []
    for h in range(ATT_HEADS):
        p, mem = divmod(h, 2)
        qd = q_ref[:, LANES * p:LANES * (p + 1)]
        if fox:
            other = augq_ref[:, LANES * h:LANES * (h + 1)]
        else:
            other = jnp.zeros_like(qd)
        qas.append(jnp.where((qlane >> 6) == mem, qd, other))

    def step(j, carry, table):
        new = []
        for h in range(ATT_HEADS):
            m, acc = carry[h]
            s = _dot(qas[h], kt_scr[h, j])
            if table is not None:
                s = s + table
            m_new = jnp.maximum(m, jnp.max(s, axis=1, keepdims=True))
            alpha = jnp.exp2(m - m_new)
            pr = jnp.exp2(s - m_new).astype(BF16)
            new.append((m_new, alpha * acc + _dot(pr, v_scr[h, j])))
        return tuple(new)

    init = tuple((jnp.full((TQ, 1), NEG, F32), jnp.zeros((TQ, LANES), F32))
                 for _ in range(ATT_HEADS))

    if fox:
        table = lambda j: lm_ref[jnp.minimum(qi - j, 1)]
    else:
        table = lambda j: lm_ref[qi - j]

    def steps(j0, count, c):
        for u in range(count):
            c = step(j0 + u, c, table(j0 + u))
        return c

    n = qi + 1
    carry = lax.fori_loop(0, n // 4, lambda p, c: steps(4 * p, 4, c), init)
    carry = lax.cond(n % 4 >= 2, lambda c: steps(n - n % 4, 2, c), lambda c: c, carry)
    carry = lax.cond(n % 2 == 1, lambda c: steps(n - 1, 1, c), lambda c: c, carry)
    outs = [acc / pltpu.roll(acc, HEAD_DIM, 1) for _, acc in carry]
    for p in range(ATT_HEADS // 2):
        o_ref[:, LANES * p:LANES * (p + 1)] = jnp.where(
            (qlane >> 6) == 0, outs[2 * p], outs[2 * p + 1]).astype(BF16)


def _attention(q, kt, v, lm, aug, B, S):
    fox = aug is not None
    nq, nk = S // TQ, S // TK
    in_specs = [
        pl.BlockSpec((TQ, ATT_W), lambda b, i: (b * nq + i, 0)),
        pl.BlockSpec((1, ATT_W, S), lambda b, i: (b, 0, 0)),
        pl.BlockSpec((S, ATT_W), lambda b, i: (b, 0)),
        pl.BlockSpec(lm.shape, lambda b, i: (0, 0, 0)),
    ]
    args = [q, kt, v, lm]
    if fox:
        in_specs += [pl.BlockSpec((TQ, ATT_HEADS * LANES), lambda b, i: (b * nq + i, 0)),
                     pl.BlockSpec((1, ATT_HEADS * LANES, S), lambda b, i: (b, 0, 0))]
        args += [aug[0].reshape(B * S, ATT_HEADS * LANES), aug[1]]
    return pl.pallas_call(
        functools.partial(_attn_kernel, fox=fox, nk=nk),
        grid=(B, nq),
        in_specs=in_specs,
        out_specs=pl.BlockSpec((TQ, ATT_W), lambda b, i: (b * nq + i, 0)),
        out_shape=jax.ShapeDtypeStruct((B * S, ATT_W), BF16),
        scratch_shapes=[pltpu.VMEM((ATT_HEADS, nk, LANES, TK), BF16),
                        pltpu.VMEM((ATT_HEADS, nk, TK, LANES), BF16)],
        compiler_params=_cparams(2),
        name="fox_attn" if fox else "dil_attn",
    )(*args)


def _ssd_kernel(xbc_ref, z_ref, sm_ref, cw_ref, cb_ref, arow_ref, dx_ref, ng_ref,
                tri_ref, pexp_ref, pq_ref, pk_ref, oq_ref, ok_ref, o_ref, buf, state):
    rows = xbc_ref.shape[0]

    @pl.when(pl.program_id(1) == 0)
    def _reset():
        buf[0:8, :] = jnp.zeros((8, SSD_CONV_CH), F32)
        state[...] = jnp.zeros(state.shape, F32)

    xb = xbc_ref[...]
    buf[8:8 + rows, :] = xb
    conv = cb_ref[...]
    for k in range(SSD_CONV):
        off = 8 - (SSD_CONV - 1) + k
        conv = conv + cw_ref[k:k + 1, :] * buf[off:off + rows, :]
    buf[0:8, :] = xb[rows - 8:rows, :]
    act = conv * jax.nn.sigmoid(conv)

    for c in range(rows // SSD_CHUNK):
        rs = slice(SSD_CHUNK * c, SSD_CHUNK * (c + 1))
        _ssd_chunk(act[rs, :], z_ref[rs, :], sm_ref[rs, :], arow_ref, dx_ref, ng_ref, tri_ref, pexp_ref,
                   pq_ref, pk_ref, oq_ref, ok_ref, o_ref.at[rs], state)


def _ssd_chunk(act, zz, dt, arow_ref, dx_ref, ng_ref, tri_ref, pexp_ref, pq_ref, pk_ref, oq_ref, ok_ref,
               o_ref, state):
    Q = SSD_CHUNK
    xs = act[:, 0:SSD_INNER]
    bm = act[:, SSD_INNER:SSD_INNER + 2 * SSD_STATE]
    cm = act[:, SSD_INNER + 2 * SSD_STATE:]

    acs = _sum3(_dot(tri_ref[...], _split3(dt * arow_ref[...])))
    acs3 = _split3(acs)
    ax = _dot(acs3, pexp_ref[...])
    dtx = _dot(_split3(dt), pexp_ref[...])
    last = ax[Q - 1:Q, :]
    ea = jnp.exp(ax)
    cdec = jnp.exp(last)
    xc = xs * dtx
    xcb = xc.astype(BF16)
    xcd = (xc * jnp.exp(last - ax)).astype(BF16)
    uq = (_dot(acs3, pq_ref[...]) + oq_ref[...]).astype(BF16)
    uk = (_dot(acs3, pk_ref[...]) + ok_ref[...]).astype(BF16)

    tril = (lax.broadcasted_iota(jnp.int32, (Q, Q), 0) >= lax.broadcasted_iota(jnp.int32, (Q, Q), 1))
    first = lax.broadcasted_iota(jnp.int32, (Q, LANES), 1) < HEAD_DIM
    ys = []
    for g in range(2):
        gs = slice(SSD_STATE * g, SSD_STATE * (g + 1))
        bg = bm[:, gs]
        cg = cm[:, gs].astype(BF16)
        cbm = _dot_nt(cg, bg.astype(BF16))
        bgt = bg.T.astype(BF16)
        for pp in range(2):
            p = 2 * g + pp
            ps = slice(LANES * p, LANES * (p + 1))
            ms = []
            for mem in range(2):
                hs = slice(LANES * (2 * p + mem), LANES * (2 * p + mem + 1))
                dm = _dot_nt(uq[:, hs], uk[:, hs])
                ms.append((cbm * jnp.exp(jnp.where(tril, dm, NEG))).astype(BF16))
            xp = xcb[:, ps]
            zero = jnp.zeros_like(xp)
            xcat = jnp.concatenate([jnp.where(first, xp, zero), jnp.where(first, zero, xp)], axis=0)
            y_diag = _dot(jnp.concatenate(ms, axis=1), xcat)
            st = state[p]
            y_off = _dot(cg, st.astype(BF16)) * ea[:, ps]
            state[p] = cdec[:, ps] * st + _dot(bgt, xcd[:, ps])
            ys.append(y_diag + y_off + xs[:, ps] * dx_ref[:, ps])
    y = jnp.concatenate(ys, axis=1)
    o_ref[...] = _rms(y * (zz * jax.nn.sigmoid(zz)), ng_ref[...]).astype(BF16)


def _ssd(xbc, z, small, cw, cb, arow, dx, ng, consts, B, S):
    nc = S // SSD_ROWS
    row = lambda b, c: (b * nc + c, 0)
    c2 = lambda b, c: (0, 0)
    full = lambda a: pl.BlockSpec(a.shape, c2)
    return pl.pallas_call(
        _ssd_kernel,
        grid=(B, nc),
        in_specs=[pl.BlockSpec((SSD_ROWS, SSD_CONV_CH), row),
                  pl.BlockSpec((SSD_ROWS, SSD_INNER), row),
                  pl.BlockSpec((SSD_ROWS, LANES), row),
                  full(cw), full(cb), full(arow), full(dx), full(ng)] + [full(a) for a in consts],
        out_specs=pl.BlockSpec((SSD_ROWS, SSD_INNER), row),
        out_shape=jax.ShapeDtypeStruct((B * S, SSD_INNER), BF16),
        scratch_shapes=[pltpu.VMEM((8 + SSD_ROWS, SSD_CONV_CH), F32),
                        pltpu.VMEM((SSD_HEADS // 2, SSD_STATE, LANES), F32)],
        compiler_params=_cparams(2),
        name="ssd",
    )(xbc, z, small, cw, cb, arow, dx, ng, *consts)


def _kv_kernel(mem_ref, g_ref, w_ref, kg_ref, kt_ref, v_ref):
    m = _rms(mem_ref[0], g_ref[...]).astype(BF16)
    kv = _dot(m, w_ref[...])
    for h in range(XA_HEADS):
        hs = slice(XA_HEAD_DIM * h, XA_HEAD_DIM * (h + 1))
        kt_ref[0, hs, :] = _rms(kv[:, hs], kg_ref[...]).T.astype(BF16)
    v_ref[0] = kv[:, D_MODEL:].astype(BF16)


def _kv(mem, g, w, kg):
    B = mem.shape[0]
    c2 = lambda b: (0, 0)
    return pl.pallas_call(
        _kv_kernel,
        grid=(B,),
        in_specs=[pl.BlockSpec((1, MEM_LEN, D_MODEL), lambda b: (b, 0, 0)),
                  pl.BlockSpec((1, D_MODEL), c2),
                  pl.BlockSpec((D_MODEL, 2 * D_MODEL), c2),
                  pl.BlockSpec((1, XA_HEAD_DIM), c2)],
        out_specs=[pl.BlockSpec((1, D_MODEL, MEM_LEN), lambda b: (b, 0, 0)),
                   pl.BlockSpec((1, MEM_LEN, D_MODEL), lambda b: (b, 0, 0))],
        out_shape=[jax.ShapeDtypeStruct((B, D_MODEL, MEM_LEN), BF16),
                   jax.ShapeDtypeStruct((B, MEM_LEN, D_MODEL), BF16)],
        compiler_params=_cparams(1),
        name="mem_kv",
    )(mem, g, w, kg)


ROW_SUB = D_MODEL // LANES


def _chunk(n, c):
    return pl.ds(c, n, stride=ROW_SUB)


def _to_row_tiles(ref, x):
    n = x.shape[0]
    for c in range(ROW_SUB):
        ref[_chunk(n, c), :] = x[:, LANES * c:LANES * (c + 1)]


def _from_row_tiles(ref):
    n = ref.shape[0] // ROW_SUB
    return jnp.concatenate([ref[_chunk(n, c), :] for c in range(ROW_SUB)], axis=1)


def _row_copy(src, dst, sem, src_row, dst_row):
    return pltpu.make_async_copy(
        src.at[pl.ds(pl.multiple_of(src_row * ROW_SUB, ROW_SUB), ROW_SUB)],
        dst.at[pl.ds(pl.multiple_of(dst_row * ROW_SUB, ROW_SUB), ROW_SUB)], sem)


def _mid_kernel(x_ref, of_ref, od_ref, os_ref, wo_ref, g2_ref, wq_ref, qg_ref, kt_ref, v_ref,
                wxo_ref, g3_ref, wra_ref, wrb_ref, rb_ref, tri_ref,
                x2_ref, h3_ref, ri_ref, cnt_ref, run):
    @pl.when(pl.program_id(0) == 0)
    def _reset():
        run[...] = jnp.zeros(run.shape, F32)

    x1 = (x_ref[...] + _dot(of_ref[...], wo_ref[0:ATT_W, :])
          + _dot(od_ref[...], wo_ref[ATT_W:2 * ATT_W, :])
          + _dot(os_ref[...], wo_ref[2 * ATT_W:, :]))

    q = _dot(_rms(x1, g2_ref[...]).astype(BF16), wq_ref[...])
    heads = []
    for h in range(XA_HEADS):
        hs = slice(XA_HEAD_DIM * h, XA_HEAD_DIM * (h + 1))
        qn = (_rms(q[:, hs], qg_ref[...]) * XA_HEAD_DIM ** -0.5).astype(BF16)
        s = _dot(qn, kt_ref[0, hs, :])
        e = jnp.exp(s - jnp.max(s, axis=1, keepdims=True))
        o = _dot(e.astype(BF16), v_ref[0, :, hs]) / jnp.sum(e, axis=1, keepdims=True)
        heads.append(o.astype(BF16))
    x2 = x1 + _dot(jnp.concatenate(heads, axis=1), wxo_ref[...])
    x2_ref[...] = x2

    h3 = _rms(x2, g3_ref[...])
    _to_row_tiles(h3_ref, h3)

    hi = h3.astype(BF16)
    mid = (h3 - hi.astype(F32)).astype(BF16)
    both = _dot(hi, wra_ref[...])
    logits = _dot(mid, wrb_ref[...]) + both[:, LANES:] + both[:, 0:LANES] + rb_ref[...]

    lane = lax.broadcasted_iota(jnp.int32, logits.shape, 1)
    lanef = lane.astype(F32)
    big = float(LANES)

    def first_max(vals):
        top = jnp.max(vals, axis=1, keepdims=True)
        return top, jnp.min(jnp.where(vals == top, lanef, big), axis=1, keepdims=True)

    gl = jnp.where(lane < N_GROUPS, logits, NEG)
    gmax, gsel = first_max(gl)
    ggate = 1.0 / jnp.sum(jnp.exp(gl - gmax), axis=1, keepdims=True)
    grp = ((lane - N_GROUPS) >> 3).astype(F32)
    el = jnp.where(grp == gsel, logits, NEG)
    v1, i1 = first_max(el)
    v2, i2 = first_max(jnp.where(lanef == i1, NEG, el))
    t = jnp.exp(v2 - v1)
    p1 = 1.0 / (1.0 + t)
    e1 = i1 - N_GROUPS
    e2 = i2 - N_GROUPS

    hit1 = lanef == e1
    hit2 = lanef == e2
    onehot = jnp.where(hit1 | hit2, 1.0, 0.0)
    before = _dot(tri_ref[...], onehot.astype(BF16)) + run[...]
    r1 = jnp.sum(jnp.where(hit1, before, 0.0), axis=1, keepdims=True)
    r2 = jnp.sum(jnp.where(hit2, before, 0.0), axis=1, keepdims=True)
    run[...] = run[...] + jnp.sum(onehot, axis=0, keepdims=True)

    cols = (e1, e2, p1 * ggate, t * p1 * ggate, r1, r2)
    info = jnp.zeros(logits.shape, F32)
    for k, col in enumerate(cols):
        info = jnp.where(lane == k, col, info)
    ri_ref[...] = info
    cnt_ref[...] = jnp.broadcast_to(run[...], cnt_ref.shape)


def _mid(x2d, o_fox, o_dil, o_ssd, wo, g2, wq, qg, kt, v, wxo, g3, wra, wrb, rb, tri, S):
    T = x2d.shape[0]
    npb = S // TM_MID
    row = lambda i: (i, 0)
    c2 = lambda i: (0, 0)
    return pl.pallas_call(
        _mid_kernel,
        grid=(T // TM_MID,),
        in_specs=[pl.BlockSpec((TM_MID, D_MODEL), row),
                  pl.BlockSpec((TM_MID, ATT_W), row),
                  pl.BlockSpec((TM_MID, ATT_W), row),
                  pl.BlockSpec((TM_MID, SSD_INNER), row),
                  pl.BlockSpec((D_MODEL, D_MODEL), c2),
                  pl.BlockSpec((1, D_MODEL), c2),
                  pl.BlockSpec((D_MODEL, D_MODEL), c2),
                  pl.BlockSpec((1, XA_HEAD_DIM), c2),
                  pl.BlockSpec((1, D_MODEL, MEM_LEN), lambda i: (i // npb, 0, 0)),
                  pl.BlockSpec((1, MEM_LEN, D_MODEL), lambda i: (i // npb, 0, 0)),
                  pl.BlockSpec((D_MODEL, D_MODEL), c2),
                  pl.BlockSpec((1, D_MODEL), c2),
                  pl.BlockSpec((D_MODEL, 2 * LANES), c2),
                  pl.BlockSpec((D_MODEL, LANES), c2),
                  pl.BlockSpec((1, LANES), c2),
                  pl.BlockSpec((TM_MID, TM_MID), c2)],
        out_specs=[pl.BlockSpec((TM_MID, D_MODEL), row),
                   pl.BlockSpec((TM_MID * ROW_SUB, LANES), row),
                   pl.BlockSpec((TM_MID, LANES), row),
                   pl.BlockSpec((8, LANES), c2)],
        out_shape=[jax.ShapeDtypeStruct((T, D_MODEL), F32),
                   jax.ShapeDtypeStruct((T * ROW_SUB, LANES), F32),
                   jax.ShapeDtypeStruct((T, LANES), F32),
                   jax.ShapeDtypeStruct((8, LANES), F32)],
        scratch_shapes=[pltpu.VMEM((1, LANES), F32)],
        compiler_params=_cparams(1),
        name="mid",
    )(x2d, o_fox, o_dil, o_ssd, wo, g2, wq, qg, kt, v, wxo, g3, wra, wrb, rb, tri)


def _dest_kernel(ri_ref, st_ref, o_ref):
    info = ri_ref[...].T
    n = info.shape[1]
    expert = lax.broadcasted_iota(jnp.int32, (N_EXPERTS, n), 0).astype(F32)
    row = lax.broadcasted_iota(jnp.int32, (8, n), 0)
    out = jnp.zeros((8, n), F32)
    for k in range(2):
        start = jnp.sum(jnp.where(expert == info[k:k + 1, :], st_ref[...], 0.0), axis=0, keepdims=True)
        out = jnp.where(row == k, start + info[4 + k:5 + k, :], out)
    o_ref[...] = out.astype(jnp.int32)


def _dest(rinfo, starts):
    T = rinfo.shape[0]
    return pl.pallas_call(
        _dest_kernel,
        grid=(T // TD,),
        in_specs=[pl.BlockSpec((TD, LANES), lambda i: (i, 0)),
                  pl.BlockSpec((N_EXPERTS, 1), lambda i: (0, 0))],
        out_specs=pl.BlockSpec((8, TD), lambda i: (0, i)),
        out_shape=jax.ShapeDtypeStruct((8, T), jnp.int32),
        compiler_params=_cparams(1),
        name="moe_dest",
    )(rinfo, starts)


def _dispatch_kernel(d0_ref, d1_ref, h_ref, xs_in, xs_out, sem):
    del xs_in

    def issue(r, carry):
        _row_copy(h_ref, xs_out, sem, r, d0_ref[r]).start(priority=0)
        _row_copy(h_ref, xs_out, sem, r, d1_ref[r]).start(priority=1)
        return carry

    def drain(r, carry):
        _row_copy(h_ref, xs_out, sem, 0, 0).wait()
        _row_copy(h_ref, xs_out, sem, 0, 0).wait()
        return carry

    lax.fori_loop(0, TD, issue, 0, unroll=8)
    lax.fori_loop(0, TD, drain, 0, unroll=8)


def _dispatch(dest0, dest1, h3t, xs_init):
    T = h3t.shape[0] // ROW_SUB
    idx = pl.BlockSpec((TD,), lambda i: (i,), memory_space=pltpu.SMEM)
    return pl.pallas_call(
        _dispatch_kernel,
        grid=(T // TD,),
        in_specs=[idx, idx,
                  pl.BlockSpec((TD * ROW_SUB, LANES), lambda i: (i, 0)),
                  pl.BlockSpec(memory_space=pl.ANY)],
        out_specs=pl.BlockSpec(memory_space=pl.ANY),
        out_shape=jax.ShapeDtypeStruct(xs_init.shape, F32),
        scratch_shapes=[pltpu.SemaphoreType.DMA],
        input_output_aliases={3: 0},
        compiler_params=_cparams(1),
        name="moe_dispatch",
    )(dest0, dest1, h3t, xs_init)


def _expert_kernel(fb_ref, xs_hbm, w1_ref, w3_ref, w2_ref, y_hbm, w1s, w3s, w2s, xin, yout, isem, osem):
    e = pl.program_id(0)
    n_used = fb_ref[N_EXPERTS]
    blk = TM_MOE * ROW_SUB

    def rows(g):
        return pl.ds(pl.multiple_of(g * blk, blk), blk)

    def in_copy(g, slot):
        return pltpu.make_async_copy(xs_hbm.at[rows(g)], xin.at[slot], isem.at[slot])

    def out_copy(g, slot):
        return pltpu.make_async_copy(yout.at[slot], y_hbm.at[rows(g)], osem.at[slot])

    @pl.when((e == 0) & (n_used > 0))
    def _prime():
        in_copy(0, 0).start(priority=1)

    w1s[...] = w1_ref[...].astype(BF16)
    w3s[...] = w3_ref[...].astype(BF16)
    w2s[...] = w2_ref[...].astype(BF16)

    def block(g, carry):
        slot = g % 2

        @pl.when(g + 1 < n_used)
        def _prefetch():
            in_copy(g + 1, 1 - slot).start(priority=1)

        in_copy(g, slot).wait()

        @pl.when(g >= 2)
        def _free_out_slot():
            out_copy(g - 2, slot).wait()

        xb = _from_row_tiles(xin.at[slot]).astype(BF16)
        a = _dot(xb, w1s[...])
        b = _dot(xb, w3s[...])
        _to_row_tiles(yout.at[slot], _dot((a * jax.nn.sigmoid(a) * b).astype(BF16), w2s[...]))
        out_copy(g, slot).start(priority=1)
        return carry

    lax.fori_loop(fb_ref[e], fb_ref[e + 1], block, 0)

    @pl.when(e == pl.num_programs(0) - 1)
    def _drain():
        for back in (2, 1):
            @pl.when(n_used >= back)
            def _wait():
                out_copy(n_used - back, (n_used - back) % 2).wait()


def _experts(first_blk, xs, w1, w3, w2, layer):
    wmap = lambda e, fb: (layer, e, 0, 0)
    blk = TM_MOE * ROW_SUB
    return pl.pallas_call(
        _expert_kernel,
        grid_spec=pltpu.PrefetchScalarGridSpec(
            num_scalar_prefetch=1,
            grid=(N_EXPERTS,),
            in_specs=[pl.BlockSpec(memory_space=pl.ANY),
                      pl.BlockSpec((None, None, D_MODEL, EXPERT_FF), wmap),
                      pl.BlockSpec((None, None, D_MODEL, EXPERT_FF), wmap),
                      pl.BlockSpec((None, None, EXPERT_FF, D_MODEL), wmap)],
            out_specs=pl.BlockSpec(memory_space=pl.ANY),
            scratch_shapes=[pltpu.VMEM((D_MODEL, EXPERT_FF), BF16),
                            pltpu.VMEM((D_MODEL, EXPERT_FF), BF16),
                            pltpu.VMEM((EXPERT_FF, D_MODEL), BF16),
                            pltpu.VMEM((2, blk, LANES), F32),
                            pltpu.VMEM((2, blk, LANES), F32),
                            pltpu.SemaphoreType.DMA((2,)),
                            pltpu.SemaphoreType.DMA((2,))]),
        out_shape=jax.ShapeDtypeStruct(xs.shape, F32),
        input_output_aliases={1: 0},
        compiler_params=_cparams(1),
        name="moe_experts",
    )(first_blk, xs, w1, w3, w2)


def _combine_kernel(d0_ref, d1_ref, d0n_ref, d1n_ref, x2_ref, ri_ref, y_hbm, o_ref, buf, sem):
    i = pl.program_id(0)
    slot = i % 2

    def gather(d0, d1, s):
        def issue(r, carry):
            _row_copy(y_hbm, buf.at[s, 0], sem.at[s], d0[r], r).start(priority=0)
            _row_copy(y_hbm, buf.at[s, 1], sem.at[s], d1[r], r).start(priority=1)
            return carry
        lax.fori_loop(0, TC, issue, 0, unroll=8)

    @pl.when(i == 0)
    def _first():
        gather(d0_ref, d1_ref, 0)

    @pl.when(i + 1 < pl.num_programs(0))
    def _next():
        gather(d0n_ref, d1n_ref, 1 - slot)

    def drain(r, carry):
        _row_copy(y_hbm, buf.at[slot, 0], sem.at[slot], 0, 0).wait()
        _row_copy(y_hbm, buf.at[slot, 1], sem.at[slot], 0, 0).wait()
        return carry

    lax.fori_loop(0, TC, drain, 0, unroll=8)
    info = ri_ref[...]
    g0 = info[:, 2:3]
    g1 = info[:, 3:4]
    for c in range(D_MODEL // LANES):
        cols = slice(LANES * c, LANES * (c + 1))
        o_ref[:, cols] = (x2_ref[:, cols] + g0 * buf[slot, 0, _chunk(TC, c), :]
                          + g1 * buf[slot, 1, _chunk(TC, c), :])


def _combine(dest0, dest1, x2, rinfo, ybuf):
    T = x2.shape[0]
    row = lambda i: (i, 0)
    nsteps = T // TC
    idx = pl.BlockSpec((TC,), lambda i: (i,), memory_space=pltpu.SMEM)
    nxt = pl.BlockSpec((TC,), lambda i: (jnp.minimum(i + 1, nsteps - 1),), memory_space=pltpu.SMEM)
    return pl.pallas_call(
        _combine_kernel,
        grid=(nsteps,),
        in_specs=[idx, idx, nxt, nxt,
                  pl.BlockSpec((TC, D_MODEL), row),
                  pl.BlockSpec((TC, LANES), row),
                  pl.BlockSpec(memory_space=pl.ANY)],
        out_specs=pl.BlockSpec((TC, D_MODEL), row),
        out_shape=jax.ShapeDtypeStruct((T, D_MODEL), F32),
        scratch_shapes=[pltpu.VMEM((2, 2, TC * ROW_SUB, LANES), F32),
                        pltpu.SemaphoreType.DMA((2,))],
        compiler_params=_cparams(1),
        name="moe_combine",
    )(dest0, dest1, dest0, dest1, x2, rinfo, ybuf)


def _tri(n, strict):
    return jnp.asarray(np.tril(np.ones((n, n), np.float32), -1 if strict else 0), BF16)


def _rope_tables(S):
    half = ROPE_DIM // 2
    inv = jnp.power(ROPE_THETA, -2.0 * jnp.arange(half, dtype=F32) / ROPE_DIM)
    ang = jnp.arange(S).astype(F32)[:, None] * inv[None, :]
    cos, sin = jnp.cos(ang), jnp.sin(ang)
    d = np.arange(ATT_W) % HEAD_DIM
    idx = d % half
    c = jnp.where(d < ROPE_DIM, cos[:, idx], 1.0)
    s1 = jnp.where(d < half, -sin[:, idx], 0.0)
    s2 = jnp.where((d >= half) & (d < ROPE_DIM), sin[:, idx], 0.0)
    return jnp.stack([c, s1, s2]).astype(F32)


def _fox_consts():
    pq = np.zeros((3 * LANES, ATT_HEADS * LANES), np.float32)
    pk = np.zeros_like(pq)
    oq = np.zeros((1, ATT_HEADS * LANES), np.float32)
    ok = np.zeros_like(oq)
    for h in range(ATT_HEADS):
        off = LANES * h + (HEAD_DIM if h % 2 == 0 else 0)
        for k in range(3):
            pq[k * LANES + h, off + k] = 1.0
            oq[0, off + 3 + k] = 1.0
            pk[k * LANES + h, off + 3 + k] = -1.0
            ok[0, off + k] = 1.0
    return (_tri(256, False), jnp.asarray(pq, BF16), jnp.asarray(pk, BF16),
            jnp.asarray(oq), jnp.asarray(ok))


def _ssd_consts():
    pexp = np.zeros((3 * LANES, SSD_INNER), np.float32)
    pq = np.zeros((3 * LANES, SSD_HEADS * LANES), np.float32)
    pk = np.zeros_like(pq)
    oq = np.zeros((1, SSD_HEADS * LANES), np.float32)
    ok = np.zeros_like(oq)
    for h in range(SSD_HEADS):
        for k in range(3):
            src = k * LANES + DT_LANE0 + h
            pexp[src, HEAD_DIM * h:HEAD_DIM * (h + 1)] = 1.0
            pq[src, LANES * h + k] = 1.0
            oq[0, LANES * h + 3 + k] = 1.0
            pk[src, LANES * h + 3 + k] = -1.0
            ok[0, LANES * h + k] = 1.0
    return (_tri(SSD_CHUNK, False), jnp.asarray(pexp, BF16), jnp.asarray(pq, BF16),
            jnp.asarray(pk, BF16), jnp.asarray(oq), jnp.asarray(ok))


def _score_tables(S):
    nd = S // TK
    i = np.arange(TQ)[:, None]
    j = np.arange(TK)[None, :]
    causal = np.stack([np.where(i >= j, 0.0, NEG), np.zeros((TQ, TK))]).astype(np.float32)
    dil = np.zeros((nd, TQ, TK), np.float32)
    for d in range(nd):
        delta = d * TK + i - j
        mult = np.zeros((TQ, TK), np.float64)
        for window, step in DIL_CONFIGS:
            mult += (delta >= 0) & (delta <= window) & (delta % step == 0)
        dil[d] = np.where(mult > 0, np.log2(np.maximum(mult, 1.0)), NEG)
    return jnp.asarray(causal), jnp.asarray(dil)


def _group_matrix():
    g = np.arange(ATT_W) // HEAD_DIM
    return jnp.asarray((g[:, None] == g[None, :]).astype(np.float32), BF16)


def _pad_lanes(v, lane0):
    return jnp.zeros((1, LANES), F32).at[0, lane0:lane0 + v.shape[0]].set(v)


def _layer_params(l, w_in, wprep_consts, fox_fgate_b, fox_qn_g, fox_kn_g, dil_qn_g, dil_kn_g,
                  ssd_dt_bias, ssd_A_log, ssd_D, router_wg, router_bg, router_we, router_be):
    w_r = _wprep(w_in, l, *wprep_consts)
    tile4 = lambda g: jnp.tile(g, ATT_HEADS)[None, :]
    qkg = jnp.stack([tile4(fox_qn_g[l]), tile4(fox_kn_g[l]), tile4(dil_qn_g[l]), tile4(dil_kn_g[l])])
    sb = _pad_lanes(fox_fgate_b[l], 0) + _pad_lanes(ssd_dt_bias[l], DT_LANE0)
    arow = _pad_lanes(-jnp.exp(ssd_A_log[l]), DT_LANE0)
    dx = jnp.repeat(ssd_D[l], HEAD_DIM)[None, :]
    wr = jnp.concatenate([router_wg[l], router_we[l],
                          jnp.zeros((D_MODEL, LANES - N_GROUPS - N_EXPERTS), F32)], axis=1)
    wr3 = _split3(wr)
    rb = _pad_lanes(router_bg[l], 0) + _pad_lanes(router_be[l], N_GROUPS)
    return w_r, qkg, sb, arow, dx, wr3[:, 0:2 * LANES], wr3[:, 0:LANES], rb


def kernel(x, mem, norm1_g, w_in, fox_fgate_b, fox_qn_g, fox_kn_g, dil_qn_g, dil_kn_g, ssd_conv_w,
           ssd_conv_b, ssd_dt_bias, ssd_A_log, ssd_D, ssd_norm_g, w_out, norm2_g, mem_norm_g, xa_wq,
           xa_wkv, xa_qn_g, xa_kn_g, xa_wo, norm3_g, router_wg, router_bg, router_we, router_be,
           exp_w1, exp_w3, exp_w2):
    B, S, _ = x.shape
    T = B * S
    depth = w_in.shape[0]
    assert S % TM_IN == 0 and S % TQ == 0 and T % TD == 0 and S >= DIL_CONFIGS[-1][0]

    rope = _rope_tables(S)
    fox_consts = _fox_consts()
    ssd_consts = _ssd_consts()
    causal, dil_tab = _score_tables(S)
    gm = _group_matrix()
    wprep_consts = _wprep_consts()
    tri_mid = _tri(TM_MID, True)
    nblk = (2 * T) // TM_MOE + N_EXPERTS

    x2d = x.reshape(T, D_MODEL)
    xs = jnp.zeros((nblk * TM_MOE * ROW_SUB, LANES), F32)
    for l in range(depth):
        w_r, qkg, sb, arow, dx, wra, wrb, rb = _layer_params(
            l, w_in, wprep_consts, fox_fgate_b, fox_qn_g, fox_kn_g, dil_qn_g, dil_kn_g,
            ssd_dt_bias, ssd_A_log, ssd_D, router_wg, router_bg, router_we, router_be)

        fq, fkt, fv, dq, dkt, dv, z, xbc, small = _inproj(
            x2d, norm1_g[l][None, :], *w_r, gm, qkg, rope, sb, B, S)
        aug = _fox_scan(small, fox_consts, B, S)
        o_fox = _attention(fq, fkt, fv, causal, aug, B, S)
        o_dil = _attention(dq, dkt, dv, dil_tab, None, B, S)
        o_ssd = _ssd(xbc, z, small, ssd_conv_w[l], ssd_conv_b[l][None, :], arow, dx,
                     ssd_norm_g[l][None, :], ssd_consts, B, S)

        kt, v = _kv(mem, mem_norm_g[l][None, :], xa_wkv[l].astype(BF16), xa_kn_g[l][None, :])
        x2, h3t, rinfo, cnt = _mid(x2d, o_fox, o_dil, o_ssd, w_out[l].astype(BF16),
                                   norm2_g[l][None, :], xa_wq[l].astype(BF16), xa_qn_g[l][None, :],
                                   kt, v, xa_wo[l].astype(BF16), norm3_g[l][None, :], wra, wrb, rb,
                                   tri_mid, S)

        counts = cnt[0, :N_EXPERTS].astype(jnp.int32)
        padded = (counts + TM_MOE - 1) // TM_MOE * TM_MOE
        ends = jnp.cumsum(padded)
        dest = _dest(rinfo, (ends - padded).astype(F32)[:, None])
        dest0, dest1 = dest[0], dest[1]
        first_blk = (jnp.concatenate([jnp.zeros((1,), jnp.int32), ends]) // TM_MOE).astype(jnp.int32)

        xs = _dispatch(dest0, dest1, h3t, xs)
        xs = _experts(first_blk, xs, exp_w1, exp_w3, exp_w2, l)
        x2d = _combine(dest0, dest1, x2, rinfo, xs)
    return x2d.reshape(B, S, D_MODEL)
```
